```python
import math
import jax, jax.numpy as jnp
from jax import lax
import numpy as np

D_MODEL = 1024
BATCH = 4
SEQ = 8192
DEPTH = 1

HEAD_DIM = 128
HEADS_PER_GROUP = 4
DILATED_GROUPS = ((128, 1), (512, 4), (2048, 16))
N_GROUPS = 3
N_ATTN_HEADS = N_GROUPS * HEADS_PER_GROUP
ATTN_QKV_WIDTH = N_ATTN_HEADS * HEAD_DIM
ATTN_OUT_WIDTH = HEADS_PER_GROUP * HEAD_DIM
BAND_BLOCK = 128
LRU_WIDTH = D_MODEL
LRU_HEADS = 4
LRU_HEAD_DIM = LRU_WIDTH // LRU_HEADS
CONV_WIDTH = 4
LRU_C = 8.0
D_FF = 2816
REL_BUCKETS = 32
REL_MAX_DISTANCE = 2048
NORM_EPS = 1e-6
N_BRANCHES = 2
W_IN_COLS = 3 * ATTN_QKV_WIDTH + 2 * LRU_WIDTH + N_BRANCHES * D_MODEL

kernel_name = "hybrid_dilated_attn_rglru_macaron"


def rms_norm(x, g):
    xf = x.astype(jnp.float32)
    y = xf * lax.rsqrt(jnp.mean(xf * xf, axis=-1, keepdims=True) + NORM_EPS)
    return (y * g.astype(jnp.float32)).astype(x.dtype)


def swiglu(x, w_gate, w_up, w_down):
    return (jax.nn.silu(x @ w_gate) * (x @ w_up)) @ w_down


def t5_causal_bucket(distance):
    max_exact = REL_BUCKETS // 2
    nf = jnp.maximum(distance, 1).astype(jnp.float32)
    large = max_exact + (jnp.log(nf / max_exact) / math.log(REL_MAX_DISTANCE / max_exact)
                         * (REL_BUCKETS - max_exact)).astype(jnp.int32)
    large = jnp.minimum(large, REL_BUCKETS - 1)
    return jnp.where(distance < max_exact, distance, large)


def dilated_group_attention(q, k, v, bias_table_g, window, dilation):
    B, S, H, E = q.shape
    d = dilation
    L = S // d
    nb = -(-L // BAND_BLOCK)
    Lp = nb * BAND_BLOCK
    m_max = window // d

    def to_blocks(t):
        t = t.reshape(B, L, d, H, E)
        t = jnp.pad(t, ((0, 0), (0, Lp - L), (0, 0), (0, 0), (0, 0)))
        return t.reshape(B, nb, BAND_BLOCK, d, H, E)

    def band(t):
        tb = jnp.pad(to_blocks(t), ((0, 0), (1, 0), (0, 0), (0, 0), (0, 0), (0, 0)))
        return jnp.concatenate([tb[:, :-1], tb[:, 1:]], axis=2)

    qb = to_blocks(q)
    kb = band(k)
    vb = band(v)

    qi = jnp.arange(BAND_BLOCK)[:, None]
    kj = jnp.arange(2 * BAND_BLOCK)[None, :]
    steps = qi + BAND_BLOCK - kj
    key_idx = jnp.arange(nb)[:, None, None] * BAND_BLOCK - BAND_BLOCK + kj[None]
    valid = (steps >= 0) & (steps <= m_max) & (key_idx >= 0)
    bucket = t5_causal_bucket(jnp.maximum(steps, 0) * d)
    bias = jnp.transpose(bias_table_g[bucket], (2, 0, 1)).astype(jnp.float32)

    scale = 1.0 / math.sqrt(E)
    scores = jnp.einsum('bnqphe,bnkphe->bphnqk', qb, kb).astype(jnp.float32) * scale
    scores = scores + bias[None, None, :, None]
    scores = jnp.where(valid[None, None, None], scores, -jnp.inf)
    mx = jnp.max(scores, axis=-1, keepdims=True)
    e = jnp.exp(scores - mx)
    den = jnp.sum(e, axis=-1, keepdims=True)
    o = jnp.einsum('bphnqk,bnkphe->bnqphe', (e / den).astype(v.dtype), vb)
    lse = (mx + jnp.log(den))[..., 0]
    o = o.reshape(B, Lp, d, H, E)[:, :L].reshape(B, S, H, E)
    lse = jnp.transpose(lse, (0, 3, 4, 1, 2)).reshape(B, Lp, d, H)[:, :L].reshape(B, S, H)
    return o, lse


def rg_lru_branch(xr, conv_w, conv_b, w_x, b_x, w_a, b_a, a_param):
    B, S, C = xr.shape
    xc = lax.conv_general_dilated(
        xr, conv_w[:, None, :].astype(xr.dtype), window_strides=(1,),
        padding=((CONV_WIDTH - 1, 0),), dimension_numbers=('NWC', 'WIO', 'NWC'),
        feature_group_count=C) + conv_b
    xh = xc.reshape(B, S, LRU_HEADS, LRU_HEAD_DIM)
    gate_x = jax.nn.sigmoid(jnp.einsum('bshi,hij->bshj', xh, w_x) + b_x).reshape(B, S, C)
    gate_a = jax.nn.sigmoid(jnp.einsum('bshi,hij->bshj', xh, w_a) + b_a).reshape(B, S, C)
    log_a = -LRU_C * gate_a.astype(jnp.float32) * jax.nn.softplus(-a_param.astype(jnp.float32))
    a = jnp.exp(log_a)
    b = jnp.sqrt(-jnp.expm1(2.0 * log_a)) * (gate_x * xc).astype(jnp.float32)

    def combine(left, right):
        a_l, b_l = left
        a_r, b_r = right
        return a_l * a_r, a_r * b_l + b_r

    _, h = lax.associative_scan(combine, (a, b), axis=1)
    return h.astype(xr.dtype)


def setup_inputs(seed: int = 0) -> dict:
    key = jax.random.key(seed)
    ks = jax.random.split(key, 32)
    f32 = jnp.float32

    def w(k, shape, fan_in):
        return jax.random.normal(k, shape, f32) * (fan_in ** -0.5)

    def gain(k):
        return 1.0 + 0.05 * jax.random.normal(k, (DEPTH, D_MODEL), f32)

    a0 = jax.random.uniform(ks[20], (DEPTH, LRU_WIDTH), f32, 0.9, 0.999)
    s = a0 ** (1.0 / LRU_C)
    a_param = jnp.log(s) - jnp.log1p(-s)

    return {
        "x": jax.random.normal(ks[0], (BATCH, SEQ, D_MODEL), f32),
        "ffn1_norm_pre": gain(ks[1]),
        "ffn1_norm_post": gain(ks[2]),
        "ffn1_w_gate": w(ks[3], (DEPTH, D_MODEL, D_FF), D_MODEL),
        "ffn1_w_up": w(ks[4], (DEPTH, D_MODEL, D_FF), D_MODEL),
        "ffn1_w_down": w(ks[5], (DEPTH, D_FF, D_MODEL), D_FF),
        "mix_norm_pre": gain(ks[6]),
        "mix_norm_post": gain(ks[7]),
        "w_in": w(ks[8], (DEPTH, D_MODEL, W_IN_COLS), D_MODEL),
        "rel_bias_table": 0.2 * jax.random.normal(ks[9], (REL_BUCKETS, N_ATTN_HEADS), f32),
        "conv_w": w(ks[10], (DEPTH, CONV_WIDTH, LRU_WIDTH), CONV_WIDTH),
        "conv_b": 0.02 * jax.random.normal(ks[11], (DEPTH, LRU_WIDTH), f32),
        "lru_w_x": w(ks[12], (DEPTH, LRU_HEADS, LRU_HEAD_DIM, LRU_HEAD_DIM), LRU_HEAD_DIM),
        "lru_b_x": 0.02 * jax.random.normal(ks[13], (DEPTH, LRU_HEADS, LRU_HEAD_DIM), f32),
        "lru_w_a": w(ks[14], (DEPTH, LRU_HEADS, LRU_HEAD_DIM, LRU_HEAD_DIM), LRU_HEAD_DIM),
        "lru_b_a": 0.02 * jax.random.normal(ks[15], (DEPTH, LRU_HEADS, LRU_HEAD_DIM), f32),
        "lru_a_param": a_param,
        "w_attn_branch": w(ks[16], (DEPTH, ATTN_OUT_WIDTH, D_MODEL), ATTN_OUT_WIDTH),
        "w_rec_branch": w(ks[17], (DEPTH, LRU_WIDTH, D_MODEL), LRU_WIDTH),
        "w_out": w(ks[18], (DEPTH, D_MODEL, D_MODEL), D_MODEL),
        "ffn2_norm_pre": gain(ks[21]),
        "ffn2_norm_post": gain(ks[22]),
        "ffn2_w_gate": w(ks[23], (DEPTH, D_MODEL, D_FF), D_MODEL),
        "ffn2_w_up": w(ks[24], (DEPTH, D_MODEL, D_FF), D_MODEL),
        "ffn2_w_down": w(ks[25], (DEPTH, D_FF, D_MODEL), D_FF),
    }


def reference(x, ffn1_norm_pre, ffn1_norm_post, ffn1_w_gate, ffn1_w_up, ffn1_w_down,
              mix_norm_pre, mix_norm_post, w_in, rel_bias_table, conv_w, conv_b,
              lru_w_x, lru_b_x, lru_w_a, lru_b_a, lru_a_param,
              w_attn_branch, w_rec_branch, w_out,
              ffn2_norm_pre, ffn2_norm_post, ffn2_w_gate, ffn2_w_up, ffn2_w_down):
    B, S, _ = x.shape
    splits = [ATTN_QKV_WIDTH, 2 * ATTN_QKV_WIDTH, 3 * ATTN_QKV_WIDTH,
              3 * ATTN_QKV_WIDTH + LRU_WIDTH, 3 * ATTN_QKV_WIDTH + 2 * LRU_WIDTH]
    h = x
    for l in range(DEPTH):
        f = swiglu(rms_norm(h, ffn1_norm_pre[l]), ffn1_w_gate[l], ffn1_w_up[l], ffn1_w_down[l])
        h = h + 0.5 * rms_norm(f, ffn1_norm_post[l])

        u = rms_norm(h, mix_norm_pre[l])
        proj = u @ w_in[l]
        q, k, v, xr, yr, gl = jnp.split(proj, splits, axis=-1)
        q = q.reshape(B, S, N_GROUPS, HEADS_PER_GROUP, HEAD_DIM)
        k = k.reshape(B, S, N_GROUPS, HEADS_PER_GROUP, HEAD_DIM)
        v = v.reshape(B, S, N_GROUPS, HEADS_PER_GROUP, HEAD_DIM)

        outs, lses = [], []
        for g, (window, dilation) in enumerate(DILATED_GROUPS):
            o_g, lse_g = dilated_group_attention(
                q[:, :, g], k[:, :, g], v[:, :, g],
                rel_bias_table[:, g * HEADS_PER_GROUP:(g + 1) * HEADS_PER_GROUP],
                window, dilation)
            outs.append(o_g)
            lses.append(lse_g)
        alpha = jax.nn.softmax(jnp.stack(lses, axis=0), axis=0)
        attn = jnp.einsum('gbsh,gbshe->bshe', alpha, jnp.stack(outs, axis=0).astype(jnp.float32))
        attn_d = attn.astype(x.dtype).reshape(B, S, ATTN_OUT_WIDTH) @ w_attn_branch[l]

        rec = rg_lru_branch(xr, conv_w[l], conv_b[l], lru_w_x[l], lru_b_x[l],
                            lru_w_a[l], lru_b_a[l], lru_a_param[l]) * jax.nn.gelu(yr)
        rec_d = rec @ w_rec_branch[l]

        gates = jax.nn.sigmoid(gl).reshape(B, S, N_BRANCHES, D_MODEL)
        merged = gates[:, :, 0] * attn_d + gates[:, :, 1] * rec_d
        h = h + rms_norm(merged @ w_out[l], mix_norm_post[l])

        f = swiglu(rms_norm(h, ffn2_norm_pre[l]), ffn2_w_gate[l], ffn2_w_up[l], ffn2_w_down[l])
        h = h + 0.5 * rms_norm(f, ffn2_norm_post[l])
    return h
```

```python
import functools
import math

import numpy as np
import jax
import jax.numpy as jnp
from jax import lax
from jax.experimental import pallas as pl
from jax.experimental.pallas import tpu as pltpu

F32 = jnp.float32
BF16 = jnp.bfloat16

D_MODEL = 1024
HEAD_DIM = 128
HEADS_PER_GROUP = 4
DILATIONS = (1, 4, 16)
BAND = 128
N_GROUPS = 3
GROUP_W = HEADS_PER_GROUP * HEAD_DIM
QKV_W = N_GROUPS * GROUP_W
LRU_W = D_MODEL
LRU_HEADS = 4
LRU_HD = LRU_W // LRU_HEADS
CONV_W = 4
LRU_C = 8.0
D_FF = 2816
REL_BUCKETS = 32
REL_MAX_DISTANCE = 2048
EPS = 1e-6

LANES = 128
SUBLANES = 8
VMEM_LIMIT = 56 * 1024 * 1024

TM = 512
ATT_R = 512
SEG = TM // SUBLANES
PITCH = SEG + SUBLANES


def _rms(x, g):
    ms = jnp.mean(x * x, axis=-1, keepdims=True)
    return x * lax.rsqrt(ms + EPS) * g


def _sigmoid(x):
    return 1.0 / (1.0 + jnp.exp(-x))


def _gelu_tanh(x):
    c = math.sqrt(2.0 / math.pi)
    return x * (0.5 * (1.0 + jnp.tanh(c * (x + 0.044715 * (x * x * x)))))


def _one_minus_exp_neg(y, exp_neg_y):
    p = 1.0 / 120.0 - y * (1.0 / 720.0)
    p = 1.0 / 24.0 - y * p
    p = 1.0 / 6.0 - y * p
    p = 0.5 - y * p
    p = y * (1.0 - y * p)
    return jnp.where(y < 0.125, p, 1.0 - exp_neg_y)


def _const_spec(shape):
    nd = len(shape)
    return pl.BlockSpec(shape, lambda *_: (0,) * nd, pipeline_mode=pl.Buffered(1))


def _ffn_kernel(x_ref, gpre_ref, gpost_ref, wg_ref, wu_ref, wd_ref, o_ref):
    x = x_ref[...]
    xn = _rms(x, gpre_ref[...]).astype(BF16)
    g = jnp.dot(xn, wg_ref[...], preferred_element_type=F32)
    u = jnp.dot(xn, wu_ref[...], preferred_element_type=F32)
    a = (g * _sigmoid(g) * u).astype(BF16)
    f = jnp.dot(a, wd_ref[...], preferred_element_type=F32)
    o_ref[...] = x + 0.5 * _rms(f, gpost_ref[...])


def _ffn(x2, g_pre, g_post, wg, wu, wd):
    n = x2.shape[0]
    row = pl.BlockSpec((TM, D_MODEL), lambda i: (i, 0))
    return pl.pallas_call(
        _ffn_kernel,
        name="ffn",
        grid=(n // TM,),
        in_specs=[row, _const_spec((1, D_MODEL)), _const_spec((1, D_MODEL)),
                  _const_spec((D_MODEL, D_FF)), _const_spec((D_MODEL, D_FF)),
                  _const_spec((D_FF, D_MODEL))],
        out_specs=row,
        out_shape=jax.ShapeDtypeStruct((n, D_MODEL), F32),
        compiler_params=pltpu.CompilerParams(
            dimension_semantics=("arbitrary",), vmem_limit_bytes=VMEM_LIMIT),
    )(x2, g_pre, g_post, wg, wu, wd)


def _qkv_kernel(x_ref, g_ref, w_ref, o_ref):
    u = _rms(x_ref[...], g_ref[...]).astype(BF16)
    o_ref[...] = jnp.dot(u, w_ref[...], preferred_element_type=F32).astype(BF16)


def _qkv(x2, g, w):
    n = x2.shape[0]
    wcols = w.shape[1]
    return pl.pallas_call(
        _qkv_kernel,
        name="qkv",
        grid=(n // TM,),
        in_specs=[pl.BlockSpec((TM, D_MODEL), lambda i: (i, 0)),
                  _const_spec((1, D_MODEL)), _const_spec((D_MODEL, wcols))],
        out_specs=pl.BlockSpec((TM, wcols), lambda i: (i, 0)),
        out_shape=jax.ShapeDtypeStruct((n, wcols), BF16),
        compiler_params=pltpu.CompilerParams(
            dimension_semantics=("arbitrary",), vmem_limit_bytes=VMEM_LIMIT),
    )(x2, g, w)


def _lru_kernel(x_ref, g_ref, w_ref, cw_ref, cb_ref, wx_ref, bx_ref, wa_ref, ba_ref, ap_ref,
                rec_ref, gl_ref, xbuf, a_s, b_s, gy_s, hcar, cin_s):
    t = pl.program_id(1)

    @pl.when(t == 0)
    def _():
        xbuf[0:SUBLANES, :] = jnp.zeros((SUBLANES, LRU_W), F32)
        hcar[...] = jnp.zeros(hcar.shape, F32)

    u = _rms(x_ref[0], g_ref[...]).astype(BF16)
    proj = jnp.dot(u, w_ref[...], preferred_element_type=F32)
    gl_ref[0] = proj[:, 2 * LRU_W:]
    gy_s[...] = _gelu_tanh(proj[:, LRU_W:2 * LRU_W])

    xbuf[SUBLANES:SUBLANES + TM, :] = proj[:, :LRU_W]
    xc = cb_ref[...] + cw_ref[0:1, :] * xbuf[SUBLANES - 3:SUBLANES - 3 + TM, :]
    for j in range(1, CONV_W):
        off = SUBLANES - (CONV_W - 1) + j
        xc = xc + cw_ref[j:j + 1, :] * xbuf[off:off + TM, :]
    xbuf[0:SUBLANES, :] = xbuf[TM:TM + SUBLANES, :]
    xcb = xc.astype(BF16)

    ap = -ap_ref[...]
    neg_c_softplus = -LRU_C * (jnp.maximum(ap, 0.0) + jnp.log1p(jnp.exp(-jnp.abs(ap))))

    slabs_per_head = LRU_HD // LANES
    for i in range(LRU_HEADS):
        cols = slice(i * LRU_HD, (i + 1) * LRU_HD)
        xh = xcb[:, cols]
        gx = _sigmoid(jnp.dot(xh, wx_ref[i], preferred_element_type=F32) + bx_ref[:, cols])
        ga = _sigmoid(jnp.dot(xh, wa_ref[i], preferred_element_type=F32) + ba_ref[:, cols])
        log_a = ga * neg_c_softplus[:, cols]
        a = jnp.exp(log_a)
        b = jnp.sqrt(_one_minus_exp_neg(-2.0 * log_a, a * a)) * (gx * xc[:, cols])
        for k in range(slabs_per_head):
            c = i * slabs_per_head + k
            for s in range(SUBLANES):
                a_s[c, s * PITCH:s * PITCH + SEG, :] = a[s * SEG:(s + 1) * SEG, k * LANES:(k + 1) * LANES]
                b_s[c, s * PITCH:s * PITCH + SEG, :] = b[s * SEG:(s + 1) * SEG, k * LANES:(k + 1) * LANES]

    n_slab = LRU_W // LANES

    def step(j, carry):
        hs, cum = carry
        nh, nc = [], []
        for c in range(n_slab):
            idx = pl.ds(j, SUBLANES, stride=PITCH)
            aj = a_s[c, idx, :]
            bj = b_s[c, idx, :]
            h = aj * hs[c] + bj
            p = aj * cum[c]
            a_s[c, idx, :] = p
            b_s[c, idx, :] = h
            nh.append(h)
            nc.append(p)
        return tuple(nh), tuple(nc)

    zeros = tuple(jnp.zeros((SUBLANES, LANES), F32) for _ in range(n_slab))
    ones = tuple(jnp.ones((SUBLANES, LANES), F32) for _ in range(n_slab))
    h_end, p_end = lax.fori_loop(0, SEG, step, (zeros, ones))

    row = lax.broadcasted_iota(jnp.int32, (SUBLANES, LANES), 0)
    for c in range(n_slab):
        cols = slice(c * LANES, (c + 1) * LANES)
        h_in = jnp.broadcast_to(hcar[SUBLANES - 1:SUBLANES, cols], (SUBLANES, LANES))
        cin = h_in
        for _ in range(SUBLANES - 1):
            out = h_end[c] + p_end[c] * cin
            cin = jnp.where(row == 0, h_in, pltpu.roll(out, 1, 0))
        cin_s[c] = cin
        hcar[:, cols] = h_end[c] + p_end[c] * cin

    for c in range(n_slab):
        cols = slice(c * LANES, (c + 1) * LANES)
        for s in range(SUBLANES):
            rows = slice(s * SEG, (s + 1) * SEG)
            prow = slice(s * PITCH, s * PITCH + SEG)
            h = b_s[c, prow, :] + a_s[c, prow, :] * cin_s[c, s:s + 1, :]
            rec_ref[0, rows, cols] = (h * gy_s[rows, cols]).astype(BF16)


def _lru(h3, g, w_xyg, conv_w, conv_b, wx, bx, wa, ba, ap):
    bsz, seq, _ = h3.shape
    n_slab = LRU_W // LANES
    return pl.pallas_call(
        _lru_kernel,
        name="lru",
        grid=(bsz, seq // TM),
        in_specs=[pl.BlockSpec((1, TM, D_MODEL), lambda b, t: (b, t, 0)),
                  _const_spec((1, D_MODEL)), _const_spec((D_MODEL, 4 * LRU_W)),
                  _const_spec((CONV_W, LRU_W)), _const_spec((1, LRU_W)),
                  _const_spec((LRU_HEADS, LRU_HD, LRU_HD)), _const_spec((1, LRU_W)),
                  _const_spec((LRU_HEADS, LRU_HD, LRU_HD)), _const_spec((1, LRU_W)),
                  _const_spec((1, LRU_W))],
        out_specs=[pl.BlockSpec((1, TM, LRU_W), lambda b, t: (b, t, 0)),
                   pl.BlockSpec((1, TM, 2 * D_MODEL), lambda b, t: (b, t, 0))],
        out_shape=[jax.ShapeDtypeStruct((bsz, seq, LRU_W), BF16),
                   jax.ShapeDtypeStruct((bsz, seq, 2 * D_MODEL), F32)],
        scratch_shapes=[pltpu.VMEM((TM + SUBLANES, LRU_W), F32),
                        pltpu.VMEM((n_slab, SUBLANES * PITCH, LANES), F32),
                        pltpu.VMEM((n_slab, SUBLANES * PITCH, LANES), F32),
                        pltpu.VMEM((TM, LRU_W), F32),
                        pltpu.VMEM((SUBLANES, LRU_W), F32),
                        pltpu.VMEM((n_slab, SUBLANES, LANES), F32)],
        compiler_params=pltpu.CompilerParams(
            dimension_semantics=("arbitrary", "arbitrary"), vmem_limit_bytes=VMEM_LIMIT),
    )(h3, g, w_xyg, conv_w, conv_b, wx, bx, wa, ba, ap)


def _bucket_table():
    qi = np.arange(BAND)[:, None]
    kj = np.arange(BAND)[None, :]
    max_exact = REL_BUCKETS // 2
    out = np.zeros((N_GROUPS, 2, BAND, BAND), np.int32)
    for g, d in enumerate(DILATIONS):
        for half in range(2):
            steps = qi + BAND - kj if half == 0 else qi - kj
            valid = (steps >= 0) & (steps <= BAND)
            dist = np.maximum(steps, 0) * d
            nf = np.maximum(dist, 1).astype(np.float32)
            large = max_exact + (np.log(nf / np.float32(max_exact))
                                 / np.float32(math.log(REL_MAX_DISTANCE / max_exact))
                                 * np.float32(REL_BUCKETS - max_exact)).astype(np.int32)
            large = np.minimum(large, REL_BUCKETS - 1)
            bucket = np.where(dist < max_exact, dist, large)
            out[g, half] = np.where(valid, bucket, -1)
    return out


def _attn_kernel(group, tab_ref, bkt_ref, q_ref, k_ref, v_ref, o_ref, lse_ref,
                 kprev, vprev, bias_s):
    first = (pl.program_id(0) == 0) & (pl.program_id(1) == 0) & (pl.program_id(2) == 0)
    c = pl.program_id(2)

    @pl.when(first)
    def _():
        for h in range(HEADS_PER_GROUP):
            for half in range(2):
                bk = bkt_ref[half]
                bias = jnp.full((BAND, BAND), -jnp.inf, F32)
                for n in range(REL_BUCKETS):
                    bias = jnp.where(bk == n, tab_ref[n, group * HEADS_PER_GROUP + h], bias)
                bias_s[h, half] = bias

    @pl.when(c == 0)
    def _():
        kprev[...] = jnp.zeros(kprev.shape, BF16)
        vprev[...] = jnp.zeros(vprev.shape, BF16)

    pen = jnp.where(c == 0, -jnp.inf, 0.0).astype(F32)
    scale = 1.0 / math.sqrt(HEAD_DIM)
    nt = (((1,), (1,)), ((), ()))
    lane = lax.broadcasted_iota(jnp.int32, (BAND, LANES), 1)

    for n in range(ATT_R // BAND):
        rows = slice(n * BAND, (n + 1) * BAND)
        prow = slice((n - 1) * BAND, n * BAND)
        lse_blk = jnp.zeros((BAND, LANES), F32)
        for h in range(HEADS_PER_GROUP):
            cols = slice(h * HEAD_DIM, (h + 1) * HEAD_DIM)
            q = q_ref[0, rows, cols]
            kc = k_ref[0, rows, cols]
            vc = v_ref[0, rows, cols]
            if n == 0:
                kp = kprev[:, cols]
                vp = vprev[:, cols]
            else:
                kp = k_ref[0, prow, cols]
                vp = v_ref[0, prow, cols]
            s_c = lax.dot_general(q, kc, nt, preferred_element_type=F32) * scale + bias_s[h, 1]
            s_p = lax.dot_general(q, kp, nt, preferred_element_type=F32) * scale + bias_s[h, 0]
            if n == 0:
                s_p = s_p + pen
            m = jnp.max(jnp.maximum(s_c, s_p), axis=-1, keepdims=True)
            e_c = jnp.exp(s_c - m)
            e_p = jnp.exp(s_p - m)
            l = jnp.sum(e_c + e_p, axis=-1, keepdims=True)
            o = (jnp.dot(e_c.astype(BF16), vc, preferred_element_type=F32)
                 + jnp.dot(e_p.astype(BF16), vp, preferred_element_type=F32))
            o_ref[0, rows, cols] = (o * (1.0 / l)).astype(BF16)
            lse_blk = jnp.where(lane == h, m + jnp.log(l), lse_blk)
        lse_ref[0, rows, :] = lse_blk

    kprev[...] = k_ref[0, ATT_R - BAND:ATT_R, :]
    vprev[...] = v_ref[0, ATT_R - BAND:ATT_R, :]


def _attn_group(group, qkv, table, bkt):
    bsz, seq, _ = qkv.shape
    d = DILATIONS[group]
    sub = seq // d
    rows = min(ATT_R, sub)
    assert rows == ATT_R and sub % ATT_R == 0
    qkv_v = qkv.reshape(bsz, sub, d * 3 * QKV_W)
    blocks_per_phase = 3 * QKV_W // GROUP_W
    qspec = pl.BlockSpec((1, ATT_R, GROUP_W), lambda b, p, c: (b, c, p * blocks_per_phase + group))
    kspec = pl.BlockSpec((1, ATT_R, GROUP_W), lambda b, p, c: (b, c, p * blocks_per_phase + N_GROUPS + group))
    vspec = pl.BlockSpec((1, ATT_R, GROUP_W), lambda b, p, c: (b, c, p * blocks_per_phase + 2 * N_GROUPS + group))
    o, lse = pl.pallas_call(
        functools.partial(_attn_kernel, group),
        name=f"attn{group}",
        grid=(bsz, d, sub // ATT_R),
        in_specs=[pl.BlockSpec(memory_space=pltpu.SMEM),
                  pl.BlockSpec((None, 2, BAND, BAND), lambda b, p, c: (group, 0, 0, 0)),
                  qspec, kspec, vspec],
        out_specs=[pl.BlockSpec((1, ATT_R, GROUP_W), lambda b, p, c: (b, c, p)),
                   pl.BlockSpec((1, ATT_R, LANES), lambda b, p, c: (b, c, p))],
        out_shape=[jax.ShapeDtypeStruct((bsz, sub, d * GROUP_W), BF16),
                   jax.ShapeDtypeStruct((bsz, sub, d * LANES), F32)],
        scratch_shapes=[pltpu.VMEM((BAND, GROUP_W), BF16), pltpu.VMEM((BAND, GROUP_W), BF16),
                        pltpu.VMEM((HEADS_PER_GROUP, 2, BAND, BAND), F32)],
        compiler_params=pltpu.CompilerParams(
            dimension_semantics=("arbitrary", "arbitrary", "arbitrary"),
            vmem_limit_bytes=VMEM_LIMIT),
    )(table, bkt, qkv_v, qkv_v, qkv_v)
    return o.reshape(bsz, seq, GROUP_W), lse.reshape(bsz, seq, LANES)


def _merge_kernel(h_ref, o0_ref, o1_ref, o2_ref, l0_ref, l1_ref, l2_ref, rec_ref, gl_ref,
                  wat_ref, wrec_ref, wout_ref, gpost_ref, out_ref):
    l0, l1, l2 = l0_ref[...], l1_ref[...], l2_ref[...]
    mx = jnp.maximum(jnp.maximum(l0, l1), l2)
    w0, w1, w2 = jnp.exp(l0 - mx), jnp.exp(l1 - mx), jnp.exp(l2 - mx)
    inv = 1.0 / (w0 + w1 + w2)
    w0, w1, w2 = w0 * inv, w1 * inv, w2 * inv
    parts = []
    for h in range(HEADS_PER_GROUP):
        cols = slice(h * HEAD_DIM, (h + 1) * HEAD_DIM)
        parts.append(w0[:, h:h + 1] * o0_ref[:, cols].astype(F32)
                     + w1[:, h:h + 1] * o1_ref[:, cols].astype(F32)
                     + w2[:, h:h + 1] * o2_ref[:, cols].astype(F32))
    attn = jnp.concatenate(parts, axis=-1).astype(BF16)
    attn_d = jnp.dot(attn, wat_ref[...], preferred_element_type=F32)
    rec_d = jnp.dot(rec_ref[...], wrec_ref[...], preferred_element_type=F32)
    gl = gl_ref[...]
    merged = _sigmoid(gl[:, :D_MODEL]) * attn_d + _sigmoid(gl[:, D_MODEL:]) * rec_d
    mo = jnp.dot(merged.astype(BF16), wout_ref[...], preferred_element_type=F32)
    out_ref[...] = h_ref[...] + _rms(mo, gpost_ref[...])


def _merge(h2, o_list, lse_list, rec2, gl2, w_attn, w_rec, w_out, g_post):
    n = h2.shape[0]

    def row(w):
        return pl.BlockSpec((TM, w), lambda i: (i, 0))

    return pl.pallas_call(
        _merge_kernel,
        name="merge",
        grid=(n // TM,),
        in_specs=[row(D_MODEL), row(GROUP_W), row(GROUP_W), row(GROUP_W),
                  row(LANES), row(LANES), row(LANES), row(LRU_W), row(2 * D_MODEL),
                  _const_spec((GROUP_W, D_MODEL)), _const_spec((LRU_W, D_MODEL)),
                  _const_spec((D_MODEL, D_MODEL)), _const_spec((1, D_MODEL))],
        out_specs=row(D_MODEL),
        out_shape=jax.ShapeDtypeStruct((n, D_MODEL), F32),
        compiler_params=pltpu.CompilerParams(
            dimension_semantics=("arbitrary",), vmem_limit_bytes=VMEM_LIMIT),
    )(h2, *o_list, *lse_list, rec2, gl2, w_attn, w_rec, w_out, g_post)


def kernel(x, ffn1_norm_pre, ffn1_norm_post, ffn1_w_gate, ffn1_w_up, ffn1_w_down, mix_norm_pre, mix_norm_post, w_in, rel_bias_table, conv_w, conv_b, lru_w_x, lru_b_x, lru_w_a, lru_b_a, lru_a_param, w_attn_branch, w_rec_branch, w_out, ffn2_norm_pre, ffn2_norm_post, ffn2_w_gate, ffn2_w_up, ffn2_w_down):
    bsz, seq, _ = x.shape
    n = bsz * seq
    depth = w_in.shape[0]
    bkt = jnp.asarray(_bucket_table())
    h = x.reshape(n, D_MODEL)
    for l in range(depth):
        h = _ffn(h, ffn1_norm_pre[l][None], ffn1_norm_post[l][None],
                 ffn1_w_gate[l].astype(BF16), ffn1_w_up[l].astype(BF16), ffn1_w_down[l].astype(BF16))

        w_in_b = w_in[l].astype(BF16)
        g_mix = mix_norm_pre[l][None]
        qkv = _qkv(h, g_mix, w_in_b[:, :3 * QKV_W]).reshape(bsz, seq, 3 * QKV_W)
        rec, gl = _lru(h.reshape(bsz, seq, D_MODEL), g_mix, w_in_b[:, 3 * QKV_W:],
                       conv_w[l], conv_b[l][None],
                       lru_w_x[l].astype(BF16), lru_b_x[l].reshape(1, LRU_W),
                       lru_w_a[l].astype(BF16), lru_b_a[l].reshape(1, LRU_W),
                       lru_a_param[l][None])
        o_list, lse_list = [], []
        for g in range(N_GROUPS):
            o_g, lse_g = _attn_group(g, qkv, rel_bias_table, bkt)
            o_list.append(o_g.reshape(n, GROUP_W))
            lse_list.append(lse_g.reshape(n, LANES))
        h = _merge(h, o_list, lse_list, rec.reshape(n, LRU_W), gl.reshape(n, 2 * D_MODEL),
                   w_attn_branch[l].astype(BF16), w_rec_branch[l].astype(BF16),
                   w_out[l].astype(BF16), mix_norm_post[l][None])

        h = _ffn(h, ffn2_norm_pre[l][None], ffn2_norm_post[l][None],
                 ffn2_w_gate[l].astype(BF16), ffn2_w_up[l].astype(BF16), ffn2_w_down[l].astype(BF16))
    return h.reshape(bsz, seq, D_MODEL)
```

```python
import functools
import math

import numpy as np
import jax
import jax.numpy as jnp
from jax import lax
from jax.experimental import pallas as pl
from jax.experimental.pallas import tpu as pltpu

F32 = jnp.float32
BF16 = jnp.bfloat16

D_MODEL = 1024
HEAD_DIM = 128
HEADS_PER_GROUP = 4
DILATIONS = (1, 4, 16)
BAND = 128
N_GROUPS = 3
GROUP_W = HEADS_PER_GROUP * HEAD_DIM
QKV_W = N_GROUPS * GROUP_W
LRU_W = D_MODEL
LRU_HEADS = 4
LRU_HD = LRU_W // LRU_HEADS
CONV_W = 4
LRU_C = 8.0
D_FF = 2816
REL_BUCKETS = 32
REL_MAX_DISTANCE = 2048
EPS = 1e-6

LANES = 128
SUBLANES = 8
VMEM_LIMIT = 56 * 1024 * 1024

TM = 512
ATT_R = 512
SEG = TM // SUBLANES
PITCH = SEG + SUBLANES


def _rms(x, g):
    ms = jnp.mean(x * x, axis=-1, keepdims=True)
    return x * lax.rsqrt(ms + EPS) * g


def _sigmoid(x):
    return 1.0 / (1.0 + jnp.exp(-x))


def _gelu_tanh(x):
    c = math.sqrt(2.0 / math.pi)
    return x * (0.5 * (1.0 + jnp.tanh(c * (x + 0.044715 * (x * x * x)))))


def _one_minus_exp_neg(y, exp_neg_y):
    p = 1.0 / 120.0 - y * (1.0 / 720.0)
    p = 1.0 / 24.0 - y * p
    p = 1.0 / 6.0 - y * p
    p = 0.5 - y * p
    p = y * (1.0 - y * p)
    return jnp.where(y < 0.125, p, 1.0 - exp_neg_y)


def _const_spec(shape):
    nd = len(shape)
    return pl.BlockSpec(shape, lambda *_: (0,) * nd, pipeline_mode=pl.Buffered(1))


def _ffn_kernel(x_ref, gpre_ref, gpost_ref, wg_ref, wu_ref, wd_ref, o_ref):
    x = x_ref[...]
    xn = _rms(x, gpre_ref[...]).astype(BF16)
    g = jnp.dot(xn, wg_ref[...], preferred_element_type=F32)
    u = jnp.dot(xn, wu_ref[...], preferred_element_type=F32)
    a = (g * _sigmoid(g) * u).astype(BF16)
    f = jnp.dot(a, wd_ref[...], preferred_element_type=F32)
    o_ref[...] = x + 0.5 * _rms(f, gpost_ref[...])


def _ffn(x2, g_pre, g_post, wg, wu, wd):
    n = x2.shape[0]
    row = pl.BlockSpec((TM, D_MODEL), lambda i: (i, 0))
    return pl.pallas_call(
        _ffn_kernel,
        name="ffn",
        grid=(n // TM,),
        in_specs=[row, _const_spec((1, D_MODEL)), _const_spec((1, D_MODEL)),
                  _const_spec((D_MODEL, D_FF)), _const_spec((D_MODEL, D_FF)),
                  _const_spec((D_FF, D_MODEL))],
        out_specs=row,
        out_shape=jax.ShapeDtypeStruct((n, D_MODEL), F32),
        compiler_params=pltpu.CompilerParams(
            dimension_semantics=("arbitrary",), vmem_limit_bytes=VMEM_LIMIT),
    )(x2, g_pre, g_post, wg, wu, wd)


def _qkv_kernel(x_ref, g_ref, w_ref, o0_ref, o1_ref, o2_ref, sc):
    u = _rms(x_ref[0], g_ref[...]).astype(BF16)
    gw = 3 * GROUP_W
    o0_ref[0, 0] = jnp.dot(u, w_ref[:, :gw], preferred_element_type=F32).astype(BF16)
    res = jnp.dot(u, w_ref[:, gw:], preferred_element_type=F32)
    slabs = gw // LANES
    for c in range(2 * slabs):
        sc[c] = res[:, c * LANES:(c + 1) * LANES]
    for gi, o_ref in ((1, o1_ref), (2, o2_ref)):
        d = DILATIONS[gi]
        for p in range(d):
            for c in range(slabs):
                o_ref[0, p, :, c * LANES:(c + 1) * LANES] = (
                    sc[(gi - 1) * slabs + c, pl.ds(p, TM // d, stride=d), :].astype(BF16))


def _qkv(h3, g, w):
    bsz, seq, _ = h3.shape
    gw = 3 * GROUP_W
    out_specs, out_shape = [], []
    for d in DILATIONS:
        out_specs.append(pl.BlockSpec((1, d, TM // d, gw), lambda b, i: (b, 0, i, 0)))
        out_shape.append(jax.ShapeDtypeStruct((bsz, d, seq // d, gw), BF16))
    return pl.pallas_call(
        _qkv_kernel,
        name="qkv",
        grid=(bsz, seq // TM),
        in_specs=[pl.BlockSpec((1, TM, D_MODEL), lambda b, i: (b, i, 0)),
                  _const_spec((1, D_MODEL)), _const_spec((D_MODEL, N_GROUPS * gw))],
        out_specs=out_specs,
        out_shape=out_shape,
        scratch_shapes=[pltpu.VMEM((2 * gw // LANES, TM, LANES), F32)],
        compiler_params=pltpu.CompilerParams(
            dimension_semantics=("arbitrary", "arbitrary"), vmem_limit_bytes=VMEM_LIMIT),
    )(h3, g, w)


def _lru_kernel(x_ref, g_ref, w_ref, cw_ref, cb_ref, wx_ref, bx_ref, wa_ref, ba_ref, ap_ref,
                rec_ref, gl_ref, xbuf, a_s, b_s, gy_s, hcar, cin_s):
    t = pl.program_id(1)

    @pl.when(t == 0)
    def _():
        xbuf[0:SUBLANES, :] = jnp.zeros((SUBLANES, LRU_W), F32)
        hcar[...] = jnp.zeros(hcar.shape, F32)

    u = _rms(x_ref[0], g_ref[...]).astype(BF16)
    proj = jnp.dot(u, w_ref[...], preferred_element_type=F32)
    gl_ref[0] = proj[:, 2 * LRU_W:]
    gy_s[...] = _gelu_tanh(proj[:, LRU_W:2 * LRU_W])

    xbuf[SUBLANES:SUBLANES + TM, :] = proj[:, :LRU_W]
    xc = cb_ref[...] + cw_ref[0:1, :] * xbuf[SUBLANES - 3:SUBLANES - 3 + TM, :]
    for j in range(1, CONV_W):
        off = SUBLANES - (CONV_W - 1) + j
        xc = xc + cw_ref[j:j + 1, :] * xbuf[off:off + TM, :]
    xbuf[0:SUBLANES, :] = xbuf[TM:TM + SUBLANES, :]
    xcb = xc.astype(BF16)

    ap = -ap_ref[...]
    neg_c_softplus = -LRU_C * (jnp.maximum(ap, 0.0) + jnp.log1p(jnp.exp(-jnp.abs(ap))))

    slabs_per_head = LRU_HD // LANES
    for i in range(LRU_HEADS):
        cols = slice(i * LRU_HD, (i + 1) * LRU_HD)
        xh = xcb[:, cols]
        gx = _sigmoid(jnp.dot(xh, wx_ref[i], preferred_element_type=F32) + bx_ref[:, cols])
        ga = _sigmoid(jnp.dot(xh, wa_ref[i], preferred_element_type=F32) + ba_ref[:, cols])
        log_a = ga * neg_c_softplus[:, cols]
        a = jnp.exp(log_a)
        b = jnp.sqrt(_one_minus_exp_neg(-2.0 * log_a, a * a)) * (gx * xc[:, cols])
        for k in range(slabs_per_head):
            c = i * slabs_per_head + k
            for s in range(SUBLANES):
                a_s[c, s * PITCH:s * PITCH + SEG, :] = a[s * SEG:(s + 1) * SEG, k * LANES:(k + 1) * LANES]
                b_s[c, s * PITCH:s * PITCH + SEG, :] = b[s * SEG:(s + 1) * SEG, k * LANES:(k + 1) * LANES]

    n_slab = LRU_W // LANES

    def step(j, carry):
        hs, cum = carry
        nh, nc = [], []
        for c in range(n_slab):
            idx = pl.ds(j, SUBLANES, stride=PITCH)
            aj = a_s[c, idx, :]
            bj = b_s[c, idx, :]
            h = aj * hs[c] + bj
            p = aj * cum[c]
            a_s[c, idx, :] = p
            b_s[c, idx, :] = h
            nh.append(h)
            nc.append(p)
        return tuple(nh), tuple(nc)

    zeros = tuple(jnp.zeros((SUBLANES, LANES), F32) for _ in range(n_slab))
    ones = tuple(jnp.ones((SUBLANES, LANES), F32) for _ in range(n_slab))
    h_end, p_end = lax.fori_loop(0, SEG, step, (zeros, ones))

    row = lax.broadcasted_iota(jnp.int32, (SUBLANES, LANES), 0)
    for c in range(n_slab):
        cols = slice(c * LANES, (c + 1) * LANES)
        h_in = jnp.broadcast_to(hcar[SUBLANES - 1:SUBLANES, cols], (SUBLANES, LANES))
        cin = h_in
        for _ in range(SUBLANES - 1):
            out = h_end[c] + p_end[c] * cin
            cin = jnp.where(row == 0, h_in, pltpu.roll(out, 1, 0))
        cin_s[c] = cin
        hcar[:, cols] = h_end[c] + p_end[c] * cin

    for c in range(n_slab):
        cols = slice(c * LANES, (c + 1) * LANES)
        for s in range(SUBLANES):
            rows = slice(s * SEG, (s + 1) * SEG)
            prow = slice(s * PITCH, s * PITCH + SEG)
            h = b_s[c, prow, :] + a_s[c, prow, :] * cin_s[c, s:s + 1, :]
            rec_ref[0, rows, cols] = (h * gy_s[rows, cols]).astype(BF16)


def _lru(h3, g, w_xyg, conv_w, conv_b, wx, bx, wa, ba, ap):
    bsz, seq, _ = h3.shape
    n_slab = LRU_W // LANES
    return pl.pallas_call(
        _lru_kernel,
        name="lru",
        grid=(bsz, seq // TM),
        in_specs=[pl.BlockSpec((1, TM, D_MODEL), lambda b, t: (b, t, 0)),
                  _const_spec((1, D_MODEL)), _const_spec((D_MODEL, 4 * LRU_W)),
                  _const_spec((CONV_W, LRU_W)), _const_spec((1, LRU_W)),
                  _const_spec((LRU_HEADS, LRU_HD, LRU_HD)), _const_spec((1, LRU_W)),
                  _const_spec((LRU_HEADS, LRU_HD, LRU_HD)), _const_spec((1, LRU_W)),
                  _const_spec((1, LRU_W))],
        out_specs=[pl.BlockSpec((1, TM, LRU_W), lambda b, t: (b, t, 0)),
                   pl.BlockSpec((1, TM, 2 * D_MODEL), lambda b, t: (b, t, 0))],
        out_shape=[jax.ShapeDtypeStruct((bsz, seq, LRU_W), BF16),
                   jax.ShapeDtypeStruct((bsz, seq, 2 * D_MODEL), F32)],
        scratch_shapes=[pltpu.VMEM((TM + SUBLANES, LRU_W), F32),
                        pltpu.VMEM((n_slab, SUBLANES * PITCH, LANES), F32),
                        pltpu.VMEM((n_slab, SUBLANES * PITCH, LANES), F32),
                        pltpu.VMEM((TM, LRU_W), F32),
                        pltpu.VMEM((SUBLANES, LRU_W), F32),
                        pltpu.VMEM((n_slab, SUBLANES, LANES), F32)],
        compiler_params=pltpu.CompilerParams(
            dimension_semantics=("arbitrary", "arbitrary"), vmem_limit_bytes=VMEM_LIMIT),
    )(h3, g, w_xyg, conv_w, conv_b, wx, bx, wa, ba, ap)


def _bucket_table():
    qi = np.arange(BAND)[:, None]
    kj = np.arange(BAND)[None, :]
    max_exact = REL_BUCKETS // 2
    out = np.zeros((N_GROUPS, 2, BAND, BAND), np.int32)
    for g, d in enumerate(DILATIONS):
        for half in range(2):
            steps = qi + BAND - kj if half == 0 else qi - kj
            valid = (steps >= 0) & (steps <= BAND)
            dist = np.maximum(steps, 0) * d
            nf = np.maximum(dist, 1).astype(np.float32)
            large = max_exact + (np.log(nf / np.float32(max_exact))
                                 / np.float32(math.log(REL_MAX_DISTANCE / max_exact))
                                 * np.float32(REL_BUCKETS - max_exact)).astype(np.int32)
            large = np.minimum(large, REL_BUCKETS - 1)
            bucket = np.where(dist < max_exact, dist, large)
            out[g, half] = np.where(valid, bucket, -1)
    return out


def _attn_kernel(group, rows, tab_ref, bkt_ref, q_ref, k_ref, v_ref, o_ref, lse_ref,
                 kbuf, vbuf, bias_s):
    first = (pl.program_id(0) == 0) & (pl.program_id(1) == 0) & (pl.program_id(2) == 0)
    c = pl.program_id(2)
    log2e = math.log2(math.e)
    vw = 2 * HEAD_DIM

    @pl.when(first)
    def _():
        for h in range(HEADS_PER_GROUP):
            for half in range(2):
                bk = bkt_ref[half]
                bias = jnp.full((BAND, BAND), -jnp.inf, F32)
                for n in range(REL_BUCKETS):
                    bias = jnp.where(bk == n, tab_ref[n, group * HEADS_PER_GROUP + h] * log2e, bias)
                bias_s[h, :, half * BAND:(half + 1) * BAND] = bias
            vbuf[:, h * vw + HEAD_DIM:(h + 1) * vw] = jnp.ones((BAND + rows, HEAD_DIM), BF16)

    @pl.when(c == 0)
    def _():
        kbuf[0:BAND, :] = jnp.zeros((BAND, GROUP_W), BF16)
        for h in range(HEADS_PER_GROUP):
            vbuf[0:BAND, h * vw:h * vw + HEAD_DIM] = jnp.zeros((BAND, HEAD_DIM), BF16)

    kbuf[BAND:BAND + rows, :] = k_ref[...]
    for h in range(HEADS_PER_GROUP):
        vbuf[BAND:BAND + rows, h * vw:h * vw + HEAD_DIM] = v_ref[:, h * HEAD_DIM:(h + 1) * HEAD_DIM]

    lane2 = lax.broadcasted_iota(jnp.int32, (1, 2 * BAND), 1)
    pen = jnp.where((lane2 < BAND) & (c == 0), -jnp.inf, 0.0).astype(F32)
    qk_scale = log2e / math.sqrt(HEAD_DIM)
    nt = (((1,), (1,)), ((), ()))
    units = [(n, h) for n in range(rows // BAND) for h in range(HEADS_PER_GROUP)]

    scores = []
    for n, h in units:
        cols = slice(h * HEAD_DIM, (h + 1) * HEAD_DIM)
        q = q_ref[n * BAND:(n + 1) * BAND, cols]
        kk = kbuf[n * BAND:(n + 2) * BAND, cols]
        s = lax.dot_general(q, kk, nt, preferred_element_type=F32) * qk_scale + bias_s[h]
        if n == 0:
            s = s + pen
        scores.append(s)
    maxes = [jnp.max(s, axis=-1, keepdims=True) for s in scores]
    probs = [jnp.exp2(s - m).astype(BF16) for s, m in zip(scores, maxes)]
    outs = [jnp.dot(p, vbuf[n * BAND:(n + 2) * BAND, h * vw:(h + 1) * vw], preferred_element_type=F32)
            for (n, h), p in zip(units, probs)]
    lane = lax.broadcasted_iota(jnp.int32, (BAND, LANES), 1)
    ln2 = math.log(2.0)
    lse_blk = None
    for (n, h), o, m in zip(units, outs, maxes):
        l = o[:, HEAD_DIM:]
        o_ref[n * BAND:(n + 1) * BAND, h * HEAD_DIM:(h + 1) * HEAD_DIM] = (o[:, :HEAD_DIM] * (1.0 / l)).astype(BF16)
        lse = (m + jnp.log2(l)) * ln2
        lse_blk = lse if h == 0 else jnp.where(lane == h, lse, lse_blk)
        if h == HEADS_PER_GROUP - 1:
            lse_ref[n * BAND:(n + 1) * BAND, :] = lse_blk

    kbuf[0:BAND, :] = kbuf[rows:rows + BAND, :]
    vbuf[0:BAND, :] = vbuf[rows:rows + BAND, :]


def _attn_group(group, qkv_g, table, bkt):
    bsz, d, sub, _ = qkv_g.shape
    rows = min(ATT_R, sub)
    assert sub % rows == 0

    def spec(width, col):
        return pl.BlockSpec((None, None, rows, width), lambda b, p, c: (b, p, c, col))

    return pl.pallas_call(
        functools.partial(_attn_kernel, group, rows),
        name=f"attn{group}",
        grid=(bsz, d, sub // rows),
        in_specs=[pl.BlockSpec(memory_space=pltpu.SMEM),
                  pl.BlockSpec((None, 2, BAND, BAND), lambda b, p, c: (group, 0, 0, 0)),
                  spec(GROUP_W, 0), spec(GROUP_W, 1), spec(GROUP_W, 2)],
        out_specs=[spec(GROUP_W, 0), spec(LANES, 0)],
        out_shape=[jax.ShapeDtypeStruct((bsz, d, sub, GROUP_W), BF16),
                   jax.ShapeDtypeStruct((bsz, d, sub, LANES), F32)],
        scratch_shapes=[pltpu.VMEM((BAND + rows, GROUP_W), BF16),
                        pltpu.VMEM((BAND + rows, 2 * GROUP_W), BF16),
                        pltpu.VMEM((HEADS_PER_GROUP, BAND, 2 * BAND), F32)],
        compiler_params=pltpu.CompilerParams(
            dimension_semantics=("arbitrary", "arbitrary", "arbitrary"),
            vmem_limit_bytes=VMEM_LIMIT),
    )(table, bkt, qkv_g, qkv_g, qkv_g)


def _merge_kernel(h_ref, o0_ref, o1_ref, o2_ref, l0_ref, l1_ref, l2_ref, rec_ref, gl_ref,
                  wat_ref, wrec_ref, wout_ref, gpost_ref, out_ref, osc, lsc):
    for gi, o_ref, l_ref in ((1, o1_ref, l1_ref), (2, o2_ref, l2_ref)):
        d = DILATIONS[gi]
        for p in range(d):
            idx = pl.ds(p, TM // d, stride=d)
            lsc[gi - 1, idx, :] = l_ref[p]
            for h in range(HEADS_PER_GROUP):
                osc[(gi - 1) * HEADS_PER_GROUP + h, idx, :] = (
                    o_ref[p, :, h * HEAD_DIM:(h + 1) * HEAD_DIM].astype(F32))
    l0, l1, l2 = l0_ref[0], lsc[0], lsc[1]
    mx = jnp.maximum(jnp.maximum(l0, l1), l2)
    w0, w1, w2 = jnp.exp(l0 - mx), jnp.exp(l1 - mx), jnp.exp(l2 - mx)
    inv = 1.0 / (w0 + w1 + w2)
    w0, w1, w2 = w0 * inv, w1 * inv, w2 * inv
    parts = []
    for h in range(HEADS_PER_GROUP):
        cols = slice(h * HEAD_DIM, (h + 1) * HEAD_DIM)
        parts.append(w0[:, h:h + 1] * o0_ref[0, :, cols].astype(F32)
                     + w1[:, h:h + 1] * osc[h]
                     + w2[:, h:h + 1] * osc[HEADS_PER_GROUP + h])
    attn = jnp.concatenate(parts, axis=-1).astype(BF16)
    attn_d = jnp.dot(attn, wat_ref[...], preferred_element_type=F32)
    rec_d = jnp.dot(rec_ref[...], wrec_ref[...], preferred_element_type=F32)
    gl = gl_ref[...]
    merged = _sigmoid(gl[:, :D_MODEL]) * attn_d + _sigmoid(gl[:, D_MODEL:]) * rec_d
    mo = jnp.dot(merged.astype(BF16), wout_ref[...], preferred_element_type=F32)
    out_ref[...] = h_ref[...] + _rms(mo, gpost_ref[...])


def _merge(h3, o_list, lse_list, rec3, gl3, w_attn, w_rec, w_out, g_post):
    bsz, seq, _ = h3.shape

    def row(w):
        return pl.BlockSpec((None, TM, w), lambda b, i: (b, i, 0))

    def phased(d, w):
        return pl.BlockSpec((None, d, TM // d, w), lambda b, i: (b, 0, i, 0))

    return pl.pallas_call(
        _merge_kernel,
        name="merge",
        grid=(bsz, seq // TM),
        in_specs=[row(D_MODEL)] + [phased(d, GROUP_W) for d in DILATIONS]
                 + [phased(d, LANES) for d in DILATIONS] + [row(LRU_W), row(2 * D_MODEL),
                 _const_spec((GROUP_W, D_MODEL)), _const_spec((LRU_W, D_MODEL)),
                 _const_spec((D_MODEL, D_MODEL)), _const_spec((1, D_MODEL))],
        out_specs=row(D_MODEL),
        out_shape=jax.ShapeDtypeStruct((bsz, seq, D_MODEL), F32),
        scratch_shapes=[pltpu.VMEM((2 * HEADS_PER_GROUP, TM, LANES), F32),
                        pltpu.VMEM((2, TM, LANES), F32)],
        compiler_params=pltpu.CompilerParams(
            dimension_semantics=("arbitrary", "arbitrary"), vmem_limit_bytes=VMEM_LIMIT),
    )(h3, *o_list, *lse_list, rec3, gl3, w_attn, w_rec, w_out, g_post)


def kernel(x, ffn1_norm_pre, ffn1_norm_post, ffn1_w_gate, ffn1_w_up, ffn1_w_down, mix_norm_pre, mix_norm_post, w_in, rel_bias_table, conv_w, conv_b, lru_w_x, lru_b_x, lru_w_a, lru_b_a, lru_a_param, w_attn_branch, w_rec_branch, w_out, ffn2_norm_pre, ffn2_norm_post, ffn2_w_gate, ffn2_w_up, ffn2_w_down):
    bsz, seq, _ = x.shape
    n = bsz * seq
    depth = w_in.shape[0]
    bkt = jnp.asarray(_bucket_table())
    h = x.reshape(n, D_MODEL)
    for l in range(depth):
        h = _ffn(h, ffn1_norm_pre[l][None], ffn1_norm_post[l][None],
                 ffn1_w_gate[l].astype(BF16), ffn1_w_up[l].astype(BF16), ffn1_w_down[l].astype(BF16))

        h3 = h.reshape(bsz, seq, D_MODEL)

        g_mix = mix_norm_pre[l][None]
        w_qkv = (w_in[l][:, :3 * QKV_W].reshape(D_MODEL, 3, N_GROUPS, GROUP_W)
                 .transpose(0, 2, 1, 3).reshape(D_MODEL, 3 * QKV_W).astype(BF16))
        qkv_groups = _qkv(h3, g_mix, w_qkv)
        rec, gl = _lru(h3, g_mix, w_in[l][:, 3 * QKV_W:].astype(BF16),
                       conv_w[l], conv_b[l][None],
                       lru_w_x[l].astype(BF16), lru_b_x[l].reshape(1, LRU_W),
                       lru_w_a[l].astype(BF16), lru_b_a[l].reshape(1, LRU_W),
                       lru_a_param[l][None])
        o_list, lse_list = [], []
        for g in range(N_GROUPS):
            o_g, lse_g = _attn_group(g, qkv_groups[g], rel_bias_table, bkt)
            o_list.append(o_g)
            lse_list.append(lse_g)
        h3 = _merge(h3, o_list, lse_list, rec, gl,
                    w_attn_branch[l].astype(BF16), w_rec_branch[l].astype(BF16),
                    w_out[l].astype(BF16), mix_norm_post[l][None])

        h = _ffn(h3.reshape(n, D_MODEL), ffn2_norm_pre[l][None], ffn2_norm_post[l][None],
                 ffn2_w_gate[l].astype(BF16), ffn2_w_up[l].astype(BF16), ffn2_w_down[l].astype(BF16))
    return h.reshape(bsz, seq, D_MODEL)
```

```python
import functools
import math

import numpy as np
import jax
import jax.numpy as jnp
from jax import lax
from jax.experimental import pallas as pl
from jax.experimental.pallas import tpu as pltpu

F32 = jnp.float32
BF16 = jnp.bfloat16

D_MODEL = 1024
HEAD_DIM = 128
HEADS_PER_GROUP = 4
DILATIONS = (1, 4, 16)
BAND = 128
N_GROUPS = 3
GROUP_W = HEADS_PER_GROUP * HEAD_DIM
QKV_W = N_GROUPS * GROUP_W
LRU_W = D_MODEL
LRU_HEADS = 4
LRU_HD = LRU_W // LRU_HEADS
CONV_W = 4
LRU_C = 8.0
D_FF = 2816
REL_BUCKETS = 32
REL_MAX_DISTANCE = 2048
EPS = 1e-6
TINY = 1e-30

LANES = 128
SUBLANES = 8
VMEM_LIMIT = 56 * 1024 * 1024

TM_FFN = 512
TM = 256
ATT_R = 512
SEG = TM // SUBLANES
PITCH = SEG + SUBLANES
PIECE_W = 512


def _rms(x, g):
    ms = jnp.mean(x * x, axis=-1, keepdims=True)
    return x * lax.rsqrt(ms + EPS) * g


def _sigmoid(x):
    return 1.0 / (1.0 + jnp.exp(-x))


def _gelu_tanh(x):
    c = math.sqrt(2.0 / math.pi)
    inner = x * (c + (c * 0.044715) * (x * x))
    return (0.5 * x) * (1.0 + jnp.tanh(inner))


def _const_spec(shape):
    nd = len(shape)
    return pl.BlockSpec(shape, lambda *_: (0,) * nd, pipeline_mode=pl.Buffered(1))


def _ffn_kernel(x_ref, gpre_ref, gpost_ref, wg_ref, wu_ref, wd_ref, o_ref):
    x = x_ref[...]
    xn = _rms(x, gpre_ref[...]).astype(BF16)
    g = jnp.dot(xn, wg_ref[...], preferred_element_type=F32)
    u = jnp.dot(xn, wu_ref[...], preferred_element_type=F32)
    a = (g * _sigmoid(g) * u).astype(BF16)
    f = jnp.dot(a, wd_ref[...], preferred_element_type=F32)
    o_ref[...] = x + 0.5 * _rms(f, gpost_ref[...])


def _ffn(x2, g_pre, g_post, wg, wu, wd):
    n = x2.shape[0]
    row = pl.BlockSpec((TM_FFN, D_MODEL), lambda i: (i, 0))
    return pl.pallas_call(
        _ffn_kernel,
        name="ffn",
        grid=(n // TM_FFN,),
        in_specs=[row, _const_spec((1, D_MODEL)), _const_spec((1, D_MODEL)),
                  _const_spec((D_MODEL, D_FF)), _const_spec((D_MODEL, D_FF)),
                  _const_spec((D_FF, D_MODEL))],
        out_specs=row,
        out_shape=jax.ShapeDtypeStruct((n, D_MODEL), F32),
        compiler_params=pltpu.CompilerParams(
            dimension_semantics=("arbitrary",), vmem_limit_bytes=VMEM_LIMIT),
    )(x2, g_pre, g_post, wg, wu, wd)


def _mixin_kernel(x_ref, g_ref, wqkv_ref, wlru_ref, cw_ref, cb_ref, wx_ref, bx_ref, wa_ref, ba_ref, ap_ref,
                  o0_ref, o1_ref, o2_ref, rec_ref, gl_ref,
                  u_s, sc, xbuf, a_s, b_s, gy_s, hcar, cin_s):
    @pl.when(pl.program_id(1) == 0)
    def _():
        xbuf[0:SUBLANES, :] = jnp.zeros((SUBLANES, LRU_W), F32)
        hcar[...] = jnp.zeros(hcar.shape, F32)

    u_s[...] = _rms(x_ref[0], g_ref[...]).astype(BF16)
    gw = 3 * GROUP_W
    pslabs = PIECE_W // LANES
    slabs_per_head = LRU_HD // LANES
    ap = -ap_ref[...]
    c_softplus = LRU_C * (jnp.maximum(ap, 0.0) + jnp.log1p(jnp.exp(-jnp.abs(ap))))
    state = {}

    def proj_x(k):
        cols = slice(k * PIECE_W, (k + 1) * PIECE_W)
        xbuf[SUBLANES:SUBLANES + TM, cols] = jnp.dot(u_s[...], wlru_ref[:, cols], preferred_element_type=F32)

    def proj_y(k):
        cols = slice(k * PIECE_W, (k + 1) * PIECE_W)
        yr = jnp.dot(u_s[...], wlru_ref[:, LRU_W + k * PIECE_W:LRU_W + (k + 1) * PIECE_W], preferred_element_type=F32)
        gy_s[:, cols] = _gelu_tanh(yr)

    def proj_gl(k):
        cols = slice(k * PIECE_W, (k + 1) * PIECE_W)
        gl_ref[0, :, cols] = jnp.dot(u_s[...], wlru_ref[:, 2 * LRU_W + k * PIECE_W:2 * LRU_W + (k + 1) * PIECE_W],
                                     preferred_element_type=F32)

    def proj_qkv(gi, k):
        cols = slice(k * PIECE_W, (k + 1) * PIECE_W)
        res = jnp.dot(u_s[...], wqkv_ref[:, gi * gw + k * PIECE_W:gi * gw + (k + 1) * PIECE_W],
                      preferred_element_type=F32)
        if gi == 0:
            o0_ref[0, 0, :, cols] = res.astype(BF16)
            return
        o_ref = (o1_ref, o2_ref)[gi - 1]
        d = DILATIONS[gi]
        base = ((gi - 1) * (gw // PIECE_W) + k) * pslabs
        for c in range(pslabs):
            sc[base + c] = res[:, c * LANES:(c + 1) * LANES]
        for p in range(d):
            for c in range(pslabs):
                o_ref[0, p, :, k * PIECE_W + c * LANES:k * PIECE_W + (c + 1) * LANES] = (
                    sc[base + c, pl.ds(p, TM // d, stride=d), :].astype(BF16))

    def gate_dots(i):
        xh = state["xcb", i]
        state["pre_x", i] = jnp.dot(xh, wx_ref[i], preferred_element_type=F32)
        state["pre_a", i] = jnp.dot(xh, wa_ref[i], preferred_element_type=F32)

    def conv(i):
        cols = slice(i * LRU_HD, (i + 1) * LRU_HD)
        xc = cb_ref[:, cols] + cw_ref[0:1, cols] * xbuf[SUBLANES - 3:SUBLANES - 3 + TM, cols]
        for j in range(1, CONV_W):
            off = SUBLANES - (CONV_W - 1) + j
            xc = xc + cw_ref[j:j + 1, cols] * xbuf[off:off + TM, cols]
        xbuf[0:SUBLANES, cols] = xbuf[TM:TM + SUBLANES, cols]
        state["xc", i] = xc
        state["xcb", i] = xc.astype(BF16)

    def gates(i):
        cols = slice(i * LRU_HD, (i + 1) * LRU_HD)
        gx = _sigmoid(state["pre_x", i] + bx_ref[:, cols])
        ga = _sigmoid(state["pre_a", i] + ba_ref[:, cols])
        nla = ga * c_softplus[:, cols]
        a = jnp.exp(-nla)
        t = jnp.tanh(nla)
        w = (t + t) / (1.0 + t)
        b = (w * lax.rsqrt(jnp.maximum(w, TINY))) * (gx * state["xc", i])
        for k in range(slabs_per_head):
            c = i * slabs_per_head + k
            for s in range(SUBLANES):
                a_s[c, s * PITCH:s * PITCH + SEG, :] = a[s * SEG:(s + 1) * SEG, k * LANES:(k + 1) * LANES]
                b_s[c, s * PITCH:s * PITCH + SEG, :] = b[s * SEG:(s + 1) * SEG, k * LANES:(k + 1) * LANES]

    def scan(i):
        slab_ids = range(i * slabs_per_head, (i + 1) * slabs_per_head)
        hs = {c: jnp.zeros((SUBLANES, LANES), F32) for c in slab_ids}
        cum = {c: jnp.ones((SUBLANES, LANES), F32) for c in slab_ids}
        for j in range(SEG):
            idx = pl.ds(j, SUBLANES, stride=PITCH)
            for c in slab_ids:
                aj = a_s[c, idx, :]
                hs[c] = aj * hs[c] + b_s[c, idx, :]
                cum[c] = aj * cum[c]
                a_s[c, idx, :] = cum[c]
                b_s[c, idx, :] = hs[c]
        row = lax.broadcasted_iota(jnp.int32, (SUBLANES, LANES), 0)
        for c in slab_ids:
            cols = slice(c * LANES, (c + 1) * LANES)
            h_in = jnp.broadcast_to(hcar[SUBLANES - 1:SUBLANES, cols], (SUBLANES, LANES))
            cin = h_in
            for _ in range(SUBLANES - 1):
                out = hs[c] + cum[c] * cin
                cin = jnp.where(row == 0, h_in, pltpu.roll(out, 1, 0))
            cin_s[c] = cin
            hcar[:, cols] = hs[c] + cum[c] * cin
        for c in slab_ids:
            cols = slice(c * LANES, (c + 1) * LANES)
            for s in range(SUBLANES):
                rows = slice(s * SEG, (s + 1) * SEG)
                prow = slice(s * PITCH, s * PITCH + SEG)
                h = b_s[c, prow, :] + a_s[c, prow, :] * cin_s[c, s:s + 1, :]
                rec_ref[0, rows, cols] = (h * gy_s[rows, cols]).astype(BF16)

    heads_per_piece = PIECE_W // LRU_HD
    filler = ([functools.partial(proj_qkv, gi, k) for gi in range(N_GROUPS) for k in range(gw // PIECE_W)]
              + [functools.partial(proj_gl, k) for k in range(2 * D_MODEL // PIECE_W)])
    filler = iter(filler)

    def fill(n=1):
        for _ in range(n):
            piece = next(filler, None)
            if piece is not None:
                piece()

    for k in range(LRU_W // PIECE_W):
        proj_x(k)
    for k in range(LRU_W // PIECE_W):
        proj_y(k)
        for i in range(k * heads_per_piece, (k + 1) * heads_per_piece):
            conv(i)
            fill()
            gate_dots(i)
    for i in range(LRU_HEADS):
        gates(i)
        fill()
        scan(i)
        fill()
    for piece in filler:
        piece()


def _mixin(h3, g, w_qkv, w_xyg, conv_w, conv_b, wx, bx, wa, ba, ap):
    bsz, seq, _ = h3.shape
    n_slab = LRU_W // LANES
    gw = 3 * GROUP_W

    def row(w):
        return pl.BlockSpec((1, TM, w), lambda b, t: (b, t, 0))

    out_specs, out_shape = [], []
    for d in DILATIONS:
        out_specs.append(pl.BlockSpec((1, d, TM // d, gw), lambda b, t: (b, 0, t, 0)))
        out_shape.append(jax.ShapeDtypeStruct((bsz, d, seq // d, gw), BF16))
    out_specs += [row(LRU_W), row(2 * D_MODEL)]
    out_shape += [jax.ShapeDtypeStruct((bsz, seq, LRU_W), BF16),
                  jax.ShapeDtypeStruct((bsz, seq, 2 * D_MODEL), F32)]
    return pl.pallas_call(
        _mixin_kernel,
        name="mixin",
        grid=(bsz, seq // TM),
        in_specs=[row(D_MODEL), _const_spec((1, D_MODEL)),
                  _const_spec((D_MODEL, N_GROUPS * gw)), _const_spec((D_MODEL, 4 * LRU_W)),
                  _const_spec((CONV_W, LRU_W)), _const_spec((1, LRU_W)),
                  _const_spec((LRU_HEADS, LRU_HD, LRU_HD)), _const_spec((1, LRU_W)),
                  _const_spec((LRU_HEADS, LRU_HD, LRU_HD)), _const_spec((1, LRU_W)),
                  _const_spec((1, LRU_W))],
        out_specs=out_specs,
        out_shape=out_shape,
        scratch_shapes=[pltpu.VMEM((TM, D_MODEL), BF16),
                        pltpu.VMEM((2 * gw // LANES, TM, LANES), F32),
                        pltpu.VMEM((TM + SUBLANES, LRU_W), F32),
                        pltpu.VMEM((n_slab, SUBLANES * PITCH, LANES), F32),
                        pltpu.VMEM((n_slab, SUBLANES * PITCH, LANES), F32),
                        pltpu.VMEM((TM, LRU_W), F32),
                        pltpu.VMEM((SUBLANES, LRU_W), F32),
                        pltpu.VMEM((n_slab, SUBLANES, LANES), F32)],
        compiler_params=pltpu.CompilerParams(
            dimension_semantics=("arbitrary", "arbitrary"), vmem_limit_bytes=VMEM_LIMIT),
    )(h3, g, w_qkv, w_xyg, conv_w, conv_b, wx, bx, wa, ba, ap)


def _bucket_table():
    qi = np.arange(BAND)[:, None]
    kj = np.arange(BAND)[None, :]
    max_exact = REL_BUCKETS // 2
    out = np.zeros((N_GROUPS, 2, BAND, BAND), np.int32)
    for g, d in enumerate(DILATIONS):
        for half in range(2):
            steps = qi + BAND - kj if half == 0 else qi - kj
            valid = (steps >= 0) & (steps <= BAND)
            dist = np.maximum(steps, 0) * d
            nf = np.maximum(dist, 1).astype(np.float32)
            large = max_exact + (np.log(nf / np.float32(max_exact))
                                 / np.float32(math.log(REL_MAX_DISTANCE / max_exact))
                                 * np.float32(REL_BUCKETS - max_exact)).astype(np.int32)
            large = np.minimum(large, REL_BUCKETS - 1)
            bucket = np.where(dist < max_exact, dist, large)
            out[g, half] = np.where(valid, bucket, -1)
    return out


def _attn_kernel(group, rows, tab_ref, bkt_ref, q_ref, k_ref, v_ref, o_ref, lse_ref,
                 kbuf, vbuf, bias_s):
    first = (pl.program_id(0) == 0) & (pl.program_id(1) == 0) & (pl.program_id(2) == 0)
    c = pl.program_id(2)
    log2e = math.log2(math.e)
    vw = 2 * HEAD_DIM

    @pl.when(first)
    def _():
        for h in range(HEADS_PER_GROUP):
            for half in range(2):
                bk = bkt_ref[half]
                bias = jnp.full((BAND, BAND), -jnp.inf, F32)
                for n in range(REL_BUCKETS):
                    bias = jnp.where(bk == n, tab_ref[n, group * HEADS_PER_GROUP + h] * log2e, bias)
                bias_s[h, :, half * BAND:(half + 1) * BAND] = bias
            vbuf[:, h * vw + HEAD_DIM:(h + 1) * vw] = jnp.ones((BAND + rows, HEAD_DIM), BF16)

    @pl.when(c == 0)
    def _():
        kbuf[0:BAND, :] = jnp.zeros((BAND, GROUP_W), BF16)
        for h in range(HEADS_PER_GROUP):
            vbuf[0:BAND, h * vw:h * vw + HEAD_DIM] = jnp.zeros((BAND, HEAD_DIM), BF16)

    kbuf[BAND:BAND + rows, :] = k_ref[...]
    for h in range(HEADS_PER_GROUP):
        vbuf[BAND:BAND + rows, h * vw:h * vw + HEAD_DIM] = v_ref[:, h * HEAD_DIM:(h + 1) * HEAD_DIM]

    lane2 = lax.broadcasted_iota(jnp.int32, (1, 2 * BAND), 1)
    pen = jnp.where((lane2 < BAND) & (c == 0), -jnp.inf, 0.0).astype(F32)
    qk_scale = log2e / math.sqrt(HEAD_DIM)
    nt = (((1,), (1,)), ((), ()))
    units = [(n, h) for n in range(rows // BAND) for h in range(HEADS_PER_GROUP)]

    scores = []
    for n, h in units:
        cols = slice(h * HEAD_DIM, (h + 1) * HEAD_DIM)
        q = q_ref[n * BAND:(n + 1) * BAND, cols]
        kk = kbuf[n * BAND:(n + 2) * BAND, cols]
        s = lax.dot_general(q, kk, nt, preferred_element_type=F32) * qk_scale + bias_s[h]
        if n == 0:
            s = s + pen
        scores.append(s)
    maxes = [jnp.max(s, axis=-1, keepdims=True) for s in scores]
    probs = [jnp.exp2(s - m).astype(BF16) for s, m in zip(scores, maxes)]
    outs = [jnp.dot(p, vbuf[n * BAND:(n + 2) * BAND, h * vw:(h + 1) * vw], preferred_element_type=F32)
            for (n, h), p in zip(units, probs)]
    lane = lax.broadcasted_iota(jnp.int32, (BAND, LANES), 1)
    ln2 = math.log(2.0)
    lse_blk = None
    for (n, h), o, m in zip(units, outs, maxes):
        l = o[:, HEAD_DIM:]
        o_ref[n * BAND:(n + 1) * BAND, h * HEAD_DIM:(h + 1) * HEAD_DIM] = (o[:, :HEAD_DIM] * (1.0 / l)).astype(BF16)
        lse = (m + jnp.log2(l)) * ln2
        lse_blk = lse if h == 0 else jnp.where(lane == h, lse, lse_blk)
        if h == HEADS_PER_GROUP - 1:
            lse_ref[n * BAND:(n + 1) * BAND, :] = lse_blk

    kbuf[0:BAND, :] = kbuf[rows:rows + BAND, :]
    vbuf[0:BAND, :] = vbuf[rows:rows + BAND, :]


def _attn_group(group, qkv_g, table, bkt):
    bsz, d, sub, _ = qkv_g.shape
    rows = min(ATT_R, sub)
    assert sub % rows == 0

    def spec(width, col):
        return pl.BlockSpec((None, None, rows, width), lambda b, p, c: (b, p, c, col))

    return pl.pallas_call(
        functools.partial(_attn_kernel, group, rows),
        name=f"attn{group}",
        grid=(bsz, d, sub // rows),
        in_specs=[pl.BlockSpec(memory_space=pltpu.SMEM),
                  pl.BlockSpec((None, 2, BAND, BAND), lambda b, p, c: (group, 0, 0, 0)),
                  spec(GROUP_W, 0), spec(GROUP_W, 1), spec(GROUP_W, 2)],
        out_specs=[spec(GROUP_W, 0), spec(LANES, 0)],
        out_shape=[jax.ShapeDtypeStruct((bsz, d, sub, GROUP_W), BF16),
                   jax.ShapeDtypeStruct((bsz, d, sub, LANES), F32)],
        scratch_shapes=[pltpu.VMEM((BAND + rows, GROUP_W), BF16),
                        pltpu.VMEM((BAND + rows, 2 * GROUP_W), BF16),
                        pltpu.VMEM((HEADS_PER_GROUP, BAND, 2 * BAND), F32)],
        compiler_params=pltpu.CompilerParams(
            dimension_semantics=("arbitrary", "arbitrary", "arbitrary"),
            vmem_limit_bytes=VMEM_LIMIT),
    )(table, bkt, qkv_g, qkv_g, qkv_g)


def _merge_kernel(h_ref, o0_ref, o1_ref, o2_ref, l0_ref, l1_ref, l2_ref, rec_ref, gl_ref,
                  wat_ref, wrec_ref, wout_ref, gpost_ref, out_ref, osc, lsc):
    for gi, o_ref, l_ref in ((1, o1_ref, l1_ref), (2, o2_ref, l2_ref)):
        d = DILATIONS[gi]
        for p in range(d):
            idx = pl.ds(p, TM // d, stride=d)
            lsc[gi - 1, idx, :] = l_ref[p]
            for h in range(HEADS_PER_GROUP):
                osc[(gi - 1) * HEADS_PER_GROUP + h, idx, :] = (
                    o_ref[p, :, h * HEAD_DIM:(h + 1) * HEAD_DIM].astype(F32))
    l0, l1, l2 = l0_ref[0], lsc[0], lsc[1]
    mx = jnp.maximum(jnp.maximum(l0, l1), l2)
    w0, w1, w2 = jnp.exp(l0 - mx), jnp.exp(l1 - mx), jnp.exp(l2 - mx)
    inv = 1.0 / (w0 + w1 + w2)
    w0, w1, w2 = w0 * inv, w1 * inv, w2 * inv
    parts = []
    for h in range(HEADS_PER_GROUP):
        cols = slice(h * HEAD_DIM, (h + 1) * HEAD_DIM)
        parts.append(w0[:, h:h + 1] * o0_ref[0, :, cols].astype(F32)
                     + w1[:, h:h + 1] * osc[h]
                     + w2[:, h:h + 1] * osc[HEADS_PER_GROUP + h])
    attn = jnp.concatenate(parts, axis=-1).astype(BF16)
    attn_d = jnp.dot(attn, wat_ref[...], preferred_element_type=F32)
    rec_d = jnp.dot(rec_ref[...], wrec_ref[...], preferred_element_type=F32)
    gl = gl_ref[...]
    merged = _sigmoid(gl[:, :D_MODEL]) * attn_d + _sigmoid(gl[:, D_MODEL:]) * rec_d
    mo = jnp.dot(merged.astype(BF16), wout_ref[...], preferred_element_type=F32)
    out_ref[...] = h_ref[...] + _rms(mo, gpost_ref[...])


def _merge(h3, o_list, lse_list, rec3, gl3, w_attn, w_rec, w_out, g_post):
    bsz, seq, _ = h3.shape

    def row(w):
        return pl.BlockSpec((None, TM, w), lambda b, i: (b, i, 0))

    def phased(d, w):
        return pl.BlockSpec((None, d, TM // d, w), lambda b, i: (b, 0, i, 0))

    return pl.pallas_call(
        _merge_kernel,
        name="merge",
        grid=(bsz, seq // TM),
        in_specs=[row(D_MODEL)] + [phased(d, GROUP_W) for d in DILATIONS]
                 + [phased(d, LANES) for d in DILATIONS] + [row(LRU_W), row(2 * D_MODEL),
                 _const_spec((GROUP_W, D_MODEL)), _const_spec((LRU_W, D_MODEL)),
                 _const_spec((D_MODEL, D_MODEL)), _const_spec((1, D_MODEL))],
        out_specs=row(D_MODEL),
        out_shape=jax.ShapeDtypeStruct((bsz, seq, D_MODEL), F32),
        scratch_shapes=[pltpu.VMEM((2 * HEADS_PER_GROUP, TM, LANES), F32),
                        pltpu.VMEM((2, TM, LANES), F32)],
        compiler_params=pltpu.CompilerParams(
            dimension_semantics=("arbitrary", "arbitrary"), vmem_limit_bytes=VMEM_LIMIT),
    )(h3, *o_list, *lse_list, rec3, gl3, w_attn, w_rec, w_out, g_post)


def kernel(x, ffn1_norm_pre, ffn1_norm_post, ffn1_w_gate, ffn1_w_up, ffn1_w_down, mix_norm_pre, mix_norm_post, w_in, rel_bias_table, conv_w, conv_b, lru_w_x, lru_b_x, lru_w_a, lru_b_a, lru_a_param, w_attn_branch, w_rec_branch, w_out, ffn2_norm_pre, ffn2_norm_post, ffn2_w_gate, ffn2_w_up, ffn2_w_down):
    bsz, seq, _ = x.shape
    n = bsz * seq
    depth = w_in.shape[0]
    bkt = jnp.asarray(_bucket_table())
    h = x.reshape(n, D_MODEL)
    for l in range(depth):
        h = _ffn(h, ffn1_norm_pre[l][None], ffn1_norm_post[l][None],
                 ffn1_w_gate[l].astype(BF16), ffn1_w_up[l].astype(BF16), ffn1_w_down[l].astype(BF16))

        h3 = h.reshape(bsz, seq, D_MODEL)

        g_mix = mix_norm_pre[l][None]
        w_qkv = (w_in[l][:, :3 * QKV_W].reshape(D_MODEL, 3, N_GROUPS, GROUP_W)
                 .transpose(0, 2, 1, 3).reshape(D_MODEL, 3 * QKV_W).astype(BF16))
        *qkv_groups, rec, gl = _mixin(h3, g_mix, w_qkv, w_in[l][:, 3 * QKV_W:].astype(BF16),
                                      conv_w[l], conv_b[l][None],
                                      lru_w_x[l].astype(BF16), lru_b_x[l].reshape(1, LRU_W),
                                      lru_w_a[l].astype(BF16), lru_b_a[l].reshape(1, LRU_W),
                                      lru_a_param[l][None])
        o_list, lse_list = [], []
        for g in range(N_GROUPS):
            o_g, lse_g = _attn_group(g, qkv_groups[g], rel_bias_table, bkt)
            o_list.append(o_g)
            lse_list.append(lse_g)
        h3 = _merge(h3, o_list, lse_list, rec, gl,
                    w_attn_branch[l].astype(BF16), w_rec_branch[l].astype(BF16),
                    w_out[l].astype(BF16), mix_norm_post[l][None])

        h = _ffn(h3.reshape(n, D_MODEL), ffn2_norm_pre[l][None], ffn2_norm_post[l][None],
                 ffn2_w_gate[l].astype(BF16), ffn2_w_up[l].astype(BF16), ffn2_w_down[l].astype(BF16))
    return h.reshape(bsz, seq, D_MODEL)
```

```python
import functools
import math

import numpy as np
import jax
import jax.numpy as jnp
from jax import lax
from jax.experimental import pallas as pl
from jax.experimental.pallas import tpu as pltpu

F32 = jnp.float32
BF16 = jnp.bfloat16

D_MODEL = 1024
HEAD_DIM = 128
HEADS_PER_GROUP = 4
DILATIONS = (1, 4, 16)
BAND = 128
N_GROUPS = 3
GROUP_W = HEADS_PER_GROUP * HEAD_DIM
QKV_W = N_GROUPS * GROUP_W
LRU_W = D_MODEL
LRU_HEADS = 4
LRU_HD = LRU_W // LRU_HEADS
CONV_W = 4
LRU_C = 8.0
D_FF = 2816
REL_BUCKETS = 32
REL_MAX_DISTANCE = 2048
EPS = 1e-6
TINY = 1e-30

LANES = 128
SUBLANES = 8
VMEM_LIMIT = 56 * 1024 * 1024

TM_FFN = 512
TM = 512
ATT_R = 512
SEG = TM // SUBLANES
PITCH = SEG + SUBLANES
PIECE_W = 256


def _rms(x, g):
    ms = jnp.mean(x * x, axis=-1, keepdims=True)
    return x * lax.rsqrt(ms + EPS) * g


def _sigmoid(x):
    return 1.0 / (1.0 + jnp.exp(-x))


def _gelu_tanh(x):
    c = math.sqrt(2.0 / math.pi)
    inner = x * (c + (c * 0.044715) * (x * x))
    return (0.5 * x) * (1.0 + jnp.tanh(inner))


def _const_spec(shape):
    nd = len(shape)
    return pl.BlockSpec(shape, lambda *_: (0,) * nd, pipeline_mode=pl.Buffered(1))


def _ffn_kernel(x_ref, gpre_ref, gpost_ref, gnext_ref, wg_ref, wu_ref, wd_ref, o_ref, *maybe_normed_ref):
    x = x_ref[...]
    xn = _rms(x, gpre_ref[...]).astype(BF16)
    g = jnp.dot(xn, wg_ref[...], preferred_element_type=F32)
    u = jnp.dot(xn, wu_ref[...], preferred_element_type=F32)
    a = (g * _sigmoid(g) * u).astype(BF16)
    f = jnp.dot(a, wd_ref[...], preferred_element_type=F32)
    h = x + 0.5 * _rms(f, gpost_ref[...])
    o_ref[...] = h
    for normed_ref in maybe_normed_ref:
        normed_ref[...] = _rms(h, gnext_ref[...]).astype(BF16)


def _ffn(x2, g_pre, g_post, wg, wu, wd, g_next=None):
    n = x2.shape[0]
    row = pl.BlockSpec((TM_FFN, D_MODEL), lambda i: (i, 0))
    emit_normed = g_next is not None
    out_specs = [row, row] if emit_normed else row
    out_f32 = jax.ShapeDtypeStruct((n, D_MODEL), F32)
    out_shape = [out_f32, jax.ShapeDtypeStruct((n, D_MODEL), BF16)] if emit_normed else out_f32
    return pl.pallas_call(
        _ffn_kernel,
        name="ffn",
        grid=(n // TM_FFN,),
        in_specs=[row, _const_spec((1, D_MODEL)), _const_spec((1, D_MODEL)), _const_spec((1, D_MODEL)),
                  _const_spec((D_MODEL, D_FF)), _const_spec((D_MODEL, D_FF)),
                  _const_spec((D_FF, D_MODEL))],
        out_specs=out_specs,
        out_shape=out_shape,
        compiler_params=pltpu.CompilerParams(
            dimension_semantics=("arbitrary",), vmem_limit_bytes=VMEM_LIMIT),
    )(x2, g_pre, g_post, g_next if emit_normed else g_post, wg, wu, wd)


def _mixin_kernel(u_ref, wqkv_ref, wlru_ref, cw_ref, cb_ref, wx_ref, bx_ref, wa_ref, ba_ref, ap_ref,
                  o0_ref, o1_ref, o2_ref, rec_ref, gl_ref,
                  sc, xbuf, a_s, b_s, gy_s, hcar, cin_s):
    @pl.when(pl.program_id(1) == 0)
    def _():
        xbuf[0:SUBLANES, :] = jnp.zeros((SUBLANES, LRU_W), F32)
        hcar[...] = jnp.zeros(hcar.shape, F32)

    gw = 3 * GROUP_W
    pslabs = PIECE_W // LANES
    slabs_per_head = LRU_HD // LANES
    ap = -ap_ref[...]
    c_softplus = LRU_C * (jnp.maximum(ap, 0.0) + jnp.log1p(jnp.exp(-jnp.abs(ap))))
    state = {}

    def proj_x(k):
        cols = slice(k * PIECE_W, (k + 1) * PIECE_W)
        xbuf[SUBLANES:SUBLANES + TM, cols] = jnp.dot(u_ref[0],wlru_ref[:, cols], preferred_element_type=F32)

    def proj_y(k):
        cols = slice(k * PIECE_W, (k + 1) * PIECE_W)
        yr = jnp.dot(u_ref[0],wlru_ref[:, LRU_W + k * PIECE_W:LRU_W + (k + 1) * PIECE_W], preferred_element_type=F32)
        gy_s[:, cols] = _gelu_tanh(yr)

    def proj_gl(k):
        cols = slice(k * PIECE_W, (k + 1) * PIECE_W)
        gl_ref[0, :, cols] = jnp.dot(u_ref[0],wlru_ref[:, 2 * LRU_W + k * PIECE_W:2 * LRU_W + (k + 1) * PIECE_W],
                                     preferred_element_type=F32).astype(BF16)

    def proj_qkv(gi, k):
        cols = slice(k * PIECE_W, (k + 1) * PIECE_W)
        res = jnp.dot(u_ref[0],wqkv_ref[:, gi * gw + k * PIECE_W:gi * gw + (k + 1) * PIECE_W],
                      preferred_element_type=F32)
        if gi == 0:
            o0_ref[0, 0, :, cols] = res.astype(BF16)
            return
        o_ref = (o1_ref, o2_ref)[gi - 1]
        d = DILATIONS[gi]
        base = (((gi - 1) * (gw // PIECE_W) + k) % 2) * pslabs
        for c in range(pslabs):
            sc[base + c] = res[:, c * LANES:(c + 1) * LANES]
        for p in range(d):
            for c in range(pslabs):
                o_ref[0, p, :, k * PIECE_W + c * LANES:k * PIECE_W + (c + 1) * LANES] = (
                    sc[base + c, pl.ds(p, TM // d, stride=d), :].astype(BF16))

    def gate_dots(i):
        xh = state["xcb", i]
        state["pre_x", i] = jnp.dot(xh, wx_ref[i], preferred_element_type=F32)
        state["pre_a", i] = jnp.dot(xh, wa_ref[i], preferred_element_type=F32)

    def conv(i):
        cols = slice(i * LRU_HD, (i + 1) * LRU_HD)
        xc = cb_ref[:, cols] + cw_ref[0:1, cols] * xbuf[SUBLANES - 3:SUBLANES - 3 + TM, cols]
        for j in range(1, CONV_W):
            off = SUBLANES - (CONV_W - 1) + j
            xc = xc + cw_ref[j:j + 1, cols] * xbuf[off:off + TM, cols]
        xbuf[0:SUBLANES, cols] = xbuf[TM:TM + SUBLANES, cols]
        state["xc", i] = xc
        state["xcb", i] = xc.astype(BF16)

    def gates(i):
        cols = slice(i * LRU_HD, (i + 1) * LRU_HD)
        gx = _sigmoid(state["pre_x", i] + bx_ref[:, cols])
        ga = _sigmoid(state["pre_a", i] + ba_ref[:, cols])
        nla = ga * c_softplus[:, cols]
        a = jnp.exp(-nla)
        t = jnp.tanh(nla)
        w = (t + t) / (1.0 + t)
        b = (w * lax.rsqrt(jnp.maximum(w, TINY))) * (gx * state["xc", i])
        for k in range(slabs_per_head):
            c = i * slabs_per_head + k
            for s in range(SUBLANES):
                a_s[c, s * PITCH:s * PITCH + SEG, :] = a[s * SEG:(s + 1) * SEG, k * LANES:(k + 1) * LANES]
                b_s[c, s * PITCH:s * PITCH + SEG, :] = b[s * SEG:(s + 1) * SEG, k * LANES:(k + 1) * LANES]

    def scan(i):
        slab_ids = range(i * slabs_per_head, (i + 1) * slabs_per_head)
        hs = {c: jnp.zeros((SUBLANES, LANES), F32) for c in slab_ids}
        cum = {c: jnp.ones((SUBLANES, LANES), F32) for c in slab_ids}
        for j in range(SEG):
            idx = pl.ds(j, SUBLANES, stride=PITCH)
            for c in slab_ids:
                aj = a_s[c, idx, :]
                hs[c] = aj * hs[c] + b_s[c, idx, :]
                cum[c] = aj * cum[c]
                a_s[c, idx, :] = cum[c]
                b_s[c, idx, :] = hs[c]
        row = lax.broadcasted_iota(jnp.int32, (SUBLANES, LANES), 0)
        for c in slab_ids:
            cols = slice(c * LANES, (c + 1) * LANES)
            h_in = jnp.broadcast_to(hcar[SUBLANES - 1:SUBLANES, cols], (SUBLANES, LANES))
            cin = h_in
            for _ in range(SUBLANES - 1):
                out = hs[c] + cum[c] * cin
                cin = jnp.where(row == 0, h_in, pltpu.roll(out, 1, 0))
            cin_s[c] = cin
            hcar[:, cols] = hs[c] + cum[c] * cin
        for c in slab_ids:
            cols = slice(c * LANES, (c + 1) * LANES)
            for s in range(SUBLANES):
                rows = slice(s * SEG, (s + 1) * SEG)
                prow = slice(s * PITCH, s * PITCH + SEG)
                h = b_s[c, prow, :] + a_s[c, prow, :] * cin_s[c, s:s + 1, :]
                rec_ref[0, rows, cols] = (h * gy_s[rows, cols]).astype(BF16)

    heads_per_piece = PIECE_W // LRU_HD
    filler = ([functools.partial(proj_qkv, gi, k) for gi in range(N_GROUPS) for k in range(gw // PIECE_W)]
              + [functools.partial(proj_gl, k) for k in range(2 * D_MODEL // PIECE_W)])
    filler = iter(filler)

    def fill(n=1):
        for _ in range(n):
            piece = next(filler, None)
            if piece is not None:
                piece()

    for k in range(LRU_W // PIECE_W):
        proj_x(k)
    for k in range(LRU_W // PIECE_W):
        proj_y(k)
        for i in range(k * heads_per_piece, (k + 1) * heads_per_piece):
            conv(i)
            fill()
            gate_dots(i)
    for i in range(LRU_HEADS):
        gates(i)
        fill()
        scan(i)
        fill()
    for piece in filler:
        piece()


def _mixin(u3, w_qkv, w_xyg, conv_w, conv_b, wx, bx, wa, ba, ap):
    bsz, seq, _ = u3.shape
    n_slab = LRU_W // LANES
    gw = 3 * GROUP_W

    def row(w):
        return pl.BlockSpec((1, TM, w), lambda b, t: (b, t, 0))

    out_specs, out_shape = [], []
    for d in DILATIONS:
        out_specs.append(pl.BlockSpec((1, d, TM // d, gw), lambda b, t: (b, 0, t, 0)))
        out_shape.append(jax.ShapeDtypeStruct((bsz, d, seq // d, gw), BF16))
    out_specs += [row(LRU_W), row(2 * D_MODEL)]
    out_shape += [jax.ShapeDtypeStruct((bsz, seq, LRU_W), BF16),
                  jax.ShapeDtypeStruct((bsz, seq, 2 * D_MODEL), BF16)]
    return pl.pallas_call(
        _mixin_kernel,
        name="mixin",
        grid=(bsz, seq // TM),
        in_specs=[row(D_MODEL),
                  _const_spec((D_MODEL, N_GROUPS * gw)), _const_spec((D_MODEL, 4 * LRU_W)),
                  _const_spec((CONV_W, LRU_W)), _const_spec((1, LRU_W)),
                  _const_spec((LRU_HEADS, LRU_HD, LRU_HD)), _const_spec((1, LRU_W)),
                  _const_spec((LRU_HEADS, LRU_HD, LRU_HD)), _const_spec((1, LRU_W)),
                  _const_spec((1, LRU_W))],
        out_specs=out_specs,
        out_shape=out_shape,
        scratch_shapes=[pltpu.VMEM((2 * PIECE_W // LANES, TM, LANES), F32),
                        pltpu.VMEM((TM + SUBLANES, LRU_W), F32),
                        pltpu.VMEM((n_slab, SUBLANES * PITCH, LANES), F32),
                        pltpu.VMEM((n_slab, SUBLANES * PITCH, LANES), F32),
                        pltpu.VMEM((TM, LRU_W), F32),
                        pltpu.VMEM((SUBLANES, LRU_W), F32),
                        pltpu.VMEM((n_slab, SUBLANES, LANES), F32)],
        compiler_params=pltpu.CompilerParams(
            dimension_semantics=("arbitrary", "arbitrary"), vmem_limit_bytes=VMEM_LIMIT),
    )(u3, w_qkv, w_xyg, conv_w, conv_b, wx, bx, wa, ba, ap)


def _bucket_table():
    qi = np.arange(BAND)[:, None]
    kj = np.arange(BAND)[None, :]
    max_exact = REL_BUCKETS // 2
    out = np.zeros((N_GROUPS, 2, BAND, BAND), np.int32)
    for g, d in enumerate(DILATIONS):
        for half in range(2):
            steps = qi + BAND - kj if half == 0 else qi - kj
            valid = (steps >= 0) & (steps <= BAND)
            dist = np.maximum(steps, 0) * d
            nf = np.maximum(dist, 1).astype(np.float32)
            large = max_exact + (np.log(nf / np.float32(max_exact))
                                 / np.float32(math.log(REL_MAX_DISTANCE / max_exact))
                                 * np.float32(REL_BUCKETS - max_exact)).astype(np.int32)
            large = np.minimum(large, REL_BUCKETS - 1)
            bucket = np.where(dist < max_exact, dist, large)
            out[g, half] = np.where(valid, bucket, -1)
    return out


def _attn_kernel(group, rows, tab_ref, bkt_ref, q_ref, k_ref, v_ref, o_ref, lse_ref,
                 kbuf, vbuf, bias_s):
    first = (pl.program_id(0) == 0) & (pl.program_id(1) == 0) & (pl.program_id(2) == 0)
    c = pl.program_id(2)
    log2e = math.log2(math.e)
    vw = 2 * HEAD_DIM

    @pl.when(first)
    def _():
        for h in range(HEADS_PER_GROUP):
            for half in range(2):
                bk = bkt_ref[half]
                bias = jnp.full((BAND, BAND), -jnp.inf, F32)
                for n in range(REL_BUCKETS):
                    bias = jnp.where(bk == n, tab_ref[n, group * HEADS_PER_GROUP + h] * log2e, bias)
                bias_s[h, :, half * BAND:(half + 1) * BAND] = bias
            vbuf[:, h * vw + HEAD_DIM:(h + 1) * vw] = jnp.ones((BAND + rows, HEAD_DIM), BF16)

    @pl.when(c == 0)
    def _():
        kbuf[0:BAND, :] = jnp.zeros((BAND, GROUP_W), BF16)
        for h in range(HEADS_PER_GROUP):
            vbuf[0:BAND, h * vw:h * vw + HEAD_DIM] = jnp.zeros((BAND, HEAD_DIM), BF16)

    kbuf[BAND:BAND + rows, :] = k_ref[...]
    for h in range(HEADS_PER_GROUP):
        vbuf[BAND:BAND + rows, h * vw:h * vw + HEAD_DIM] = v_ref[:, h * HEAD_DIM:(h + 1) * HEAD_DIM]

    lane2 = lax.broadcasted_iota(jnp.int32, (1, 2 * BAND), 1)
    pen = jnp.where((lane2 < BAND) & (c == 0), -jnp.inf, 0.0).astype(F32)
    qk_scale = log2e / math.sqrt(HEAD_DIM)
    nt = (((1,), (1,)), ((), ()))
    units = [(n, h) for n in range(rows // BAND) for h in range(HEADS_PER_GROUP)]

    scores = []
    for n, h in units:
        cols = slice(h * HEAD_DIM, (h + 1) * HEAD_DIM)
        q = q_ref[n * BAND:(n + 1) * BAND, cols]
        kk = kbuf[n * BAND:(n + 2) * BAND, cols]
        s = lax.dot_general(q, kk, nt, preferred_element_type=F32) * qk_scale + bias_s[h]
        if n == 0:
            s = s + pen
        scores.append(s)
    maxes = [jnp.max(s, axis=-1, keepdims=True) for s in scores]
    probs = [jnp.exp2(s - m).astype(BF16) for s, m in zip(scores, maxes)]
    outs = [jnp.dot(p, vbuf[n * BAND:(n + 2) * BAND, h * vw:(h + 1) * vw], preferred_element_type=F32)
            for (n, h), p in zip(units, probs)]
    lane = lax.broadcasted_iota(jnp.int32, (BAND, LANES), 1)
    ln2 = math.log(2.0)
    lse_blk = None
    for (n, h), o, m in zip(units, outs, maxes):
        l = o[:, HEAD_DIM:]
        o_ref[n * BAND:(n + 1) * BAND, h * HEAD_DIM:(h + 1) * HEAD_DIM] = (o[:, :HEAD_DIM] * (1.0 / l)).astype(BF16)
        lse = (m + jnp.log2(l)) * ln2
        lse_blk = lse if h == 0 else jnp.where(lane == h, lse, lse_blk)
        if h == HEADS_PER_GROUP - 1:
            lse_ref[n * BAND:(n + 1) * BAND, :] = lse_blk

    kbuf[0:BAND, :] = kbuf[rows:rows + BAND, :]
    vbuf[0:BAND, :] = vbuf[rows:rows + BAND, :]


def _attn_group(group, qkv_g, table, bkt):
    bsz, d, sub, _ = qkv_g.shape
    rows = min(ATT_R, sub)
    assert sub % rows == 0

    def spec(width, col):
        return pl.BlockSpec((None, None, rows, width), lambda b, p, c: (b, p, c, col))

    return pl.pallas_call(
        functools.partial(_attn_kernel, group, rows),
        name=f"attn{group}",
        grid=(bsz, d, sub // rows),
        in_specs=[pl.BlockSpec(memory_space=pltpu.SMEM),
                  pl.BlockSpec((None, 2, BAND, BAND), lambda b, p, c: (group, 0, 0, 0)),
                  spec(GROUP_W, 0), spec(GROUP_W, 1), spec(GROUP_W, 2)],
        out_specs=[spec(GROUP_W, 0), spec(LANES, 0)],
        out_shape=[jax.ShapeDtypeStruct((bsz, d, sub, GROUP_W), BF16),
                   jax.ShapeDtypeStruct((bsz, d, sub, LANES), F32)],
        scratch_shapes=[pltpu.VMEM((BAND + rows, GROUP_W), BF16),
                        pltpu.VMEM((BAND + rows, 2 * GROUP_W), BF16),
                        pltpu.VMEM((HEADS_PER_GROUP, BAND, 2 * BAND), F32)],
        compiler_params=pltpu.CompilerParams(
            dimension_semantics=("arbitrary", "arbitrary", "arbitrary"),
            vmem_limit_bytes=VMEM_LIMIT),
    )(table, bkt, qkv_g, qkv_g, qkv_g)


def _merge_kernel(h_ref, o0_ref, o1_ref, o2_ref, l0_ref, l1_ref, l2_ref, rec_ref, gl_ref,
                  wat_ref, wrec_ref, wout_ref, gpost_ref, out_ref, osc, lsc):
    for gi, o_ref, l_ref in ((1, o1_ref, l1_ref), (2, o2_ref, l2_ref)):
        d = DILATIONS[gi]
        for p in range(d):
            idx = pl.ds(p, TM // d, stride=d)
            lsc[gi - 1, idx, :] = l_ref[p]
            for h in range(HEADS_PER_GROUP):
                osc[(gi - 1) * HEADS_PER_GROUP + h, idx, :] = (
                    o_ref[p, :, h * HEAD_DIM:(h + 1) * HEAD_DIM].astype(F32))
    l0, l1, l2 = l0_ref[0], lsc[0], lsc[1]
    mx = jnp.maximum(jnp.maximum(l0, l1), l2)
    w0, w1, w2 = jnp.exp(l0 - mx), jnp.exp(l1 - mx), jnp.exp(l2 - mx)
    inv = 1.0 / (w0 + w1 + w2)
    w0, w1, w2 = w0 * inv, w1 * inv, w2 * inv
    parts = []
    for h in range(HEADS_PER_GROUP):
        cols = slice(h * HEAD_DIM, (h + 1) * HEAD_DIM)
        parts.append(w0[:, h:h + 1] * o0_ref[0, :, cols].astype(F32)
                     + w1[:, h:h + 1] * osc[h]
                     + w2[:, h:h + 1] * osc[HEADS_PER_GROUP + h])
    attn = jnp.concatenate(parts, axis=-1).astype(BF16)
    attn_d = jnp.dot(attn, wat_ref[...], preferred_element_type=F32)
    rec_d = jnp.dot(rec_ref[...], wrec_ref[...], preferred_element_type=F32)
    gl = gl_ref[...].astype(F32)
    merged = _sigmoid(gl[:, :D_MODEL]) * attn_d + _sigmoid(gl[:, D_MODEL:]) * rec_d
    mo = jnp.dot(merged.astype(BF16), wout_ref[...], preferred_element_type=F32)
    out_ref[...] = h_ref[...] + _rms(mo, gpost_ref[...])


def _merge(h3, o_list, lse_list, rec3, gl3, w_attn, w_rec, w_out, g_post):
    bsz, seq, _ = h3.shape

    def row(w):
        return pl.BlockSpec((None, TM, w), lambda b, i: (b, i, 0))

    def phased(d, w):
        return pl.BlockSpec((None, d, TM // d, w), lambda b, i: (b, 0, i, 0))

    return pl.pallas_call(
        _merge_kernel,
        name="merge",
        grid=(bsz, seq // TM),
        in_specs=[row(D_MODEL)] + [phased(d, GROUP_W) for d in DILATIONS]
                 + [phased(d, LANES) for d in DILATIONS] + [row(LRU_W), row(2 * D_MODEL),
                 _const_spec((GROUP_W, D_MODEL)), _const_spec((LRU_W, D_MODEL)),
                 _const_spec((D_MODEL, D_MODEL)), _const_spec((1, D_MODEL))],
        out_specs=row(D_MODEL),
        out_shape=jax.ShapeDtypeStruct((bsz, seq, D_MODEL), F32),
        scratch_shapes=[pltpu.VMEM((2 * HEADS_PER_GROUP, TM, LANES), F32),
                        pltpu.VMEM((2, TM, LANES), F32)],
        compiler_params=pltpu.CompilerParams(
            dimension_semantics=("arbitrary", "arbitrary"), vmem_limit_bytes=VMEM_LIMIT),
    )(h3, *o_list, *lse_list, rec3, gl3, w_attn, w_rec, w_out, g_post)


def kernel(x, ffn1_norm_pre, ffn1_norm_post, ffn1_w_gate, ffn1_w_up, ffn1_w_down, mix_norm_pre, mix_norm_post, w_in, rel_bias_table, conv_w, conv_b, lru_w_x, lru_b_x, lru_w_a, lru_b_a, lru_a_param, w_attn_branch, w_rec_branch, w_out, ffn2_norm_pre, ffn2_norm_post, ffn2_w_gate, ffn2_w_up, ffn2_w_down):
    bsz, seq, _ = x.shape
    n = bsz * seq
    depth = w_in.shape[0]
    bkt = jnp.asarray(_bucket_table())
    h = x.reshape(n, D_MODEL)
    for l in range(depth):
        h, u = _ffn(h, ffn1_norm_pre[l][None], ffn1_norm_post[l][None],
                    ffn1_w_gate[l].astype(BF16), ffn1_w_up[l].astype(BF16), ffn1_w_down[l].astype(BF16),
                    g_next=mix_norm_pre[l][None])
        h3 = h.reshape(bsz, seq, D_MODEL)

        w_qkv = (w_in[l][:, :3 * QKV_W].reshape(D_MODEL, 3, N_GROUPS, GROUP_W)
                 .transpose(0, 2, 1, 3).reshape(D_MODEL, 3 * QKV_W).astype(BF16))
        *qkv_groups, rec, gl = _mixin(u.reshape(bsz, seq, D_MODEL), w_qkv, w_in[l][:, 3 * QKV_W:].astype(BF16),
                                      conv_w[l], conv_b[l][None],
                                      lru_w_x[l].astype(BF16), lru_b_x[l].reshape(1, LRU_W),
                                      lru_w_a[l].astype(BF16), lru_b_a[l].reshape(1, LRU_W),
                                      lru_a_param[l][None])
        o_list, lse_list = [], []
        for g in range(N_GROUPS):
            o_g, lse_g = _attn_group(g, qkv_groups[g], rel_bias_table, bkt)
            o_list.append(o_g)
            lse_list.append(lse_g)
        h3 = _merge(h3, o_list, lse_list, rec, gl,
                    w_attn_branch[l].astype(BF16), w_rec_branch[l].astype(BF16),
                    w_out[l].astype(BF16), mix_norm_post[l][None])

        h = _ffn(h3.reshape(n, D_MODEL), ffn2_norm_pre[l][None], ffn2_norm_post[l][None],
                 ffn2_w_gate[l].astype(BF16), ffn2_w_up[l].astype(BF16), ffn2_w_down[l].astype(BF16))
    return h.reshape(bsz, seq, D_MODEL)
```

```python
import functools
import math

import numpy as np
import jax
import jax.numpy as jnp
from jax import lax
from jax.experimental import pallas as pl
from jax.experimental.pallas import tpu as pltpu

F32 = jnp.float32
BF16 = jnp.bfloat16

D_MODEL = 1024
HEAD_DIM = 128
HEADS_PER_GROUP = 4
DILATIONS = (1, 4, 16)
BAND = 128
N_GROUPS = 3
GROUP_W = HEADS_PER_GROUP * HEAD_DIM
QKV_W = N_GROUPS * GROUP_W
LRU_W = D_MODEL
LRU_HEADS = 4
LRU_HD = LRU_W // LRU_HEADS
CONV_W = 4
LRU_C = 8.0
D_FF = 2816
REL_BUCKETS = 32
REL_MAX_DISTANCE = 2048
EPS = 1e-6
TINY = 1e-30

LANES = 128
SUBLANES = 8
VMEM_LIMIT = 56 * 1024 * 1024

TM_FFN = 512
TM = 512
ATT_R = 1024
SEG = TM // SUBLANES
PITCH = SEG + SUBLANES
PIECE_W = 256


def _rms(x, g):
    ms = jnp.mean(x * x, axis=-1, keepdims=True)
    return x * lax.rsqrt(ms + EPS) * g


def _sigmoid(x):
    return 1.0 / (1.0 + jnp.exp(-x))


def _gelu_tanh(x):
    c = math.sqrt(2.0 / math.pi)
    inner = x * (c + (c * 0.044715) * (x * x))
    return (0.5 * x) * (1.0 + jnp.tanh(inner))


def _const_spec(shape):
    nd = len(shape)
    return pl.BlockSpec(shape, lambda *_: (0,) * nd, pipeline_mode=pl.Buffered(1))


def _ffn_kernel(x_ref, gpre_ref, gpost_ref, gnext_ref, wg_ref, wu_ref, wd_ref, o_ref, *maybe_normed_ref):
    x = x_ref[...]
    xn = _rms(x, gpre_ref[...]).astype(BF16)
    g = jnp.dot(xn, wg_ref[...], preferred_element_type=F32)
    u = jnp.dot(xn, wu_ref[...], preferred_element_type=F32)
    a = (g * _sigmoid(g) * u).astype(BF16)
    f = jnp.dot(a, wd_ref[...], preferred_element_type=F32)
    h = x + 0.5 * _rms(f, gpost_ref[...])
    o_ref[...] = h
    for normed_ref in maybe_normed_ref:
        normed_ref[...] = _rms(h, gnext_ref[...]).astype(BF16)


def _ffn(x2, g_pre, g_post, wg, wu, wd, g_next=None):
    n = x2.shape[0]
    row = pl.BlockSpec((TM_FFN, D_MODEL), lambda i: (i, 0))
    emit_normed = g_next is not None
    out_specs = [row, row] if emit_normed else row
    out_f32 = jax.ShapeDtypeStruct((n, D_MODEL), F32)
    out_shape = [out_f32, jax.ShapeDtypeStruct((n, D_MODEL), BF16)] if emit_normed else out_f32
    return pl.pallas_call(
        _ffn_kernel,
        name="ffn",
        grid=(n // TM_FFN,),
        in_specs=[row, _const_spec((1, D_MODEL)), _const_spec((1, D_MODEL)), _const_spec((1, D_MODEL)),
                  _const_spec((D_MODEL, D_FF)), _const_spec((D_MODEL, D_FF)),
                  _const_spec((D_FF, D_MODEL))],
        out_specs=out_specs,
        out_shape=out_shape,
        compiler_params=pltpu.CompilerParams(
            dimension_semantics=("arbitrary",), vmem_limit_bytes=VMEM_LIMIT),
    )(x2, g_pre, g_post, g_next if emit_normed else g_post, wg, wu, wd)


def _mixin_kernel(u_ref, wqkv_ref, wlru_ref, cw_ref, cb_ref, wx_ref, bx_ref, wa_ref, ba_ref, ap_ref,
                  o0_ref, o1_ref, o2_ref, rec_ref, gl_ref,
                  sc, xbuf, a_s, b_s, gy_s, hcar, cin_s):
    @pl.when(pl.program_id(1) == 0)
    def _():
        xbuf[0:SUBLANES, :] = jnp.zeros((SUBLANES, LRU_W), F32)
        hcar[...] = jnp.zeros(hcar.shape, F32)

    gw = 3 * GROUP_W
    pslabs = PIECE_W // LANES
    slabs_per_head = LRU_HD // LANES
    ap = -ap_ref[...]
    c_softplus = LRU_C * (jnp.maximum(ap, 0.0) + jnp.log1p(jnp.exp(-jnp.abs(ap))))
    state = {}

    def proj_x(k):
        cols = slice(k * PIECE_W, (k + 1) * PIECE_W)
        xbuf[SUBLANES:SUBLANES + TM, cols] = jnp.dot(u_ref[0],wlru_ref[:, cols], preferred_element_type=F32)

    def proj_y(k):
        cols = slice(k * PIECE_W, (k + 1) * PIECE_W)
        yr = jnp.dot(u_ref[0],wlru_ref[:, LRU_W + k * PIECE_W:LRU_W + (k + 1) * PIECE_W], preferred_element_type=F32)
        gy_s[:, cols] = _gelu_tanh(yr)

    def proj_gl(k):
        cols = slice(k * PIECE_W, (k + 1) * PIECE_W)
        gl_ref[0, :, cols] = jnp.dot(u_ref[0],wlru_ref[:, 2 * LRU_W + k * PIECE_W:2 * LRU_W + (k + 1) * PIECE_W],
                                     preferred_element_type=F32).astype(BF16)

    def proj_qkv(gi, k):
        cols = slice(k * PIECE_W, (k + 1) * PIECE_W)
        res = jnp.dot(u_ref[0],wqkv_ref[:, gi * gw + k * PIECE_W:gi * gw + (k + 1) * PIECE_W],
                      preferred_element_type=F32)
        if gi == 0:
            o0_ref[0, 0, :, cols] = res.astype(BF16)
            return
        o_ref = (o1_ref, o2_ref)[gi - 1]
        d = DILATIONS[gi]
        base = (((gi - 1) * (gw // PIECE_W) + k) % 2) * pslabs
        for c in range(pslabs):
            sc[base + c] = res[:, c * LANES:(c + 1) * LANES]
        for p in range(d):
            for c in range(pslabs):
                o_ref[0, p, :, k * PIECE_W + c * LANES:k * PIECE_W + (c + 1) * LANES] = (
                    sc[base + c, pl.ds(p, TM // d, stride=d), :].astype(BF16))

    def gate_dots(i):
        xh = state["xcb", i]
        state["pre_x", i] = jnp.dot(xh, wx_ref[i], preferred_element_type=F32)
        state["pre_a", i] = jnp.dot(xh, wa_ref[i], preferred_element_type=F32)

    def conv(i):
        cols = slice(i * LRU_HD, (i + 1) * LRU_HD)
        xc = cb_ref[:, cols] + cw_ref[0:1, cols] * xbuf[SUBLANES - 3:SUBLANES - 3 + TM, cols]
        for j in range(1, CONV_W):
            off = SUBLANES - (CONV_W - 1) + j
            xc = xc + cw_ref[j:j + 1, cols] * xbuf[off:off + TM, cols]
        xbuf[0:SUBLANES, cols] = xbuf[TM:TM + SUBLANES, cols]
        state["xc", i] = xc
        state["xcb", i] = xc.astype(BF16)

    def gates(i):
        cols = slice(i * LRU_HD, (i + 1) * LRU_HD)
        gx = _sigmoid(state["pre_x", i] + bx_ref[:, cols])
        ga = _sigmoid(state["pre_a", i] + ba_ref[:, cols])
        nla = ga * c_softplus[:, cols]
        a = jnp.exp(-nla)
        t = jnp.tanh(nla)
        w = (t + t) / (1.0 + t)
        b = (w * lax.rsqrt(jnp.maximum(w, TINY))) * (gx * state["xc", i])
        for k in range(slabs_per_head):
            c = i * slabs_per_head + k
            for s in range(SUBLANES):
                a_s[c, s * PITCH:s * PITCH + SEG, :] = a[s * SEG:(s + 1) * SEG, k * LANES:(k + 1) * LANES]
                b_s[c, s * PITCH:s * PITCH + SEG, :] = b[s * SEG:(s + 1) * SEG, k * LANES:(k + 1) * LANES]

    def scan(i):
        slab_ids = range(i * slabs_per_head, (i + 1) * slabs_per_head)
        hs = {c: jnp.zeros((SUBLANES, LANES), F32) for c in slab_ids}
        cum = {c: jnp.ones((SUBLANES, LANES), F32) for c in slab_ids}
        for j in range(SEG):
            idx = pl.ds(j, SUBLANES, stride=PITCH)
            for c in slab_ids:
                aj = a_s[c, idx, :]
                hs[c] = aj * hs[c] + b_s[c, idx, :]
                cum[c] = aj * cum[c]
                a_s[c, idx, :] = cum[c]
                b_s[c, idx, :] = hs[c]
        row = lax.broadcasted_iota(jnp.int32, (SUBLANES, LANES), 0)
        for c in slab_ids:
            cols = slice(c * LANES, (c + 1) * LANES)
            h_in = jnp.broadcast_to(hcar[SUBLANES - 1:SUBLANES, cols], (SUBLANES, LANES))
            cin = h_in
            for _ in range(SUBLANES - 1):
                out = hs[c] + cum[c] * cin
                cin = jnp.where(row == 0, h_in, pltpu.roll(out, 1, 0))
            cin_s[c] = cin
            hcar[:, cols] = hs[c] + cum[c] * cin
        for c in slab_ids:
            cols = slice(c * LANES, (c + 1) * LANES)
            for s in range(SUBLANES):
                rows = slice(s * SEG, (s + 1) * SEG)
                prow = slice(s * PITCH, s * PITCH + SEG)
                h = b_s[c, prow, :] + a_s[c, prow, :] * cin_s[c, s:s + 1, :]
                rec_ref[0, rows, cols] = (h * gy_s[rows, cols]).astype(BF16)

    heads_per_piece = PIECE_W // LRU_HD
    filler = ([functools.partial(proj_qkv, gi, k) for gi in range(N_GROUPS) for k in range(gw // PIECE_W)]
              + [functools.partial(proj_gl, k) for k in range(2 * D_MODEL // PIECE_W)])
    filler = iter(filler)

    def fill(n=1):
        for _ in range(n):
            piece = next(filler, None)
            if piece is not None:
                piece()

    for k in range(LRU_W // PIECE_W):
        proj_x(k)
    for k in range(LRU_W // PIECE_W):
        proj_y(k)
        for i in range(k * heads_per_piece, (k + 1) * heads_per_piece):
            conv(i)
            fill()
            gate_dots(i)
    for i in range(LRU_HEADS):
        gates(i)
        fill()
        scan(i)
        fill()
    for piece in filler:
        piece()


def _mixin(u3, w_qkv, w_xyg, conv_w, conv_b, wx, bx, wa, ba, ap):
    bsz, seq, _ = u3.shape
    n_slab = LRU_W // LANES
    gw = 3 * GROUP_W

    def row(w):
        return pl.BlockSpec((1, TM, w), lambda b, t: (b, t, 0))

    out_specs, out_shape = [], []
    for d in DILATIONS:
        out_specs.append(pl.BlockSpec((1, d, TM // d, gw), lambda b, t: (b, 0, t, 0)))
        out_shape.append(jax.ShapeDtypeStruct((bsz, d, seq // d, gw), BF16))
    out_specs += [row(LRU_W), row(2 * D_MODEL)]
    out_shape += [jax.ShapeDtypeStruct((bsz, seq, LRU_W), BF16),
                  jax.ShapeDtypeStruct((bsz, seq, 2 * D_MODEL), BF16)]
    return pl.pallas_call(
        _mixin_kernel,
        name="mixin",
        grid=(bsz, seq // TM),
        in_specs=[row(D_MODEL),
                  _const_spec((D_MODEL, N_GROUPS * gw)), _const_spec((D_MODEL, 4 * LRU_W)),
                  _const_spec((CONV_W, LRU_W)), _const_spec((1, LRU_W)),
                  _const_spec((LRU_HEADS, LRU_HD, LRU_HD)), _const_spec((1, LRU_W)),
                  _const_spec((LRU_HEADS, LRU_HD, LRU_HD)), _const_spec((1, LRU_W)),
                  _const_spec((1, LRU_W))],
        out_specs=out_specs,
        out_shape=out_shape,
        scratch_shapes=[pltpu.VMEM((2 * PIECE_W // LANES, TM, LANES), F32),
                        pltpu.VMEM((TM + SUBLANES, LRU_W), F32),
                        pltpu.VMEM((n_slab, SUBLANES * PITCH, LANES), F32),
                        pltpu.VMEM((n_slab, SUBLANES * PITCH, LANES), F32),
                        pltpu.VMEM((TM, LRU_W), F32),
                        pltpu.VMEM((SUBLANES, LRU_W), F32),
                        pltpu.VMEM((n_slab, SUBLANES, LANES), F32)],
        compiler_params=pltpu.CompilerParams(
            dimension_semantics=("arbitrary", "arbitrary"), vmem_limit_bytes=VMEM_LIMIT),
    )(u3, w_qkv, w_xyg, conv_w, conv_b, wx, bx, wa, ba, ap)


def _bucket_table():
    qi = np.arange(BAND)[:, None]
    kj = np.arange(BAND)[None, :]
    max_exact = REL_BUCKETS // 2
    out = np.zeros((N_GROUPS, 2, BAND, BAND), np.int32)
    for g, d in enumerate(DILATIONS):
        for half in range(2):
            steps = qi + BAND - kj if half == 0 else qi - kj
            valid = (steps >= 0) & (steps <= BAND)
            dist = np.maximum(steps, 0) * d
            nf = np.maximum(dist, 1).astype(np.float32)
            large = max_exact + (np.log(nf / np.float32(max_exact))
                                 / np.float32(math.log(REL_MAX_DISTANCE / max_exact))
                                 * np.float32(REL_BUCKETS - max_exact)).astype(np.int32)
            large = np.minimum(large, REL_BUCKETS - 1)
            bucket = np.where(dist < max_exact, dist, large)
            out[g, half] = np.where(valid, bucket, -1)
    return out


def _attn_kernel(group, pp, rows, tab_ref, bkt_ref, q_ref, k_ref, v_ref, o_ref, st_ref,
                 kbuf, vbuf, bias_s):
    first = (pl.program_id(0) == 0) & (pl.program_id(1) == 0) & (pl.program_id(2) == 0)
    c = pl.program_id(2)
    log2e = math.log2(math.e)
    vw = 2 * HEAD_DIM

    @pl.when(first)
    def _():
        for h in range(HEADS_PER_GROUP):
            for half in range(2):
                bk = bkt_ref[half]
                bias = jnp.full((BAND, BAND), -jnp.inf, F32)
                for n in range(REL_BUCKETS):
                    bias = jnp.where(bk == n, tab_ref[n, group * HEADS_PER_GROUP + h] * log2e, bias)
                bias_s[h, :, half * BAND:(half + 1) * BAND] = bias
            for ph in range(pp):
                vbuf[ph, :, h * vw + HEAD_DIM:(h + 1) * vw] = jnp.ones((BAND + rows, HEAD_DIM), BF16)

    @pl.when(c == 0)
    def _():
        for ph in range(pp):
            kbuf[ph, 0:BAND, :] = jnp.zeros((BAND, GROUP_W), BF16)
            for h in range(HEADS_PER_GROUP):
                vbuf[ph, 0:BAND, h * vw:h * vw + HEAD_DIM] = jnp.zeros((BAND, HEAD_DIM), BF16)

    for ph in range(pp):
        kbuf[ph, BAND:BAND + rows, :] = k_ref[ph]
        for h in range(HEADS_PER_GROUP):
            vbuf[ph, BAND:BAND + rows, h * vw:h * vw + HEAD_DIM] = v_ref[ph, :, h * HEAD_DIM:(h + 1) * HEAD_DIM]

    lane2 = lax.broadcasted_iota(jnp.int32, (1, 2 * BAND), 1)
    pen = jnp.where((lane2 < BAND) & (c == 0), -jnp.inf, 0.0).astype(F32)
    qk_scale = log2e / math.sqrt(HEAD_DIM)
    nt = (((1,), (1,)), ((), ()))
    units = [(ph, n, h) for ph in range(pp) for n in range(rows // BAND) for h in range(HEADS_PER_GROUP)]

    scores = []
    for ph, n, h in units:
        cols = slice(h * HEAD_DIM, (h + 1) * HEAD_DIM)
        q = q_ref[ph, n * BAND:(n + 1) * BAND, cols]
        kk = kbuf[ph, n * BAND:(n + 2) * BAND, cols]
        s = lax.dot_general(q, kk, nt, preferred_element_type=F32) * qk_scale + bias_s[h]
        if n == 0:
            s = s + pen
        scores.append(s)
    maxes = [jnp.max(s, axis=-1, keepdims=True) for s in scores]
    probs = [jnp.exp2(s - m).astype(BF16) for s, m in zip(scores, maxes)]
    outs = [jnp.dot(p, vbuf[ph, n * BAND:(n + 2) * BAND, h * vw:(h + 1) * vw], preferred_element_type=F32)
            for (ph, n, h), p in zip(units, probs)]
    lane = lax.broadcasted_iota(jnp.int32, (BAND, LANES), 1)
    stats = None
    for (ph, n, h), o, m in zip(units, outs, maxes):
        o_ref[ph, n * BAND:(n + 1) * BAND, h * HEAD_DIM:(h + 1) * HEAD_DIM] = o[:, :HEAD_DIM].astype(BF16)
        den = o[:, HEAD_DIM:]
        stats = jnp.broadcast_to(m, (BAND, LANES)) if h == 0 else jnp.where(lane == h, m, stats)
        stats = jnp.where(lane == HEADS_PER_GROUP + h, den, stats)
        if h == HEADS_PER_GROUP - 1:
            st_ref[ph, n * BAND:(n + 1) * BAND, :] = stats

    for ph in range(pp):
        kbuf[ph, 0:BAND, :] = kbuf[ph, rows:rows + BAND, :]
        vbuf[ph, 0:BAND, :] = vbuf[ph, rows:rows + BAND, :]


def _attn_group(group, qkv_g, table, bkt):
    bsz, d, sub, _ = qkv_g.shape
    rows = min(ATT_R, sub)
    pp = min(d, ATT_R // rows)
    assert sub % rows == 0 and d % pp == 0 and (pp == 1 or rows == sub)

    def spec(width, col):
        return pl.BlockSpec((None, pp, rows, width), lambda b, p, c: (b, p, c, col))

    return pl.pallas_call(
        functools.partial(_attn_kernel, group, pp, rows),
        name=f"attn{group}",
        grid=(bsz, d // pp, sub // rows),
        in_specs=[pl.BlockSpec(memory_space=pltpu.SMEM),
                  pl.BlockSpec((None, 2, BAND, BAND), lambda b, p, c: (group, 0, 0, 0)),
                  spec(GROUP_W, 0), spec(GROUP_W, 1), spec(GROUP_W, 2)],
        out_specs=[spec(GROUP_W, 0), spec(LANES, 0)],
        out_shape=[jax.ShapeDtypeStruct((bsz, d, sub, GROUP_W), BF16),
                   jax.ShapeDtypeStruct((bsz, d, sub, LANES), F32)],
        scratch_shapes=[pltpu.VMEM((pp, BAND + rows, GROUP_W), BF16),
                        pltpu.VMEM((pp, BAND + rows, 2 * GROUP_W), BF16),
                        pltpu.VMEM((HEADS_PER_GROUP, BAND, 2 * BAND), F32)],
        compiler_params=pltpu.CompilerParams(
            dimension_semantics=("arbitrary", "arbitrary", "arbitrary"),
            vmem_limit_bytes=VMEM_LIMIT),
    )(table, bkt, qkv_g, qkv_g, qkv_g)


def _merge_kernel(h_ref, o0_ref, o1_ref, o2_ref, l0_ref, l1_ref, l2_ref, rec_ref, gl_ref,
                  wat_ref, wrec_ref, wout_ref, gpost_ref, out_ref, osc, lsc):
    for gi, o_ref, l_ref in ((1, o1_ref, l1_ref), (2, o2_ref, l2_ref)):
        d = DILATIONS[gi]
        for p in range(d):
            idx = pl.ds(p, TM // d, stride=d)
            lsc[gi - 1, idx, :] = l_ref[p]
            for h in range(HEADS_PER_GROUP):
                osc[(gi - 1) * HEADS_PER_GROUP + h, idx, :] = (
                    o_ref[p, :, h * HEAD_DIM:(h + 1) * HEAD_DIM].astype(F32))
    st = (l0_ref[0], lsc[0], lsc[1])
    mx = jnp.maximum(jnp.maximum(st[0], st[1]), st[2])
    e = [jnp.exp2(x - mx) for x in st]
    den = [pltpu.roll(x, LANES - HEADS_PER_GROUP, 1) for x in st]
    inv = 1.0 / (e[0] * den[0] + e[1] * den[1] + e[2] * den[2])
    w0, w1, w2 = e[0] * inv, e[1] * inv, e[2] * inv
    parts = []
    for h in range(HEADS_PER_GROUP):
        cols = slice(h * HEAD_DIM, (h + 1) * HEAD_DIM)
        parts.append(w0[:, h:h + 1] * o0_ref[0, :, cols].astype(F32)
                     + w1[:, h:h + 1] * osc[h]
                     + w2[:, h:h + 1] * osc[HEADS_PER_GROUP + h])
    attn = jnp.concatenate(parts, axis=-1).astype(BF16)
    attn_d = jnp.dot(attn, wat_ref[...], preferred_element_type=F32)
    rec_d = jnp.dot(rec_ref[...], wrec_ref[...], preferred_element_type=F32)
    gl = gl_ref[...].astype(F32)
    merged = _sigmoid(gl[:, :D_MODEL]) * attn_d + _sigmoid(gl[:, D_MODEL:]) * rec_d
    mo = jnp.dot(merged.astype(BF16), wout_ref[...], preferred_element_type=F32)
    out_ref[...] = h_ref[...] + _rms(mo, gpost_ref[...])


def _merge(h3, o_list, lse_list, rec3, gl3, w_attn, w_rec, w_out, g_post):
    bsz, seq, _ = h3.shape

    def row(w):
        return pl.BlockSpec((None, TM, w), lambda b, i: (b, i, 0))

    def phased(d, w):
        return pl.BlockSpec((None, d, TM // d, w), lambda b, i: (b, 0, i, 0))

    return pl.pallas_call(
        _merge_kernel,
        name="merge",
        grid=(bsz, seq // TM),
        in_specs=[row(D_MODEL)] + [phased(d, GROUP_W) for d in DILATIONS]
                 + [phased(d, LANES) for d in DILATIONS] + [row(LRU_W), row(2 * D_MODEL),
                 _const_spec((GROUP_W, D_MODEL)), _const_spec((LRU_W, D_MODEL)),
                 _const_spec((D_MODEL, D_MODEL)), _const_spec((1, D_MODEL))],
        out_specs=row(D_MODEL),
        out_shape=jax.ShapeDtypeStruct((bsz, seq, D_MODEL), F32),
        scratch_shapes=[pltpu.VMEM((2 * HEADS_PER_GROUP, TM, LANES), F32),
                        pltpu.VMEM((2, TM, LANES), F32)],
        compiler_params=pltpu.CompilerParams(
            dimension_semantics=("arbitrary", "arbitrary"), vmem_limit_bytes=VMEM_LIMIT),
    )(h3, *o_list, *lse_list, rec3, gl3, w_attn, w_rec, w_out, g_post)


def kernel(x, ffn1_norm_pre, ffn1_norm_post, ffn1_w_gate, ffn1_w_up, ffn1_w_down, mix_norm_pre, mix_norm_post, w_in, rel_bias_table, conv_w, conv_b, lru_w_x, lru_b_x, lru_w_a, lru_b_a, lru_a_param, w_attn_branch, w_rec_branch, w_out, ffn2_norm_pre, ffn2_norm_post, ffn2_w_gate, ffn2_w_up, ffn2_w_down):
    bsz, seq, _ = x.shape
    n = bsz * seq
    depth = w_in.shape[0]
    bkt = jnp.asarray(_bucket_table())
    h = x.reshape(n, D_MODEL)
    for l in range(depth):
        h, u = _ffn(h, ffn1_norm_pre[l][None], ffn1_norm_post[l][None],
                    ffn1_w_gate[l].astype(BF16), ffn1_w_up[l].astype(BF16), ffn1_w_down[l].astype(BF16),
                    g_next=mix_norm_pre[l][None])
        h3 = h.reshape(bsz, seq, D_MODEL)

        w_qkv = (w_in[l][:, :3 * QKV_W].reshape(D_MODEL, 3, N_GROUPS, GROUP_W)
                 .transpose(0, 2, 1, 3).reshape(D_MODEL, 3 * QKV_W).astype(BF16))
        *qkv_groups, rec, gl = _mixin(u.reshape(bsz, seq, D_MODEL), w_qkv, w_in[l][:, 3 * QKV_W:].astype(BF16),
                                      conv_w[l], conv_b[l][None],
                                      lru_w_x[l].astype(BF16), lru_b_x[l].reshape(1, LRU_W),
                                      lru_w_a[l].astype(BF16), lru_b_a[l].reshape(1, LRU_W),
                                      lru_a_param[l][None])
        o_list, lse_list = [], []
        for g in range(N_GROUPS):
            o_g, lse_g = _attn_group(g, qkv_groups[g], rel_bias_table, bkt)
            o_list.append(o_g)
            lse_list.append(lse_g)
        h3 = _merge(h3, o_list, lse_list, rec, gl,
                    w_attn_branch[l].astype(BF16), w_rec_branch[l].astype(BF16),
                    w_out[l].astype(BF16), mix_norm_post[l][None])

        h = _ffn(h3.reshape(n, D_MODEL), ffn2_norm_pre[l][None], ffn2_norm_post[l][None],
                 ffn2_w_gate[l].astype(BF16), ffn2_w_up[l].astype(BF16), ffn2_w_down[l].astype(BF16))
    return h.reshape(bsz, seq, D_MODEL)
```

```python
import functools
import math
from typing import NamedTuple

import numpy as np
import jax
import jax.numpy as jnp
from jax import lax
from jax.experimental import pallas as pl
from jax.experimental.pallas import tpu as pltpu

F32 = jnp.float32
BF16 = jnp.bfloat16

D_MODEL = 1024
HEAD_DIM = 128
HEADS_PER_GROUP = 4
DILATIONS = (1, 4, 16)
BAND = 128
N_GROUPS = 3
GROUP_W = HEADS_PER_GROUP * HEAD_DIM
QKV_W = N_GROUPS * GROUP_W
LRU_W = D_MODEL
LRU_HEADS = 4
LRU_HD = LRU_W // LRU_HEADS
CONV_W = 4
LRU_C = 8.0
D_FF = 2816
REL_BUCKETS = 32
REL_MAX_DISTANCE = 2048
EPS = 1e-6
TINY = 1e-30

LANES = 128
SUBLANES = 8
BF16_TILE_ROWS = 16
MAX_CAST_BLOCKS = 64
VMEM_LIMIT = 56 * 1024 * 1024

TM_FFN = 512
TM = 512
ATT_R = 1024
SEG = TM // SUBLANES
PITCH = SEG + SUBLANES
PIECE_W = 256


def _rms(x, g):
    ms = jnp.mean(x * x, axis=-1, keepdims=True)
    return x * lax.rsqrt(ms + EPS) * g


def _sigmoid(x):
    return 1.0 / (1.0 + jnp.exp(-x))


def _gelu_tanh(x):
    c = math.sqrt(2.0 / math.pi)
    inner = x * (c + (c * 0.044715) * (x * x))
    return (0.5 * x) * (1.0 + jnp.tanh(inner))


def _const_spec(shape):
    nd = len(shape)
    return pl.BlockSpec(shape, lambda *_: (0,) * nd, pipeline_mode=pl.Buffered(1))


class _CastJob(NamedTuple):
    src: jax.Array
    outs: tuple

    @property
    def block_rows(self):
        rows = self.src.shape[0]
        br = BF16_TILE_ROWS
        while rows % br or rows // br > MAX_CAST_BLOCKS:
            br += BF16_TILE_ROWS
        return br


def _plain_cast(w):
    return _CastJob(w, ((w.shape[1], ((0, 0, w.shape[1]),)),))


def _ffn_kernel(emit_normed, jobs_pieces, x_ref, gpre_ref, gpost_ref, gnext_ref, wg_ref, wu_ref, wd_ref, *rest):
    n_jobs = len(jobs_pieces)
    cast_in, o_ref = rest[:n_jobs], rest[n_jobs]
    normed_refs, cast_out = rest[n_jobs + 1:n_jobs + 1 + emit_normed], rest[n_jobs + 1 + emit_normed:]
    x = x_ref[...]
    xn = _rms(x, gpre_ref[...]).astype(BF16)
    g = jnp.dot(xn, wg_ref[...], preferred_element_type=F32)
    u = jnp.dot(xn, wu_ref[...], preferred_element_type=F32)
    a = (g * _sigmoid(g) * u).astype(BF16)
    f = jnp.dot(a, wd_ref[...], preferred_element_type=F32)
    h = x + 0.5 * _rms(f, gpost_ref[...])
    o_ref[...] = h
    for normed_ref in normed_refs:
        normed_ref[...] = _rms(h, gnext_ref[...]).astype(BF16)
    outs = iter(cast_out)
    for src_ref, pieces_per_out in zip(cast_in, jobs_pieces):
        for pieces in pieces_per_out:
            dst_ref = next(outs)
            for dst_col, src_col, width in pieces:
                dst_ref[:, dst_col:dst_col + width] = src_ref[:, src_col:src_col + width].astype(BF16)


def _ffn(x2, g_pre, g_post, wg, wu, wd, g_next=None, cast_jobs=()):
    n = x2.shape[0]
    steps = n // TM_FFN
    assert steps <= MAX_CAST_BLOCKS
    emit_normed = g_next is not None
    row = pl.BlockSpec((TM_FFN, D_MODEL), lambda i: (i, 0))
    in_specs = [row, _const_spec((1, D_MODEL)), _const_spec((1, D_MODEL)), _const_spec((1, D_MODEL)),
                _const_spec((D_MODEL, D_FF)), _const_spec((D_MODEL, D_FF)), _const_spec((D_FF, D_MODEL))]
    out_specs = [row] + [row] * emit_normed
    out_shape = [jax.ShapeDtypeStruct((n, D_MODEL), F32)] + [jax.ShapeDtypeStruct((n, D_MODEL), BF16)] * emit_normed
    for job in cast_jobs:
        rows, br = job.src.shape[0], job.block_rows
        index_map = functools.partial(lambda i, last: (jnp.minimum(i, last), 0), last=rows // br - 1)
        in_specs.append(pl.BlockSpec((br, job.src.shape[1]), index_map))
        for cols, _ in job.outs:
            out_specs.append(pl.BlockSpec((br, cols), index_map))
            out_shape.append(jax.ShapeDtypeStruct((rows, cols), BF16))
    jobs_pieces = tuple(tuple(pieces for _, pieces in job.outs) for job in cast_jobs)
    return pl.pallas_call(
        functools.partial(_ffn_kernel, emit_normed, jobs_pieces),
        name="ffn",
        grid=(steps,),
        in_specs=in_specs,
        out_specs=out_specs,
        out_shape=out_shape,
        compiler_params=pltpu.CompilerParams(
            dimension_semantics=("arbitrary",), vmem_limit_bytes=VMEM_LIMIT),
    )(x2, g_pre, g_post, g_next if emit_normed else g_post, wg, wu, wd, *[job.src for job in cast_jobs])


def _mixin_kernel(u_ref, wqkv_ref, wlru_ref, cw_ref, cb_ref, wx_ref, bx_ref, wa_ref, ba_ref, ap_ref,
                  o0_ref, o1_ref, o2_ref, rec_ref, gl_ref,
                  sc, xbuf, a_s, b_s, gy_s, hcar, cin_s):
    @pl.when(pl.program_id(1) == 0)
    def _():
        xbuf[0:SUBLANES, :] = jnp.zeros((SUBLANES, LRU_W), F32)
        hcar[...] = jnp.zeros(hcar.shape, F32)

    gw = 3 * GROUP_W
    pslabs = PIECE_W // LANES
    slabs_per_head = LRU_HD // LANES
    ap = -ap_ref[...]
    c_softplus = LRU_C * (jnp.maximum(ap, 0.0) + jnp.log1p(jnp.exp(-jnp.abs(ap))))
    state = {}

    def proj_x(k):
        cols = slice(k * PIECE_W, (k + 1) * PIECE_W)
        xbuf[SUBLANES:SUBLANES + TM, cols] = jnp.dot(u_ref[0],wlru_ref[:, cols], preferred_element_type=F32)

    def proj_y(k):
        cols = slice(k * PIECE_W, (k + 1) * PIECE_W)
        yr = jnp.dot(u_ref[0],wlru_ref[:, LRU_W + k * PIECE_W:LRU_W + (k + 1) * PIECE_W], preferred_element_type=F32)
        gy_s[:, cols] = _gelu_tanh(yr)

    def proj_gl(k):
        cols = slice(k * PIECE_W, (k + 1) * PIECE_W)
        gl_ref[0, :, cols] = jnp.dot(u_ref[0],wlru_ref[:, 2 * LRU_W + k * PIECE_W:2 * LRU_W + (k + 1) * PIECE_W],
                                     preferred_element_type=F32).astype(BF16)

    def proj_qkv(gi, k):
        cols = slice(k * PIECE_W, (k + 1) * PIECE_W)
        res = jnp.dot(u_ref[0],wqkv_ref[:, gi * gw + k * PIECE_W:gi * gw + (k + 1) * PIECE_W],
                      preferred_element_type=F32)
        if gi == 0:
            o0_ref[0, 0, :, cols] = res.astype(BF16)
            return
        o_ref = (o1_ref, o2_ref)[gi - 1]
        d = DILATIONS[gi]
        base = (((gi - 1) * (gw // PIECE_W) + k) % 2) * pslabs
        for c in range(pslabs):
            sc[base + c] = res[:, c * LANES:(c + 1) * LANES]
        for p in range(d):
            for c in range(pslabs):
                o_ref[0, p, :, k * PIECE_W + c * LANES:k * PIECE_W + (c + 1) * LANES] = (
                    sc[base + c, pl.ds(p, TM // d, stride=d), :].astype(BF16))

    def gate_dots(i):
        xh = state["xcb", i]
        state["pre_x", i] = jnp.dot(xh, wx_ref[i], preferred_element_type=F32)
        state["pre_a", i] = jnp.dot(xh, wa_ref[i], preferred_element_type=F32)

    def conv(i):
        cols = slice(i * LRU_HD, (i + 1) * LRU_HD)
        xc = cb_ref[:, cols] + cw_ref[0:1, cols] * xbuf[SUBLANES - 3:SUBLANES - 3 + TM, cols]
        for j in range(1, CONV_W):
            off = SUBLANES - (CONV_W - 1) + j
            xc = xc + cw_ref[j:j + 1, cols] * xbuf[off:off + TM, cols]
        xbuf[0:SUBLANES, cols] = xbuf[TM:TM + SUBLANES, cols]
        state["xc", i] = xc
        state["xcb", i] = xc.astype(BF16)

    def gates(i):
        cols = slice(i * LRU_HD, (i + 1) * LRU_HD)
        gx = _sigmoid(state["pre_x", i] + bx_ref[:, cols])
        ga = _sigmoid(state["pre_a", i] + ba_ref[:, cols])
        nla = ga * c_softplus[:, cols]
        a = jnp.exp(-nla)
        t = jnp.tanh(nla)
        w = (t + t) / (1.0 + t)
        b = (w * lax.rsqrt(jnp.maximum(w, TINY))) * (gx * state["xc", i])
        for k in range(slabs_per_head):
            c = i * slabs_per_head + k
            for s in range(SUBLANES):
                a_s[c, s * PITCH:s * PITCH + SEG, :] = a[s * SEG:(s + 1) * SEG, k * LANES:(k + 1) * LANES]
                b_s[c, s * PITCH:s * PITCH + SEG, :] = b[s * SEG:(s + 1) * SEG, k * LANES:(k + 1) * LANES]

    def scan(i):
        slab_ids = range(i * slabs_per_head, (i + 1) * slabs_per_head)
        hs = {c: jnp.zeros((SUBLANES, LANES), F32) for c in slab_ids}
        cum = {c: jnp.ones((SUBLANES, LANES), F32) for c in slab_ids}
        for j in range(SEG):
            idx = pl.ds(j, SUBLANES, stride=PITCH)
            for c in slab_ids:
                aj = a_s[c, idx, :]
                hs[c] = aj * hs[c] + b_s[c, idx, :]
                cum[c] = aj * cum[c]
                a_s[c, idx, :] = cum[c]
                b_s[c, idx, :] = hs[c]
        row = lax.broadcasted_iota(jnp.int32, (SUBLANES, LANES), 0)
        for c in slab_ids:
            cols = slice(c * LANES, (c + 1) * LANES)
            h_in = jnp.broadcast_to(hcar[SUBLANES - 1:SUBLANES, cols], (SUBLANES, LANES))
            cin = h_in
            for _ in range(SUBLANES - 1):
                out = hs[c] + cum[c] * cin
                cin = jnp.where(row == 0, h_in, pltpu.roll(out, 1, 0))
            cin_s[c] = cin
            hcar[:, cols] = hs[c] + cum[c] * cin
        for c in slab_ids:
            cols = slice(c * LANES, (c + 1) * LANES)
            for s in range(SUBLANES):
                rows = slice(s * SEG, (s + 1) * SEG)
                prow = slice(s * PITCH, s * PITCH + SEG)
                h = b_s[c, prow, :] + a_s[c, prow, :] * cin_s[c, s:s + 1, :]
                rec_ref[0, rows, cols] = (h * gy_s[rows, cols]).astype(BF16)

    heads_per_piece = PIECE_W // LRU_HD
    filler = ([functools.partial(proj_qkv, gi, k) for gi in range(N_GROUPS) for k in range(gw // PIECE_W)]
              + [functools.partial(proj_gl, k) for k in range(2 * D_MODEL // PIECE_W)])
    filler = iter(filler)

    def fill(n=1):
        for _ in range(n):
            piece = next(filler, None)
            if piece is not None:
                piece()

    for k in range(LRU_W // PIECE_W):
        proj_x(k)
    for k in range(LRU_W // PIECE_W):
        proj_y(k)
        for i in range(k * heads_per_piece, (k + 1) * heads_per_piece):
            conv(i)
            fill()
            gate_dots(i)
    for i in range(LRU_HEADS):
        gates(i)
        fill()
        scan(i)
        fill()
    for piece in filler:
        piece()


def _mixin(u3, w_qkv, w_xyg, conv_w, conv_b, wx, bx, wa, ba, ap):
    bsz, seq, _ = u3.shape
    n_slab = LRU_W // LANES
    gw = 3 * GROUP_W

    def row(w):
        return pl.BlockSpec((1, TM, w), lambda b, t: (b, t, 0))

    out_specs, out_shape = [], []
    for d in DILATIONS:
        out_specs.append(pl.BlockSpec((1, d, TM // d, gw), lambda b, t: (b, 0, t, 0)))
        out_shape.append(jax.ShapeDtypeStruct((bsz, d, seq // d, gw), BF16))
    out_specs += [row(LRU_W), row(2 * D_MODEL)]
    out_shape += [jax.ShapeDtypeStruct((bsz, seq, LRU_W), BF16),
                  jax.ShapeDtypeStruct((bsz, seq, 2 * D_MODEL), BF16)]
    return pl.pallas_call(
        _mixin_kernel,
        name="mixin",
        grid=(bsz, seq // TM),
        in_specs=[row(D_MODEL),
                  _const_spec((D_MODEL, N_GROUPS * gw)), _const_spec((D_MODEL, 4 * LRU_W)),
                  _const_spec((CONV_W, LRU_W)), _const_spec((1, LRU_W)),
                  _const_spec((LRU_HEADS, LRU_HD, LRU_HD)), _const_spec((1, LRU_W)),
                  _const_spec((LRU_HEADS, LRU_HD, LRU_HD)), _const_spec((1, LRU_W)),
                  _const_spec((1, LRU_W))],
        out_specs=out_specs,
        out_shape=out_shape,
        scratch_shapes=[pltpu.VMEM((2 * PIECE_W // LANES, TM, LANES), F32),
                        pltpu.VMEM((TM + SUBLANES, LRU_W), F32),
                        pltpu.VMEM((n_slab, SUBLANES * PITCH, LANES), F32),
                        pltpu.VMEM((n_slab, SUBLANES * PITCH, LANES), F32),
                        pltpu.VMEM((TM, LRU_W), F32),
                        pltpu.VMEM((SUBLANES, LRU_W), F32),
                        pltpu.VMEM((n_slab, SUBLANES, LANES), F32)],
        compiler_params=pltpu.CompilerParams(
            dimension_semantics=("arbitrary", "arbitrary"), vmem_limit_bytes=VMEM_LIMIT),
    )(u3, w_qkv, w_xyg, conv_w, conv_b, wx, bx, wa, ba, ap)


def _bucket_table():
    qi = np.arange(BAND)[:, None]
    kj = np.arange(BAND)[None, :]
    max_exact = REL_BUCKETS // 2
    out = np.zeros((N_GROUPS, 2, BAND, BAND), np.int32)
    for g, d in enumerate(DILATIONS):
        for half in range(2):
            steps = qi + BAND - kj if half == 0 else qi - kj
            valid = (steps >= 0) & (steps <= BAND)
            dist = np.maximum(steps, 0) * d
            nf = np.maximum(dist, 1).astype(np.float32)
            large = max_exact + (np.log(nf / np.float32(max_exact))
                                 / np.float32(math.log(REL_MAX_DISTANCE / max_exact))
                                 * np.float32(REL_BUCKETS - max_exact)).astype(np.int32)
            large = np.minimum(large, REL_BUCKETS - 1)
            bucket = np.where(dist < max_exact, dist, large)
            out[g, half] = np.where(valid, bucket, -1)
    return out


def _attn_kernel(group, pp, rows, tab_ref, bkt_ref, q_ref, k_ref, v_ref, o_ref, st_ref,
                 kbuf, vbuf, bias_s):
    first = (pl.program_id(0) == 0) & (pl.program_id(1) == 0) & (pl.program_id(2) == 0)
    c = pl.program_id(2)
    log2e = math.log2(math.e)
    vw = 2 * HEAD_DIM

    @pl.when(first)
    def _():
        for h in range(HEADS_PER_GROUP):
            for half in range(2):
                bk = bkt_ref[half]
                bias = jnp.full((BAND, BAND), -jnp.inf, F32)
                for n in range(REL_BUCKETS):
                    bias = jnp.where(bk == n, tab_ref[n, group * HEADS_PER_GROUP + h] * log2e, bias)
                bias_s[h, :, half * BAND:(half + 1) * BAND] = bias
            for ph in range(pp):
                vbuf[ph, :, h * vw + HEAD_DIM:(h + 1) * vw] = jnp.ones((BAND + rows, HEAD_DIM), BF16)

    @pl.when(c == 0)
    def _():
        for ph in range(pp):
            kbuf[ph, 0:BAND, :] = jnp.zeros((BAND, GROUP_W), BF16)
            for h in range(HEADS_PER_GROUP):
                vbuf[ph, 0:BAND, h * vw:h * vw + HEAD_DIM] = jnp.zeros((BAND, HEAD_DIM), BF16)

    for ph in range(pp):
        kbuf[ph, BAND:BAND + rows, :] = k_ref[ph]
        for h in range(HEADS_PER_GROUP):
            vbuf[ph, BAND:BAND + rows, h * vw:h * vw + HEAD_DIM] = v_ref[ph, :, h * HEAD_DIM:(h + 1) * HEAD_DIM]

    lane2 = lax.broadcasted_iota(jnp.int32, (1, 2 * BAND), 1)
    pen = jnp.where((lane2 < BAND) & (c == 0), -jnp.inf, 0.0).astype(F32)
    qk_scale = log2e / math.sqrt(HEAD_DIM)
    nt = (((1,), (1,)), ((), ()))
    units = [(ph, n, h) for ph in range(pp) for n in range(rows // BAND) for h in range(HEADS_PER_GROUP)]

    scores = []
    for ph, n, h in units:
        cols = slice(h * HEAD_DIM, (h + 1) * HEAD_DIM)
        q = q_ref[ph, n * BAND:(n + 1) * BAND, cols]
        kk = kbuf[ph, n * BAND:(n + 2) * BAND, cols]
        s = lax.dot_general(q, kk, nt, preferred_element_type=F32) * qk_scale + bias_s[h]
        if n == 0:
            s = s + pen
        scores.append(s)
    maxes = [jnp.max(s, axis=-1, keepdims=True) for s in scores]
    probs = [jnp.exp2(s - m).astype(BF16) for s, m in zip(scores, maxes)]
    outs = [jnp.dot(p, vbuf[ph, n * BAND:(n + 2) * BAND, h * vw:(h + 1) * vw], preferred_element_type=F32)
            for (ph, n, h), p in zip(units, probs)]
    lane = lax.broadcasted_iota(jnp.int32, (BAND, LANES), 1)
    stats = None
    for (ph, n, h), o, m in zip(units, outs, maxes):
        o_ref[ph, n * BAND:(n + 1) * BAND, h * HEAD_DIM:(h + 1) * HEAD_DIM] = o[:, :HEAD_DIM].astype(BF16)
        den = o[:, HEAD_DIM:]
        stats = jnp.broadcast_to(m, (BAND, LANES)) if h == 0 else jnp.where(lane == h, m, stats)
        stats = jnp.where(lane == HEADS_PER_GROUP + h, den, stats)
        if h == HEADS_PER_GROUP - 1:
            st_ref[ph, n * BAND:(n + 1) * BAND, :] = stats

    for ph in range(pp):
        kbuf[ph, 0:BAND, :] = kbuf[ph, rows:rows + BAND, :]
        vbuf[ph, 0:BAND, :] = vbuf[ph, rows:rows + BAND, :]


def _attn_group(group, qkv_g, table, bkt):
    bsz, d, sub, _ = qkv_g.shape
    rows = min(ATT_R, sub)
    pp = min(d, ATT_R // rows)
    assert sub % rows == 0 and d % pp == 0 and (pp == 1 or rows == sub)

    def spec(width, col):
        return pl.BlockSpec((None, pp, rows, width), lambda b, p, c: (b, p, c, col))

    return pl.pallas_call(
        functools.partial(_attn_kernel, group, pp, rows),
        name=f"attn{group}",
        grid=(bsz, d // pp, sub // rows),
        in_specs=[pl.BlockSpec(memory_space=pltpu.SMEM),
                  pl.BlockSpec((None, 2, BAND, BAND), lambda b, p, c: (group, 0, 0, 0)),
                  spec(GROUP_W, 0), spec(GROUP_W, 1), spec(GROUP_W, 2)],
        out_specs=[spec(GROUP_W, 0), spec(LANES, 0)],
        out_shape=[jax.ShapeDtypeStruct((bsz, d, sub, GROUP_W), BF16),
                   jax.ShapeDtypeStruct((bsz, d, sub, LANES), F32)],
        scratch_shapes=[pltpu.VMEM((pp, BAND + rows, GROUP_W), BF16),
                        pltpu.VMEM((pp, BAND + rows, 2 * GROUP_W), BF16),
                        pltpu.VMEM((HEADS_PER_GROUP, BAND, 2 * BAND), F32)],
        compiler_params=pltpu.CompilerParams(
            dimension_semantics=("arbitrary", "arbitrary", "arbitrary"),
            vmem_limit_bytes=VMEM_LIMIT),
    )(table, bkt, qkv_g, qkv_g, qkv_g)


def _merge_kernel(h_ref, o0_ref, o1_ref, o2_ref, l0_ref, l1_ref, l2_ref, rec_ref, gl_ref,
                  wat_ref, wrec_ref, wout_ref, gpost_ref, out_ref, osc, lsc):
    rec_d = jnp.dot(rec_ref[...], wrec_ref[...], preferred_element_type=F32)
    for gi, o_ref, l_ref in ((1, o1_ref, l1_ref), (2, o2_ref, l2_ref)):
        d = DILATIONS[gi]
        for p in range(d):
            idx = pl.ds(p, TM // d, stride=d)
            lsc[gi - 1, idx, :] = l_ref[p]
            for h in range(HEADS_PER_GROUP):
                osc[(gi - 1) * HEADS_PER_GROUP + h, idx, :] = (
                    o_ref[p, :, h * HEAD_DIM:(h + 1) * HEAD_DIM].astype(F32))
    st = (l0_ref[0], lsc[0], lsc[1])
    mx = jnp.maximum(jnp.maximum(st[0], st[1]), st[2])
    e = [jnp.exp2(x - mx) for x in st]
    den = [pltpu.roll(x, LANES - HEADS_PER_GROUP, 1) for x in st]
    inv = 1.0 / (e[0] * den[0] + e[1] * den[1] + e[2] * den[2])
    w0, w1, w2 = e[0] * inv, e[1] * inv, e[2] * inv
    parts = []
    for h in range(HEADS_PER_GROUP):
        cols = slice(h * HEAD_DIM, (h + 1) * HEAD_DIM)
        parts.append(w0[:, h:h + 1] * o0_ref[0, :, cols].astype(F32)
                     + w1[:, h:h + 1] * osc[h]
                     + w2[:, h:h + 1] * osc[HEADS_PER_GROUP + h])
    attn = jnp.concatenate(parts, axis=-1).astype(BF16)
    attn_d = jnp.dot(attn, wat_ref[...], preferred_element_type=F32)
    gl = gl_ref[...].astype(F32)
    merged = _sigmoid(gl[:, :D_MODEL]) * attn_d + _sigmoid(gl[:, D_MODEL:]) * rec_d
    mo = jnp.dot(merged.astype(BF16), wout_ref[...], preferred_element_type=F32)
    out_ref[...] = h_ref[...] + _rms(mo, gpost_ref[...])


def _merge(h3, o_list, lse_list, rec3, gl3, w_attn, w_rec, w_out, g_post):
    bsz, seq, _ = h3.shape

    def row(w):
        return pl.BlockSpec((None, TM, w), lambda b, i: (b, i, 0))

    def phased(d, w):
        return pl.BlockSpec((None, d, TM // d, w), lambda b, i: (b, 0, i, 0))

    return pl.pallas_call(
        _merge_kernel,
        name="merge",
        grid=(bsz, seq // TM),
        in_specs=[row(D_MODEL)] + [phased(d, GROUP_W) for d in DILATIONS]
                 + [phased(d, LANES) for d in DILATIONS] + [row(LRU_W), row(2 * D_MODEL),
                 _const_spec((GROUP_W, D_MODEL)), _const_spec((LRU_W, D_MODEL)),
                 _const_spec((D_MODEL, D_MODEL)), _const_spec((1, D_MODEL))],
        out_specs=row(D_MODEL),
        out_shape=jax.ShapeDtypeStruct((bsz, seq, D_MODEL), F32),
        scratch_shapes=[pltpu.VMEM((2 * HEADS_PER_GROUP, TM, LANES), F32),
                        pltpu.VMEM((2, TM, LANES), F32)],
        compiler_params=pltpu.CompilerParams(
            dimension_semantics=("arbitrary", "arbitrary"), vmem_limit_bytes=VMEM_LIMIT),
    )(h3, *o_list, *lse_list, rec3, gl3, w_attn, w_rec, w_out, g_post)


def kernel(x, ffn1_norm_pre, ffn1_norm_post, ffn1_w_gate, ffn1_w_up, ffn1_w_down, mix_norm_pre, mix_norm_post, w_in, rel_bias_table, conv_w, conv_b, lru_w_x, lru_b_x, lru_w_a, lru_b_a, lru_a_param, w_attn_branch, w_rec_branch, w_out, ffn2_norm_pre, ffn2_norm_post, ffn2_w_gate, ffn2_w_up, ffn2_w_down):
    bsz, seq, _ = x.shape
    n = bsz * seq
    depth = w_in.shape[0]
    bkt = jnp.asarray(_bucket_table())
    h = x.reshape(n, D_MODEL)
    for l in range(depth):
        gw = 3 * GROUP_W
        regroup = tuple((g * gw + j * GROUP_W, j * QKV_W + g * GROUP_W, GROUP_W)
                        for g in range(N_GROUPS) for j in range(3))
        jobs = [_CastJob(w_in[l], ((3 * QKV_W, regroup), (4 * LRU_W, ((0, 3 * QKV_W, 4 * LRU_W),)))),
                _plain_cast(lru_w_x[l].reshape(LRU_W, LRU_HD)), _plain_cast(lru_w_a[l].reshape(LRU_W, LRU_HD)),
                _plain_cast(w_attn_branch[l]), _plain_cast(w_rec_branch[l]), _plain_cast(w_out[l]),
                _plain_cast(ffn2_w_gate[l]), _plain_cast(ffn2_w_up[l]), _plain_cast(ffn2_w_down[l])]
        (h, u, w_qkv, w_xyg, wx, wa, w_attn, w_rec, w_o, w2_gate, w2_up, w2_down) = _ffn(
            h, ffn1_norm_pre[l][None], ffn1_norm_post[l][None],
            ffn1_w_gate[l].astype(BF16), ffn1_w_up[l].astype(BF16), ffn1_w_down[l].astype(BF16),
            g_next=mix_norm_pre[l][None], cast_jobs=jobs)
        h3 = h.reshape(bsz, seq, D_MODEL)

        *qkv_groups, rec, gl = _mixin(u.reshape(bsz, seq, D_MODEL), w_qkv, w_xyg,
                                      conv_w[l], conv_b[l][None],
                                      wx.reshape(LRU_HEADS, LRU_HD, LRU_HD), lru_b_x[l].reshape(1, LRU_W),
                                      wa.reshape(LRU_HEADS, LRU_HD, LRU_HD), lru_b_a[l].reshape(1, LRU_W),
                                      lru_a_param[l][None])
        o_list, lse_list = [], []
        for g in range(N_GROUPS):
            o_g, lse_g = _attn_group(g, qkv_groups[g], rel_bias_table, bkt)
            o_list.append(o_g)
            lse_list.append(lse_g)
        h3 = _merge(h3, o_list, lse_list, rec, gl, w_attn, w_rec, w_o, mix_norm_post[l][None])

        h, = _ffn(h3.reshape(n, D_MODEL), ffn2_norm_pre[l][None], ffn2_norm_post[l][None],
                  w2_gate, w2_up, w2_down)
    return h.reshape(bsz, seq, D_MODEL)
```

```python
import functools
import math
from typing import NamedTuple

import numpy as np
import jax
import jax.numpy as jnp
from jax import lax
from jax.experimental import pallas as pl
from jax.experimental.pallas import tpu as pltpu

F32 = jnp.float32
BF16 = jnp.bfloat16

D_MODEL = 1024
HEAD_DIM = 128
HEADS_PER_GROUP = 4
DILATIONS = (1, 4, 16)
BAND = 128
N_GROUPS = 3
GROUP_W = HEADS_PER_GROUP * HEAD_DIM
QKV_W = N_GROUPS * GROUP_W
LRU_W = D_MODEL
LRU_HEADS = 4
LRU_HD = LRU_W // LRU_HEADS
CONV_W = 4
LRU_C = 8.0
D_FF = 2816
REL_BUCKETS = 32
REL_MAX_DISTANCE = 2048
EPS = 1e-6
TINY = 1e-30

LANES = 128
SUBLANES = 8
BF16_TILE_ROWS = 16
MAX_CAST_BLOCKS = 64
VMEM_LIMIT = 56 * 1024 * 1024

TM_FFN = 512
TM = 512
ATT_R = 2048
SEG = TM // SUBLANES
PITCH = SEG + SUBLANES
PIECE_W = 256


def _rms(x, g):
    ms = jnp.mean(x * x, axis=-1, keepdims=True)
    return x * lax.rsqrt(ms + EPS) * g


def _sigmoid(x):
    return 1.0 / (1.0 + jnp.exp(-x))


def _gelu_tanh(x):
    c = math.sqrt(2.0 / math.pi)
    inner = x * (c + (c * 0.044715) * (x * x))
    return (0.5 * x) * (1.0 + jnp.tanh(inner))


def _const_spec(shape):
    nd = len(shape)
    return pl.BlockSpec(shape, lambda *_: (0,) * nd, pipeline_mode=pl.Buffered(1))


class _CastJob(NamedTuple):
    src: jax.Array
    outs: tuple

    @property
    def block_rows(self):
        rows = self.src.shape[0]
        br = BF16_TILE_ROWS
        while rows % br or rows // br > MAX_CAST_BLOCKS:
            br += BF16_TILE_ROWS
        return br


def _plain_cast(w):
    return _CastJob(w, ((w.shape[1], ((0, 0, w.shape[1]),)),))


def _ffn_kernel(emit_normed, jobs_pieces, x_ref, gpre_ref, gpost_ref, gnext_ref, wg_ref, wu_ref, wd_ref, *rest):
    n_jobs = len(jobs_pieces)
    cast_in, o_ref = rest[:n_jobs], rest[n_jobs]
    normed_refs, cast_out = rest[n_jobs + 1:n_jobs + 1 + emit_normed], rest[n_jobs + 1 + emit_normed:]
    x = x_ref[...]
    xn = _rms(x, gpre_ref[...]).astype(BF16)
    g = jnp.dot(xn, wg_ref[...], preferred_element_type=F32)
    u = jnp.dot(xn, wu_ref[...], preferred_element_type=F32)
    a = (g * _sigmoid(g) * u).astype(BF16)
    f = jnp.dot(a, wd_ref[...], preferred_element_type=F32)
    h = x + 0.5 * _rms(f, gpost_ref[...])
    o_ref[...] = h
    for normed_ref in normed_refs:
        normed_ref[...] = _rms(h, gnext_ref[...]).astype(BF16)
    outs = iter(cast_out)
    for src_ref, pieces_per_out in zip(cast_in, jobs_pieces):
        for pieces in pieces_per_out:
            dst_ref = next(outs)
            for dst_col, src_col, width in pieces:
                dst_ref[:, dst_col:dst_col + width] = src_ref[:, src_col:src_col + width].astype(BF16)


def _ffn(x2, g_pre, g_post, wg, wu, wd, g_next=None, cast_jobs=()):
    n = x2.shape[0]
    steps = n // TM_FFN
    assert steps <= MAX_CAST_BLOCKS
    emit_normed = g_next is not None
    row = pl.BlockSpec((TM_FFN, D_MODEL), lambda i: (i, 0))
    in_specs = [row, _const_spec((1, D_MODEL)), _const_spec((1, D_MODEL)), _const_spec((1, D_MODEL)),
                _const_spec((D_MODEL, D_FF)), _const_spec((D_MODEL, D_FF)), _const_spec((D_FF, D_MODEL))]
    out_specs = [row] + [row] * emit_normed
    out_shape = [jax.ShapeDtypeStruct((n, D_MODEL), F32)] + [jax.ShapeDtypeStruct((n, D_MODEL), BF16)] * emit_normed
    for job in cast_jobs:
        rows, br = job.src.shape[0], job.block_rows
        index_map = functools.partial(lambda i, last: (jnp.minimum(i, last), 0), last=rows // br - 1)
        in_specs.append(pl.BlockSpec((br, job.src.shape[1]), index_map))
        for cols, _ in job.outs:
            out_specs.append(pl.BlockSpec((br, cols), index_map))
            out_shape.append(jax.ShapeDtypeStruct((rows, cols), BF16))
    jobs_pieces = tuple(tuple(pieces for _, pieces in job.outs) for job in cast_jobs)
    return pl.pallas_call(
        functools.partial(_ffn_kernel, emit_normed, jobs_pieces),
        name="ffn",
        grid=(steps,),
        in_specs=in_specs,
        out_specs=out_specs,
        out_shape=out_shape,
        compiler_params=pltpu.CompilerParams(
            dimension_semantics=("arbitrary",), vmem_limit_bytes=VMEM_LIMIT),
    )(x2, g_pre, g_post, g_next if emit_normed else g_post, wg, wu, wd, *[job.src for job in cast_jobs])


def _mixin_kernel(u_ref, wqkv_ref, wlru_ref, cw_ref, cb_ref, wx_ref, bx_ref, wa_ref, ba_ref, ap_ref,
                  o0_ref, o1_ref, o2_ref, rec_ref, gl_ref,
                  sc, xbuf, a_s, b_s, gy_s, hcar, cin_s):
    @pl.when(pl.program_id(1) == 0)
    def _():
        xbuf[0:SUBLANES, :] = jnp.zeros((SUBLANES, LRU_W), F32)
        hcar[...] = jnp.zeros(hcar.shape, F32)

    gw = 3 * GROUP_W
    pslabs = PIECE_W // LANES
    slabs_per_head = LRU_HD // LANES
    ap = -ap_ref[...]
    c_softplus = LRU_C * (jnp.maximum(ap, 0.0) + jnp.log1p(jnp.exp(-jnp.abs(ap))))
    state = {}

    def proj_x(k):
        cols = slice(k * PIECE_W, (k + 1) * PIECE_W)
        xbuf[SUBLANES:SUBLANES + TM, cols] = jnp.dot(u_ref[0],wlru_ref[:, cols], preferred_element_type=F32)

    def proj_y(k):
        cols = slice(k * PIECE_W, (k + 1) * PIECE_W)
        yr = jnp.dot(u_ref[0],wlru_ref[:, LRU_W + k * PIECE_W:LRU_W + (k + 1) * PIECE_W], preferred_element_type=F32)
        gy_s[:, cols] = _gelu_tanh(yr)

    def proj_gl(k):
        cols = slice(k * PIECE_W, (k + 1) * PIECE_W)
        gl_ref[0, :, cols] = jnp.dot(u_ref[0],wlru_ref[:, 2 * LRU_W + k * PIECE_W:2 * LRU_W + (k + 1) * PIECE_W],
                                     preferred_element_type=F32).astype(BF16)

    def proj_qkv(gi, k):
        cols = slice(k * PIECE_W, (k + 1) * PIECE_W)
        res = jnp.dot(u_ref[0],wqkv_ref[:, gi * gw + k * PIECE_W:gi * gw + (k + 1) * PIECE_W],
                      preferred_element_type=F32)
        if gi == 0:
            o0_ref[0, 0, :, cols] = res.astype(BF16)
            return
        o_ref = (o1_ref, o2_ref)[gi - 1]
        d = DILATIONS[gi]
        base = (((gi - 1) * (gw // PIECE_W) + k) % 2) * pslabs
        for c in range(pslabs):
            sc[base + c] = res[:, c * LANES:(c + 1) * LANES]
        for p in range(d):
            for c in range(pslabs):
                o_ref[0, p, :, k * PIECE_W + c * LANES:k * PIECE_W + (c + 1) * LANES] = (
                    sc[base + c, pl.ds(p, TM // d, stride=d), :].astype(BF16))

    def gate_dots(i):
        xh = state["xcb", i]
        state["pre_x", i] = jnp.dot(xh, wx_ref[i], preferred_element_type=F32)
        state["pre_a", i] = jnp.dot(xh, wa_ref[i], preferred_element_type=F32)

    def conv(i):
        cols = slice(i * LRU_HD, (i + 1) * LRU_HD)
        xc = cb_ref[:, cols] + cw_ref[0:1, cols] * xbuf[SUBLANES - 3:SUBLANES - 3 + TM, cols]
        for j in range(1, CONV_W):
            off = SUBLANES - (CONV_W - 1) + j
            xc = xc + cw_ref[j:j + 1, cols] * xbuf[off:off + TM, cols]
        xbuf[0:SUBLANES, cols] = xbuf[TM:TM + SUBLANES, cols]
        state["xc", i] = xc
        state["xcb", i] = xc.astype(BF16)

    def gates(i):
        cols = slice(i * LRU_HD, (i + 1) * LRU_HD)
        gx = _sigmoid(state["pre_x", i] + bx_ref[:, cols])
        ga = _sigmoid(state["pre_a", i] + ba_ref[:, cols])
        nla = ga * c_softplus[:, cols]
        a = jnp.exp(-nla)
        t = jnp.tanh(nla)
        w = (t + t) / (1.0 + t)
        b = (w * lax.rsqrt(jnp.maximum(w, TINY))) * (gx * state["xc", i])
        for k in range(slabs_per_head):
            c = i * slabs_per_head + k
            for s in range(SUBLANES):
                a_s[c, s * PITCH:s * PITCH + SEG, :] = a[s * SEG:(s + 1) * SEG, k * LANES:(k + 1) * LANES]
                b_s[c, s * PITCH:s * PITCH + SEG, :] = b[s * SEG:(s + 1) * SEG, k * LANES:(k + 1) * LANES]

    def scan(i):
        slab_ids = range(i * slabs_per_head, (i + 1) * slabs_per_head)
        hs = {c: jnp.zeros((SUBLANES, LANES), F32) for c in slab_ids}
        cum = {c: jnp.ones((SUBLANES, LANES), F32) for c in slab_ids}
        for j in range(SEG):
            idx = pl.ds(j, SUBLANES, stride=PITCH)
            for c in slab_ids:
                aj = a_s[c, idx, :]
                hs[c] = aj * hs[c] + b_s[c, idx, :]
                cum[c] = aj * cum[c]
                a_s[c, idx, :] = cum[c]
                b_s[c, idx, :] = hs[c]
        row = lax.broadcasted_iota(jnp.int32, (SUBLANES, LANES), 0)
        for c in slab_ids:
            cols = slice(c * LANES, (c + 1) * LANES)
            h_in = jnp.broadcast_to(hcar[SUBLANES - 1:SUBLANES, cols], (SUBLANES, LANES))
            cin = h_in
            for _ in range(SUBLANES - 1):
                out = hs[c] + cum[c] * cin
                cin = jnp.where(row == 0, h_in, pltpu.roll(out, 1, 0))
            cin_s[c] = cin
            hcar[:, cols] = hs[c] + cum[c] * cin
        for c in slab_ids:
            cols = slice(c * LANES, (c + 1) * LANES)
            for s in range(SUBLANES):
                rows = slice(s * SEG, (s + 1) * SEG)
                prow = slice(s * PITCH, s * PITCH + SEG)
                h = b_s[c, prow, :] + a_s[c, prow, :] * cin_s[c, s:s + 1, :]
                rec_ref[0, rows, cols] = (h * gy_s[rows, cols]).astype(BF16)

    heads_per_piece = PIECE_W // LRU_HD
    filler = ([functools.partial(proj_qkv, gi, k) for gi in range(N_GROUPS) for k in range(gw // PIECE_W)]
              + [functools.partial(proj_gl, k) for k in range(2 * D_MODEL // PIECE_W)])
    filler = iter(filler)

    def fill(n=1):
        for _ in range(n):
            piece = next(filler, None)
            if piece is not None:
                piece()

    for k in range(LRU_W // PIECE_W):
        proj_x(k)
    for k in range(LRU_W // PIECE_W):
        proj_y(k)
        for i in range(k * heads_per_piece, (k + 1) * heads_per_piece):
            conv(i)
            fill()
            gate_dots(i)
    for i in range(LRU_HEADS):
        gates(i)
        fill()
        scan(i)
        fill()
    for piece in filler:
        piece()


def _mixin(u3, w_qkv, w_xyg, conv_w, conv_b, wx, bx, wa, ba, ap):
    bsz, seq, _ = u3.shape
    n_slab = LRU_W // LANES
    gw = 3 * GROUP_W

    def row(w):
        return pl.BlockSpec((1, TM, w), lambda b, t: (b, t, 0))

    out_specs, out_shape = [], []
    for d in DILATIONS:
        out_specs.append(pl.BlockSpec((1, d, TM // d, gw), lambda b, t: (b, 0, t, 0)))
        out_shape.append(jax.ShapeDtypeStruct((bsz, d, seq // d, gw), BF16))
    out_specs += [row(LRU_W), row(2 * D_MODEL)]
    out_shape += [jax.ShapeDtypeStruct((bsz, seq, LRU_W), BF16),
                  jax.ShapeDtypeStruct((bsz, seq, 2 * D_MODEL), BF16)]
    return pl.pallas_call(
        _mixin_kernel,
        name="mixin",
        grid=(bsz, seq // TM),
        in_specs=[row(D_MODEL),
                  _const_spec((D_MODEL, N_GROUPS * gw)), _const_spec((D_MODEL, 4 * LRU_W)),
                  _const_spec((CONV_W, LRU_W)), _const_spec((1, LRU_W)),
                  _const_spec((LRU_HEADS, LRU_HD, LRU_HD)), _const_spec((1, LRU_W)),
                  _const_spec((LRU_HEADS, LRU_HD, LRU_HD)), _const_spec((1, LRU_W)),
                  _const_spec((1, LRU_W))],
        out_specs=out_specs,
        out_shape=out_shape,
        scratch_shapes=[pltpu.VMEM((2 * PIECE_W // LANES, TM, LANES), F32),
                        pltpu.VMEM((TM + SUBLANES, LRU_W), F32),
                        pltpu.VMEM((n_slab, SUBLANES * PITCH, LANES), F32),
                        pltpu.VMEM((n_slab, SUBLANES * PITCH, LANES), F32),
                        pltpu.VMEM((TM, LRU_W), F32),
                        pltpu.VMEM((SUBLANES, LRU_W), F32),
                        pltpu.VMEM((n_slab, SUBLANES, LANES), F32)],
        compiler_params=pltpu.CompilerParams(
            dimension_semantics=("arbitrary", "arbitrary"), vmem_limit_bytes=VMEM_LIMIT),
    )(u3, w_qkv, w_xyg, conv_w, conv_b, wx, bx, wa, ba, ap)


def _bucket_table():
    qi = np.arange(BAND)[:, None]
    kj = np.arange(BAND)[None, :]
    max_exact = REL_BUCKETS // 2
    out = np.zeros((N_GROUPS, 2, BAND, BAND), np.int32)
    for g, d in enumerate(DILATIONS):
        for half in range(2):
            steps = qi + BAND - kj if half == 0 else qi - kj
            valid = (steps >= 0) & (steps <= BAND)
            dist = np.maximum(steps, 0) * d
            nf = np.maximum(dist, 1).astype(np.float32)
            large = max_exact + (np.log(nf / np.float32(max_exact))
                                 / np.float32(math.log(REL_MAX_DISTANCE / max_exact))
                                 * np.float32(REL_BUCKETS - max_exact)).astype(np.int32)
            large = np.minimum(large, REL_BUCKETS - 1)
            bucket = np.where(dist < max_exact, dist, large)
            out[g, half] = np.where(valid, bucket, -1)
    return out


def _attn_kernel(group, pp, rows, tab_ref, bkt_ref, q_ref, k_ref, v_ref, o_ref, st_ref,
                 kbuf, vbuf, bias_s):
    first = (pl.program_id(0) == 0) & (pl.program_id(1) == 0) & (pl.program_id(2) == 0)
    c = pl.program_id(2)
    log2e = math.log2(math.e)
    vw = 2 * HEAD_DIM

    @pl.when(first)
    def _():
        for h in range(HEADS_PER_GROUP):
            for half in range(2):
                bk = bkt_ref[half]
                bias = jnp.full((BAND, BAND), -jnp.inf, F32)
                for n in range(REL_BUCKETS):
                    bias = jnp.where(bk == n, tab_ref[n, group * HEADS_PER_GROUP + h] * log2e, bias)
                bias_s[h, :, half * BAND:(half + 1) * BAND] = bias
            for ph in range(pp):
                vbuf[ph, :, h * vw + HEAD_DIM:(h + 1) * vw] = jnp.ones((BAND + rows, HEAD_DIM), BF16)

    @pl.when(c == 0)
    def _():
        for ph in range(pp):
            kbuf[ph, 0:BAND, :] = jnp.zeros((BAND, GROUP_W), BF16)
            for h in range(HEADS_PER_GROUP):
                vbuf[ph, 0:BAND, h * vw:h * vw + HEAD_DIM] = jnp.zeros((BAND, HEAD_DIM), BF16)

    for ph in range(pp):
        kbuf[ph, BAND:BAND + rows, :] = k_ref[ph]
        for h in range(HEADS_PER_GROUP):
            vbuf[ph, BAND:BAND + rows, h * vw:h * vw + HEAD_DIM] = v_ref[ph, :, h * HEAD_DIM:(h + 1) * HEAD_DIM]

    lane2 = lax.broadcasted_iota(jnp.int32, (1, 2 * BAND), 1)
    pen = jnp.where((lane2 < BAND) & (c == 0), -jnp.inf, 0.0).astype(F32)
    qk_scale = log2e / math.sqrt(HEAD_DIM)
    nt = (((1,), (1,)), ((), ()))
    units = [(ph, n, h) for ph in range(pp) for n in range(rows // BAND) for h in range(HEADS_PER_GROUP)]

    scores = []
    for ph, n, h in units:
        cols = slice(h * HEAD_DIM, (h + 1) * HEAD_DIM)
        q = q_ref[ph, n * BAND:(n + 1) * BAND, cols]
        kk = kbuf[ph, n * BAND:(n + 2) * BAND, cols]
        s = lax.dot_general(q, kk, nt, preferred_element_type=F32) * qk_scale + bias_s[h]
        if n == 0:
            s = s + pen
        scores.append(s)
    maxes = [jnp.max(s, axis=-1, keepdims=True) for s in scores]
    probs = [jnp.exp2(s - m).astype(BF16) for s, m in zip(scores, maxes)]
    outs = [jnp.dot(p, vbuf[ph, n * BAND:(n + 2) * BAND, h * vw:(h + 1) * vw], preferred_element_type=F32)
            for (ph, n, h), p in zip(units, probs)]
    lane = lax.broadcasted_iota(jnp.int32, (BAND, LANES), 1)
    stats = None
    for (ph, n, h), o, m in zip(units, outs, maxes):
        o_ref[ph, n * BAND:(n + 1) * BAND, h * HEAD_DIM:(h + 1) * HEAD_DIM] = o[:, :HEAD_DIM].astype(BF16)
        den = o[:, HEAD_DIM:]
        stats = jnp.broadcast_to(m, (BAND, LANES)) if h == 0 else jnp.where(lane == h, m, stats)
        stats = jnp.where(lane == HEADS_PER_GROUP + h, den, stats)
        if h == HEADS_PER_GROUP - 1:
            st_ref[ph, n * BAND:(n + 1) * BAND, :] = stats

    for ph in range(pp):
        kbuf[ph, 0:BAND, :] = kbuf[ph, rows:rows + BAND, :]
        vbuf[ph, 0:BAND, :] = vbuf[ph, rows:rows + BAND, :]


def _attn_group(group, qkv_g, table, bkt):
    bsz, d, sub, _ = qkv_g.shape
    rows = min(ATT_R, sub)
    pp = min(d, ATT_R // rows)
    assert sub % rows == 0 and d % pp == 0 and (pp == 1 or rows == sub)

    def spec(width, col):
        return pl.BlockSpec((None, pp, rows, width), lambda b, p, c: (b, p, c, col))

    return pl.pallas_call(
        functools.partial(_attn_kernel, group, pp, rows),
        name=f"attn{group}",
        grid=(bsz, d // pp, sub // rows),
        in_specs=[pl.BlockSpec(memory_space=pltpu.SMEM),
                  pl.BlockSpec((None, 2, BAND, BAND), lambda b, p, c: (group, 0, 0, 0)),
                  spec(GROUP_W, 0), spec(GROUP_W, 1), spec(GROUP_W, 2)],
        out_specs=[spec(GROUP_W, 0), spec(LANES, 0)],
        out_shape=[jax.ShapeDtypeStruct((bsz, d, sub, GROUP_W), BF16),
                   jax.ShapeDtypeStruct((bsz, d, sub, LANES), F32)],
        scratch_shapes=[pltpu.VMEM((pp, BAND + rows, GROUP_W), BF16),
                        pltpu.VMEM((pp, BAND + rows, 2 * GROUP_W), BF16),
                        pltpu.VMEM((HEADS_PER_GROUP, BAND, 2 * BAND), F32)],
        compiler_params=pltpu.CompilerParams(
            dimension_semantics=("arbitrary", "arbitrary", "arbitrary"),
            vmem_limit_bytes=VMEM_LIMIT),
    )(table, bkt, qkv_g, qkv_g, qkv_g)


def _merge_kernel(h_ref, o0_ref, o1_ref, o2_ref, l0_ref, l1_ref, l2_ref, rec_ref, gl_ref,
                  wat_ref, wrec_ref, wout_ref, gpost_ref, out_ref, osc, lsc):
    rec_d = jnp.dot(rec_ref[...], wrec_ref[...], preferred_element_type=F32)
    for gi, o_ref, l_ref in ((1, o1_ref, l1_ref), (2, o2_ref, l2_ref)):
        d = DILATIONS[gi]
        for p in range(d):
            idx = pl.ds(p, TM // d, stride=d)
            lsc[gi - 1, idx, :] = l_ref[p]
            for h in range(HEADS_PER_GROUP):
                osc[(gi - 1) * HEADS_PER_GROUP + h, idx, :] = (
                    o_ref[p, :, h * HEAD_DIM:(h + 1) * HEAD_DIM].astype(F32))
    st = (l0_ref[0], lsc[0], lsc[1])
    mx = jnp.maximum(jnp.maximum(st[0], st[1]), st[2])
    e = [jnp.exp2(x - mx) for x in st]
    den = [pltpu.roll(x, LANES - HEADS_PER_GROUP, 1) for x in st]
    inv = 1.0 / (e[0] * den[0] + e[1] * den[1] + e[2] * den[2])
    w0, w1, w2 = e[0] * inv, e[1] * inv, e[2] * inv
    parts = []
    for h in range(HEADS_PER_GROUP):
        cols = slice(h * HEAD_DIM, (h + 1) * HEAD_DIM)
        parts.append(w0[:, h:h + 1] * o0_ref[0, :, cols].astype(F32)
                     + w1[:, h:h + 1] * osc[h]
                     + w2[:, h:h + 1] * osc[HEADS_PER_GROUP + h])
    attn = jnp.concatenate(parts, axis=-1).astype(BF16)
    attn_d = jnp.dot(attn, wat_ref[...], preferred_element_type=F32)
    gl = gl_ref[...].astype(F32)
    merged = _sigmoid(gl[:, :D_MODEL]) * attn_d + _sigmoid(gl[:, D_MODEL:]) * rec_d
    mo = jnp.dot(merged.astype(BF16), wout_ref[...], preferred_element_type=F32)
    out_ref[...] = h_ref[...] + _rms(mo, gpost_ref[...])


def _merge(h3, o_list, lse_list, rec3, gl3, w_attn, w_rec, w_out, g_post):
    bsz, seq, _ = h3.shape

    def row(w):
        return pl.BlockSpec((None, TM, w), lambda b, i: (b, i, 0))

    def phased(d, w):
        return pl.BlockSpec((None, d, TM // d, w), lambda b, i: (b, 0, i, 0))

    return pl.pallas_call(
        _merge_kernel,
        name="merge",
        grid=(bsz, seq // TM),
        in_specs=[row(D_MODEL)] + [phased(d, GROUP_W) for d in DILATIONS]
                 + [phased(d, LANES) for d in DILATIONS] + [row(LRU_W), row(2 * D_MODEL),
                 _const_spec((GROUP_W, D_MODEL)), _const_spec((LRU_W, D_MODEL)),
                 _const_spec((D_MODEL, D_MODEL)), _const_spec((1, D_MODEL))],
        out_specs=row(D_MODEL),
        out_shape=jax.ShapeDtypeStruct((bsz, seq, D_MODEL), F32),
        scratch_shapes=[pltpu.VMEM((2 * HEADS_PER_GROUP, TM, LANES), F32),
                        pltpu.VMEM((2, TM, LANES), F32)],
        compiler_params=pltpu.CompilerParams(
            dimension_semantics=("arbitrary", "arbitrary"), vmem_limit_bytes=VMEM_LIMIT),
    )(h3, *o_list, *lse_list, rec3, gl3, w_attn, w_rec, w_out, g_post)


def kernel(x, ffn1_norm_pre, ffn1_norm_post, ffn1_w_gate, ffn1_w_up, ffn1_w_down, mix_norm_pre, mix_norm_post, w_in, rel_bias_table, conv_w, conv_b, lru_w_x, lru_b_x, lru_w_a, lru_b_a, lru_a_param, w_attn_branch, w_rec_branch, w_out, ffn2_norm_pre, ffn2_norm_post, ffn2_w_gate, ffn2_w_up, ffn2_w_down):
    bsz, seq, _ = x.shape
    n = bsz * seq
    depth = w_in.shape[0]
    bkt = jnp.asarray(_bucket_table())
    h = x.reshape(n, D_MODEL)
    for l in range(depth):
        gw = 3 * GROUP_W
        regroup = tuple((g * gw + j * GROUP_W, j * QKV_W + g * GROUP_W, GROUP_W)
                        for g in range(N_GROUPS) for j in range(3))
        jobs = [_CastJob(w_in[l], ((3 * QKV_W, regroup), (4 * LRU_W, ((0, 3 * QKV_W, 4 * LRU_W),)))),
                _plain_cast(lru_w_x[l].reshape(LRU_W, LRU_HD)), _plain_cast(lru_w_a[l].reshape(LRU_W, LRU_HD)),
                _plain_cast(w_attn_branch[l]), _plain_cast(w_rec_branch[l]), _plain_cast(w_out[l]),
                _plain_cast(ffn2_w_gate[l]), _plain_cast(ffn2_w_up[l]), _plain_cast(ffn2_w_down[l])]
        (h, u, w_qkv, w_xyg, wx, wa, w_attn, w_rec, w_o, w2_gate, w2_up, w2_down) = _ffn(
            h, ffn1_norm_pre[l][None], ffn1_norm_post[l][None],
            ffn1_w_gate[l].astype(BF16), ffn1_w_up[l].astype(BF16), ffn1_w_down[l].astype(BF16),
            g_next=mix_norm_pre[l][None], cast_jobs=jobs)
        h3 = h.reshape(bsz, seq, D_MODEL)

        *qkv_groups, rec, gl = _mixin(u.reshape(bsz, seq, D_MODEL), w_qkv, w_xyg,
                                      conv_w[l], conv_b[l][None],
                                      wx.reshape(LRU_HEADS, LRU_HD, LRU_HD), lru_b_x[l].reshape(1, LRU_W),
                                      wa.reshape(LRU_HEADS, LRU_HD, LRU_HD), lru_b_a[l].reshape(1, LRU_W),
                                      lru_a_param[l][None])
        o_list, lse_list = [], []
        for g in range(N_GROUPS):
            o_g, lse_g = _attn_group(g, qkv_groups[g], rel_bias_table, bkt)
            o_list.append(o_g)
            lse_list.append(lse_g)
        h3 = _merge(h3, o_list, lse_list, rec, gl, w_attn, w_rec, w_o, mix_norm_post[l][None])

        h, = _ffn(h3.reshape(n, D_MODEL), ffn2_norm_pre[l][None], ffn2_norm_post[l][None],
                  w2_gate, w2_up, w2_down)
    return h.reshape(bsz, seq, D_MODEL)
```

```python
import functools
import math
from typing import NamedTuple

import numpy as np
import jax
import jax.numpy as jnp
from jax import lax
from jax.experimental import pallas as pl
from jax.experimental.pallas import tpu as pltpu

F32 = jnp.float32
BF16 = jnp.bfloat16

D_MODEL = 1024
HEAD_DIM = 128
HEADS_PER_GROUP = 4
DILATIONS = (1, 4, 16)
BAND = 128
N_GROUPS = 3
GROUP_W = HEADS_PER_GROUP * HEAD_DIM
QKV_W = N_GROUPS * GROUP_W
LRU_W = D_MODEL
LRU_HEADS = 4
LRU_HD = LRU_W // LRU_HEADS
CONV_W = 4
LRU_C = 8.0
D_FF = 2816
REL_BUCKETS = 32
REL_MAX_DISTANCE = 2048
EPS = 1e-6
TINY = 1e-30

LANES = 128
SUBLANES = 8
BF16_TILE_ROWS = 16
MAX_CAST_BLOCKS = 64
VMEM_LIMIT = 56 * 1024 * 1024

TM_FFN = 512
TM = 512
ATT_R = 2048
SEG = TM // SUBLANES
PITCH = SEG + SUBLANES
PIECE_W = 256
PROJ_ROWS = 512
PHASE_STEP = 4
assert DILATIONS == (1, PHASE_STEP, PHASE_STEP * PHASE_STEP)


def _rms(x, g):
    ms = jnp.mean(x * x, axis=-1, keepdims=True)
    return x * lax.rsqrt(ms + EPS) * g


def _sigmoid(x):
    return 1.0 / (1.0 + jnp.exp(-x))


def _gelu_tanh(x):
    c = math.sqrt(2.0 / math.pi)
    inner = x * (c + (c * 0.044715) * (x * x))
    return (0.5 * x) * (1.0 + jnp.tanh(inner))


def _const_spec(shape):
    nd = len(shape)
    return pl.BlockSpec(shape, lambda *_: (0,) * nd, pipeline_mode=pl.Buffered(1))


class _CastJob(NamedTuple):
    src: jax.Array
    outs: tuple

    @property
    def block_rows(self):
        rows = self.src.shape[0]
        br = BF16_TILE_ROWS
        while rows % br or rows // br > MAX_CAST_BLOCKS:
            br += BF16_TILE_ROWS
        return br


def _plain_cast(w):
    return _CastJob(w, ((w.shape[1], ((0, 0, w.shape[1]),)),))


def _ffn_kernel(emit_normed, jobs_pieces, x_ref, gpre_ref, gpost_ref, gnext_ref, wg_ref, wu_ref, wd_ref, *rest):
    n_jobs = len(jobs_pieces)
    n_normed = len(DILATIONS) if emit_normed else 0
    cast_in, o_ref = rest[:n_jobs], rest[n_jobs]
    normed_refs = rest[n_jobs + 1:n_jobs + 1 + n_normed]
    n_cast_out = sum(len(p) for p in jobs_pieces)
    cast_out = rest[n_jobs + 1 + n_normed:n_jobs + 1 + n_normed + n_cast_out]
    scratch = rest[n_jobs + 1 + n_normed + n_cast_out:]
    x = x_ref[...]
    xn = _rms(x, gpre_ref[...]).astype(BF16)
    g = jnp.dot(xn, wg_ref[...], preferred_element_type=F32)
    u = jnp.dot(xn, wu_ref[...], preferred_element_type=F32)
    a = (g * _sigmoid(g) * u).astype(BF16)
    f = jnp.dot(a, wd_ref[...], preferred_element_type=F32)
    h = x + 0.5 * _rms(f, gpost_ref[...])
    o_ref[...] = h
    if emit_normed:
        slab, slab4 = scratch
        u4_ref, u16_ref = normed_refs[1:]
        hn = _rms(h, gnext_ref[...])
        normed_refs[0][...] = hn.astype(BF16)
        r4 = TM_FFN // PHASE_STEP
        for c in range(D_MODEL // LANES):
            lanes = slice(c * LANES, (c + 1) * LANES)
            slab[c] = hn[:, lanes]
            for p in range(PHASE_STEP):
                part = slab[c, pl.ds(p, r4, stride=PHASE_STEP), :]
                u4_ref[p, :, lanes] = part.astype(BF16)
                slab4[c, p * r4:(p + 1) * r4, :] = part
            for p in range(PHASE_STEP):
                for q in range(PHASE_STEP):
                    u16_ref[q * PHASE_STEP + p, :, lanes] = (
                        slab4[c, pl.ds(p * r4 + q, r4 // PHASE_STEP, stride=PHASE_STEP), :].astype(BF16))
    outs = iter(cast_out)
    for src_ref, pieces_per_out in zip(cast_in, jobs_pieces):
        for pieces in pieces_per_out:
            dst_ref = next(outs)
            for dst_col, src_col, width in pieces:
                dst_ref[:, dst_col:dst_col + width] = src_ref[:, src_col:src_col + width].astype(BF16)


def _ffn(x2, g_pre, g_post, wg, wu, wd, g_next=None, seq=None, cast_jobs=()):
    n = x2.shape[0]
    steps = n // TM_FFN
    assert steps <= MAX_CAST_BLOCKS
    emit_normed = g_next is not None
    row = pl.BlockSpec((TM_FFN, D_MODEL), lambda i: (i, 0))
    in_specs = [row, _const_spec((1, D_MODEL)), _const_spec((1, D_MODEL)), _const_spec((1, D_MODEL)),
                _const_spec((D_MODEL, D_FF)), _const_spec((D_MODEL, D_FF)), _const_spec((D_FF, D_MODEL))]
    out_specs = [row]
    out_shape = [jax.ShapeDtypeStruct((n, D_MODEL), F32)]
    scratch_shapes = []
    if emit_normed:
        tiles = seq // TM_FFN
        out_specs.append(row)
        out_shape.append(jax.ShapeDtypeStruct((n, D_MODEL), BF16))
        for d in DILATIONS[1:]:
            out_specs.append(pl.BlockSpec((None, d, TM_FFN // d, D_MODEL),
                                          functools.partial(lambda i, t: (i // t, 0, i % t, 0), t=tiles)))
            out_shape.append(jax.ShapeDtypeStruct((n // seq, d, seq // d, D_MODEL), BF16))
        scratch_shapes += [pltpu.VMEM((D_MODEL // LANES, TM_FFN, LANES), F32)] * 2
    for job in cast_jobs:
        rows, br = job.src.shape[0], job.block_rows
        index_map = functools.partial(lambda i, last: (jnp.minimum(i, last), 0), last=rows // br - 1)
        in_specs.append(pl.BlockSpec((br, job.src.shape[1]), index_map))
        for cols, _ in job.outs:
            out_specs.append(pl.BlockSpec((br, cols), index_map))
            out_shape.append(jax.ShapeDtypeStruct((rows, cols), BF16))
    jobs_pieces = tuple(tuple(pieces for _, pieces in job.outs) for job in cast_jobs)
    return pl.pallas_call(
        functools.partial(_ffn_kernel, emit_normed, jobs_pieces),
        name="ffn",
        grid=(steps,),
        in_specs=in_specs,
        out_specs=out_specs,
        out_shape=out_shape,
        scratch_shapes=scratch_shapes,
        compiler_params=pltpu.CompilerParams(
            dimension_semantics=("arbitrary",), vmem_limit_bytes=VMEM_LIMIT),
    )(x2, g_pre, g_post, g_next if emit_normed else g_post, wg, wu, wd, *[job.src for job in cast_jobs])


def _lru_kernel(u_ref, wlru_ref, cw_ref, cb_ref, wx_ref, bx_ref, wa_ref, ba_ref, ap_ref,
                rec_ref, gl_ref, xbuf, a_s, b_s, gy_s, hcar, cin_s):
    @pl.when(pl.program_id(1) == 0)
    def _():
        xbuf[0:SUBLANES, :] = jnp.zeros((SUBLANES, LRU_W), F32)
        hcar[...] = jnp.zeros(hcar.shape, F32)

    slabs_per_head = LRU_HD // LANES
    ap = -ap_ref[...]
    c_softplus = LRU_C * (jnp.maximum(ap, 0.0) + jnp.log1p(jnp.exp(-jnp.abs(ap))))
    state = {}

    def proj_x(k):
        cols = slice(k * PIECE_W, (k + 1) * PIECE_W)
        xbuf[SUBLANES:SUBLANES + TM, cols] = jnp.dot(u_ref[0], wlru_ref[:, cols], preferred_element_type=F32)

    def proj_y(k):
        cols = slice(k * PIECE_W, (k + 1) * PIECE_W)
        yr = jnp.dot(u_ref[0], wlru_ref[:, LRU_W + k * PIECE_W:LRU_W + (k + 1) * PIECE_W],
                     preferred_element_type=F32)
        gy_s[:, cols] = _gelu_tanh(yr)

    def proj_gl(k):
        cols = slice(k * PIECE_W, (k + 1) * PIECE_W)
        gl_ref[0, :, cols] = jnp.dot(u_ref[0], wlru_ref[:, 2 * LRU_W + k * PIECE_W:2 * LRU_W + (k + 1) * PIECE_W],
                                     preferred_element_type=F32).astype(BF16)

    def gate_dots(i):
        xh = state["xcb", i]
        state["pre_x", i] = jnp.dot(xh, wx_ref[i], preferred_element_type=F32)
        state["pre_a", i] = jnp.dot(xh, wa_ref[i], preferred_element_type=F32)

    def conv(i):
        cols = slice(i * LRU_HD, (i + 1) * LRU_HD)
        xc = cb_ref[:, cols] + cw_ref[0:1, cols] * xbuf[SUBLANES - 3:SUBLANES - 3 + TM, cols]
        for j in range(1, CONV_W):
            off = SUBLANES - (CONV_W - 1) + j
            xc = xc + cw_ref[j:j + 1, cols] * xbuf[off:off + TM, cols]
        xbuf[0:SUBLANES, cols] = xbuf[TM:TM + SUBLANES, cols]
        state["xc", i] = xc
        state["xcb", i] = xc.astype(BF16)

    def gates(i):
        cols = slice(i * LRU_HD, (i + 1) * LRU_HD)
        gx = _sigmoid(state["pre_x", i] + bx_ref[:, cols])
        ga = _sigmoid(state["pre_a", i] + ba_ref[:, cols])
        nla = ga * c_softplus[:, cols]
        a = jnp.exp(-nla)
        t = jnp.tanh(nla)
        w = (t + t) / (1.0 + t)
        b = (w * lax.rsqrt(jnp.maximum(w, TINY))) * (gx * state["xc", i])
        for k in range(slabs_per_head):
            c = i * slabs_per_head + k
            for s in range(SUBLANES):
                a_s[c, s * PITCH:s * PITCH + SEG, :] = a[s * SEG:(s + 1) * SEG, k * LANES:(k + 1) * LANES]
                b_s[c, s * PITCH:s * PITCH + SEG, :] = b[s * SEG:(s + 1) * SEG, k * LANES:(k + 1) * LANES]

    def scan(i):
        slab_ids = range(i * slabs_per_head, (i + 1) * slabs_per_head)
        hs = {c: jnp.zeros((SUBLANES, LANES), F32) for c in slab_ids}
        cum = {c: jnp.ones((SUBLANES, LANES), F32) for c in slab_ids}
        for j in range(SEG):
            idx = pl.ds(j, SUBLANES, stride=PITCH)
            for c in slab_ids:
                aj = a_s[c, idx, :]
                hs[c] = aj * hs[c] + b_s[c, idx, :]
                cum[c] = aj * cum[c]
                a_s[c, idx, :] = cum[c]
                b_s[c, idx, :] = hs[c]
        row = lax.broadcasted_iota(jnp.int32, (SUBLANES, LANES), 0)
        for c in slab_ids:
            cols = slice(c * LANES, (c + 1) * LANES)
            h_in = jnp.broadcast_to(hcar[SUBLANES - 1:SUBLANES, cols], (SUBLANES, LANES))
            cin = h_in
            for _ in range(SUBLANES - 1):
                out = hs[c] + cum[c] * cin
                cin = jnp.where(row == 0, h_in, pltpu.roll(out, 1, 0))
            cin_s[c] = cin
            hcar[:, cols] = hs[c] + cum[c] * cin
        for c in slab_ids:
            cols = slice(c * LANES, (c + 1) * LANES)
            for s in range(SUBLANES):
                rows = slice(s * SEG, (s + 1) * SEG)
                prow = slice(s * PITCH, s * PITCH + SEG)
                h = b_s[c, prow, :] + a_s[c, prow, :] * cin_s[c, s:s + 1, :]
                rec_ref[0, rows, cols] = (h * gy_s[rows, cols]).astype(BF16)

    heads_per_piece = PIECE_W // LRU_HD
    filler = iter([functools.partial(proj_gl, k) for k in range(2 * D_MODEL // PIECE_W)])

    def fill():
        piece = next(filler, None)
        if piece is not None:
            piece()

    for k in range(LRU_W // PIECE_W):
        proj_x(k)
    for k in range(LRU_W // PIECE_W):
        proj_y(k)
        for i in range(k * heads_per_piece, (k + 1) * heads_per_piece):
            conv(i)
            fill()
            gate_dots(i)
    for i in range(LRU_HEADS):
        gates(i)
        fill()
        scan(i)
    for piece in filler:
        piece()


def _lru(u3, w_xyg, conv_w, conv_b, wx, bx, wa, ba, ap):
    bsz, seq, _ = u3.shape
    n_slab = LRU_W // LANES

    def row(w):
        return pl.BlockSpec((1, TM, w), lambda b, t: (b, t, 0))

    return pl.pallas_call(
        _lru_kernel,
        name="lru",
        grid=(bsz, seq // TM),
        in_specs=[row(D_MODEL), _const_spec((D_MODEL, 4 * LRU_W)),
                  _const_spec((CONV_W, LRU_W)), _const_spec((1, LRU_W)),
                  _const_spec((LRU_HEADS, LRU_HD, LRU_HD)), _const_spec((1, LRU_W)),
                  _const_spec((LRU_HEADS, LRU_HD, LRU_HD)), _const_spec((1, LRU_W)),
                  _const_spec((1, LRU_W))],
        out_specs=[row(LRU_W), row(2 * D_MODEL)],
        out_shape=[jax.ShapeDtypeStruct((bsz, seq, LRU_W), BF16),
                   jax.ShapeDtypeStruct((bsz, seq, 2 * D_MODEL), BF16)],
        scratch_shapes=[pltpu.VMEM((TM + SUBLANES, LRU_W), F32),
                        pltpu.VMEM((n_slab, SUBLANES * PITCH, LANES), F32),
                        pltpu.VMEM((n_slab, SUBLANES * PITCH, LANES), F32),
                        pltpu.VMEM((TM, LRU_W), F32),
                        pltpu.VMEM((SUBLANES, LRU_W), F32),
                        pltpu.VMEM((n_slab, SUBLANES, LANES), F32)],
        compiler_params=pltpu.CompilerParams(
            dimension_semantics=("arbitrary", "arbitrary"), vmem_limit_bytes=VMEM_LIMIT),
    )(u3, w_xyg, conv_w, conv_b, wx, bx, wa, ba, ap)


def _bucket_table():
    qi = np.arange(BAND)[:, None]
    kj = np.arange(BAND)[None, :]
    max_exact = REL_BUCKETS // 2
    out = np.zeros((N_GROUPS, 2, BAND, BAND), np.int32)
    for g, d in enumerate(DILATIONS):
        for half in range(2):
            steps = qi + BAND - kj if half == 0 else qi - kj
            valid = (steps >= 0) & (steps <= BAND)
            dist = np.maximum(steps, 0) * d
            nf = np.maximum(dist, 1).astype(np.float32)
            large = max_exact + (np.log(nf / np.float32(max_exact))
                                 / np.float32(math.log(REL_MAX_DISTANCE / max_exact))
                                 * np.float32(REL_BUCKETS - max_exact)).astype(np.int32)
            large = np.minimum(large, REL_BUCKETS - 1)
            bucket = np.where(dist < max_exact, dist, large)
            out[g, half] = np.where(valid, bucket, -1)
    return out


def _attn_kernel(group, pp, rows, tab_ref, bkt_ref, u_ref, w_ref, o_ref, st_ref,
                 qbuf, kbuf, vbuf, bias_s):
    first = (pl.program_id(0) == 0) & (pl.program_id(1) == 0) & (pl.program_id(2) == 0)
    c = pl.program_id(2)
    log2e = math.log2(math.e)
    vw = 2 * HEAD_DIM

    @pl.when(first)
    def _():
        for h in range(HEADS_PER_GROUP):
            for half in range(2):
                bk = bkt_ref[half]
                bias = jnp.full((BAND, BAND), -jnp.inf, F32)
                for n in range(REL_BUCKETS):
                    bias = jnp.where(bk == n, tab_ref[n, group * HEADS_PER_GROUP + h] * log2e, bias)
                bias_s[h, :, half * BAND:(half + 1) * BAND] = bias
            for ph in range(pp):
                vbuf[ph, :, h * vw + HEAD_DIM:(h + 1) * vw] = jnp.ones((BAND + rows, HEAD_DIM), BF16)

    @pl.when(c == 0)
    def _():
        for ph in range(pp):
            kbuf[ph, 0:BAND, :] = jnp.zeros((BAND, GROUP_W), BF16)
            for h in range(HEADS_PER_GROUP):
                vbuf[ph, 0:BAND, h * vw:h * vw + HEAD_DIM] = jnp.zeros((BAND, HEAD_DIM), BF16)

    lane2 = lax.broadcasted_iota(jnp.int32, (1, 2 * BAND), 1)
    pen = jnp.where((lane2 < BAND) & (c == 0), -jnp.inf, 0.0).astype(F32)
    qk_scale = log2e / math.sqrt(HEAD_DIM)
    nt = (((1,), (1,)), ((), ()))
    lane = lax.broadcasted_iota(jnp.int32, (BAND, LANES), 1)

    def project(ph, r0):
        qkv = jnp.dot(u_ref[ph, r0:r0 + PROJ_ROWS, :], w_ref[...], preferred_element_type=F32).astype(BF16)
        qbuf[ph, r0:r0 + PROJ_ROWS, :] = qkv[:, :GROUP_W]
        kbuf[ph, BAND + r0:BAND + r0 + PROJ_ROWS, :] = qkv[:, GROUP_W:2 * GROUP_W]
        for h in range(HEADS_PER_GROUP):
            vbuf[ph, BAND + r0:BAND + r0 + PROJ_ROWS, h * vw:h * vw + HEAD_DIM] = (
                qkv[:, 2 * GROUP_W + h * HEAD_DIM:2 * GROUP_W + (h + 1) * HEAD_DIM])

    def attend(ph, r0):
        units = [(n, h) for n in range(r0 // BAND, (r0 + PROJ_ROWS) // BAND) for h in range(HEADS_PER_GROUP)]
        scores = []
        for n, h in units:
            cols = slice(h * HEAD_DIM, (h + 1) * HEAD_DIM)
            q = qbuf[ph, n * BAND:(n + 1) * BAND, cols]
            kk = kbuf[ph, n * BAND:(n + 2) * BAND, cols]
            s = lax.dot_general(q, kk, nt, preferred_element_type=F32) * qk_scale + bias_s[h]
            if n == 0:
                s = s + pen
            scores.append(s)
        maxes = [jnp.max(s, axis=-1, keepdims=True) for s in scores]
        probs = [jnp.exp2(s - m).astype(BF16) for s, m in zip(scores, maxes)]
        outs = [jnp.dot(p, vbuf[ph, n * BAND:(n + 2) * BAND, h * vw:(h + 1) * vw], preferred_element_type=F32)
                for (n, h), p in zip(units, probs)]
        stats = None
        for (n, h), o, m in zip(units, outs, maxes):
            o_ref[ph, n * BAND:(n + 1) * BAND, h * HEAD_DIM:(h + 1) * HEAD_DIM] = o[:, :HEAD_DIM].astype(BF16)
            den = o[:, HEAD_DIM:]
            stats = jnp.broadcast_to(m, (BAND, LANES)) if h == 0 else jnp.where(lane == h, m, stats)
            stats = jnp.where(lane == HEADS_PER_GROUP + h, den, stats)
            if h == HEADS_PER_GROUP - 1:
                st_ref[ph, n * BAND:(n + 1) * BAND, :] = stats

    sub_blocks = [(ph, r0) for ph in range(pp) for r0 in range(0, rows, PROJ_ROWS)]
    project(*sub_blocks[0])
    for i, sb in enumerate(sub_blocks):
        if i + 1 < len(sub_blocks):
            project(*sub_blocks[i + 1])
        attend(*sb)

    for ph in range(pp):
        kbuf[ph, 0:BAND, :] = kbuf[ph, rows:rows + BAND, :]
        vbuf[ph, 0:BAND, :] = vbuf[ph, rows:rows + BAND, :]


def _attn_group(group, u_g, w_qkv, table, bkt):
    bsz, d, sub, _ = u_g.shape
    rows = min(ATT_R, sub)
    pp = min(d, ATT_R // rows)
    assert sub % rows == 0 and d % pp == 0 and (pp == 1 or rows == sub) and rows % PROJ_ROWS == 0

    def spec(width):
        return pl.BlockSpec((None, pp, rows, width), lambda b, p, c: (b, p, c, 0))

    return pl.pallas_call(
        functools.partial(_attn_kernel, group, pp, rows),
        name=f"attn{group}",
        grid=(bsz, d // pp, sub // rows),
        in_specs=[pl.BlockSpec(memory_space=pltpu.SMEM),
                  pl.BlockSpec((None, 2, BAND, BAND), lambda b, p, c: (group, 0, 0, 0)),
                  spec(D_MODEL),
                  pl.BlockSpec((D_MODEL, 3 * GROUP_W), lambda b, p, c: (0, group), pipeline_mode=pl.Buffered(1))],
        out_specs=[spec(GROUP_W), spec(LANES)],
        out_shape=[jax.ShapeDtypeStruct((bsz, d, sub, GROUP_W), BF16),
                   jax.ShapeDtypeStruct((bsz, d, sub, LANES), F32)],
        scratch_shapes=[pltpu.VMEM((pp, rows, GROUP_W), BF16),
                        pltpu.VMEM((pp, BAND + rows, GROUP_W), BF16),
                        pltpu.VMEM((pp, BAND + rows, 2 * GROUP_W), BF16),
                        pltpu.VMEM((HEADS_PER_GROUP, BAND, 2 * BAND), F32)],
        compiler_params=pltpu.CompilerParams(
            dimension_semantics=("arbitrary", "arbitrary", "arbitrary"),
            vmem_limit_bytes=VMEM_LIMIT),
    )(table, bkt, u_g, w_qkv)


def _merge_kernel(h_ref, o0_ref, o1_ref, o2_ref, l0_ref, l1_ref, l2_ref, rec_ref, gl_ref,
                  wat_ref, wrec_ref, wout_ref, gpost_ref, out_ref, osc, lsc):
    rec_d = jnp.dot(rec_ref[...], wrec_ref[...], preferred_element_type=F32)
    for gi, o_ref, l_ref in ((1, o1_ref, l1_ref), (2, o2_ref, l2_ref)):
        d = DILATIONS[gi]
        for p in range(d):
            idx = pl.ds(p, TM // d, stride=d)
            lsc[gi - 1, idx, :] = l_ref[p]
            for h in range(HEADS_PER_GROUP):
                osc[(gi - 1) * HEADS_PER_GROUP + h, idx, :] = (
                    o_ref[p, :, h * HEAD_DIM:(h + 1) * HEAD_DIM].astype(F32))
    st = (l0_ref[0], lsc[0], lsc[1])
    mx = jnp.maximum(jnp.maximum(st[0], st[1]), st[2])
    e = [jnp.exp2(x - mx) for x in st]
    den = [pltpu.roll(x, LANES - HEADS_PER_GROUP, 1) for x in st]
    inv = 1.0 / (e[0] * den[0] + e[1] * den[1] + e[2] * den[2])
    w0, w1, w2 = e[0] * inv, e[1] * inv, e[2] * inv
    parts = []
    for h in range(HEADS_PER_GROUP):
        cols = slice(h * HEAD_DIM, (h + 1) * HEAD_DIM)
        parts.append(w0[:, h:h + 1] * o0_ref[0, :, cols].astype(F32)
                     + w1[:, h:h + 1] * osc[h]
                     + w2[:, h:h + 1] * osc[HEADS_PER_GROUP + h])
    attn = jnp.concatenate(parts, axis=-1).astype(BF16)
    attn_d = jnp.dot(attn, wat_ref[...], preferred_element_type=F32)
    gl = gl_ref[...].astype(F32)
    merged = _sigmoid(gl[:, :D_MODEL]) * attn_d + _sigmoid(gl[:, D_MODEL:]) * rec_d
    mo = jnp.dot(merged.astype(BF16), wout_ref[...], preferred_element_type=F32)
    out_ref[...] = h_ref[...] + _rms(mo, gpost_ref[...])


def _merge(h3, o_list, lse_list, rec3, gl3, w_attn, w_rec, w_out, g_post):
    bsz, seq, _ = h3.shape

    def row(w):
        return pl.BlockSpec((None, TM, w), lambda b, i: (b, i, 0))

    def phased(d, w):
        return pl.BlockSpec((None, d, TM // d, w), lambda b, i: (b, 0, i, 0))

    return pl.pallas_call(
        _merge_kernel,
        name="merge",
        grid=(bsz, seq // TM),
        in_specs=[row(D_MODEL)] + [phased(d, GROUP_W) for d in DILATIONS]
                 + [phased(d, LANES) for d in DILATIONS] + [row(LRU_W), row(2 * D_MODEL),
                 _const_spec((GROUP_W, D_MODEL)), _const_spec((LRU_W, D_MODEL)),
                 _const_spec((D_MODEL, D_MODEL)), _const_spec((1, D_MODEL))],
        out_specs=row(D_MODEL),
        out_shape=jax.ShapeDtypeStruct((bsz, seq, D_MODEL), F32),
        scratch_shapes=[pltpu.VMEM((2 * HEADS_PER_GROUP, TM, LANES), F32),
                        pltpu.VMEM((2, TM, LANES), F32)],
        compiler_params=pltpu.CompilerParams(
            dimension_semantics=("arbitrary", "arbitrary"), vmem_limit_bytes=VMEM_LIMIT),
    )(h3, *o_list, *lse_list, rec3, gl3, w_attn, w_rec, w_out, g_post)


def kernel(x, ffn1_norm_pre, ffn1_norm_post, ffn1_w_gate, ffn1_w_up, ffn1_w_down, mix_norm_pre, mix_norm_post, w_in, rel_bias_table, conv_w, conv_b, lru_w_x, lru_b_x, lru_w_a, lru_b_a, lru_a_param, w_attn_branch, w_rec_branch, w_out, ffn2_norm_pre, ffn2_norm_post, ffn2_w_gate, ffn2_w_up, ffn2_w_down):
    bsz, seq, _ = x.shape
    n = bsz * seq
    depth = w_in.shape[0]
    bkt = jnp.asarray(_bucket_table())
    h = x.reshape(n, D_MODEL)
    for l in range(depth):
        gw = 3 * GROUP_W
        regroup = tuple((g * gw + j * GROUP_W, j * QKV_W + g * GROUP_W, GROUP_W)
                        for g in range(N_GROUPS) for j in range(3))
        jobs = [_CastJob(w_in[l], ((3 * QKV_W, regroup), (4 * LRU_W, ((0, 3 * QKV_W, 4 * LRU_W),)))),
                _plain_cast(lru_w_x[l].reshape(LRU_W, LRU_HD)), _plain_cast(lru_w_a[l].reshape(LRU_W, LRU_HD)),
                _plain_cast(w_attn_branch[l]), _plain_cast(w_rec_branch[l]), _plain_cast(w_out[l]),
                _plain_cast(ffn2_w_gate[l]), _plain_cast(ffn2_w_up[l]), _plain_cast(ffn2_w_down[l])]
        (h, u, u_d4, u_d16, w_qkv, w_xyg, wx, wa, w_attn, w_rec, w_o, w2_gate, w2_up, w2_down) = _ffn(
            h, ffn1_norm_pre[l][None], ffn1_norm_post[l][None],
            ffn1_w_gate[l].astype(BF16), ffn1_w_up[l].astype(BF16), ffn1_w_down[l].astype(BF16),
            g_next=mix_norm_pre[l][None], seq=seq, cast_jobs=jobs)
        h3 = h.reshape(bsz, seq, D_MODEL)
        u3 = u.reshape(bsz, seq, D_MODEL)

        rec, gl = _lru(u3, w_xyg, conv_w[l], conv_b[l][None],
                       wx.reshape(LRU_HEADS, LRU_HD, LRU_HD), lru_b_x[l].reshape(1, LRU_W),
                       wa.reshape(LRU_HEADS, LRU_HD, LRU_HD), lru_b_a[l].reshape(1, LRU_W),
                       lru_a_param[l][None])
        o_list, lse_list = [], []
        for g, u_g in enumerate((u3[:, None], u_d4, u_d16)):
            o_g, lse_g = _attn_group(g, u_g, w_qkv, rel_bias_table, bkt)
            o_list.append(o_g)
            lse_list.append(lse_g)
        h3 = _merge(h3, o_list, lse_list, rec, gl, w_attn, w_rec, w_o, mix_norm_post[l][None])

        h, = _ffn(h3.reshape(n, D_MODEL), ffn2_norm_pre[l][None], ffn2_norm_post[l][None],
                  w2_gate, w2_up, w2_down)
    return h.reshape(bsz, seq, D_MODEL)
```

```python
import functools
import math
from typing import NamedTuple

import numpy as np
import jax
import jax.numpy as jnp
from jax import lax
from jax.experimental import pallas as pl
from jax.experimental.pallas import tpu as pltpu

F32 = jnp.float32
BF16 = jnp.bfloat16

D_MODEL = 1024
HEAD_DIM = 128
HEADS_PER_GROUP = 4
DILATIONS = (1, 4, 16)
BAND = 128
N_GROUPS = 3
GROUP_W = HEADS_PER_GROUP * HEAD_DIM
QKV_W = N_GROUPS * GROUP_W
LRU_W = D_MODEL
LRU_HEADS = 4
LRU_HD = LRU_W // LRU_HEADS
CONV_W = 4
LRU_C = 8.0
D_FF = 2816
REL_BUCKETS = 32
REL_MAX_DISTANCE = 2048
EPS = 1e-6
TINY = 1e-30

LANES = 128
SUBLANES = 8
BF16_TILE_ROWS = 16
MAX_CAST_BLOCKS = 64
VMEM_LIMIT = 56 * 1024 * 1024

TM_FFN = 512
TM = 512
ATT_R = 2048
SEG = TM // SUBLANES
PITCH = SEG + SUBLANES
PIECE_W = 256
PROJ_ROWS = 512
PHASE_STEP = 4
assert DILATIONS == (1, PHASE_STEP, PHASE_STEP * PHASE_STEP)


def _rms(x, g):
    ms = jnp.mean(x * x, axis=-1, keepdims=True)
    return x * lax.rsqrt(ms + EPS) * g


def _sigmoid(x):
    return 1.0 / (1.0 + jnp.exp(-x))


def _gelu_tanh(x):
    c = math.sqrt(2.0 / math.pi)
    inner = x * (c + (c * 0.044715) * (x * x))
    return (0.5 * x) * (1.0 + jnp.tanh(inner))


def _const_spec(shape):
    nd = len(shape)
    return pl.BlockSpec(shape, lambda *_: (0,) * nd, pipeline_mode=pl.Buffered(1))


class _CastJob(NamedTuple):
    src: jax.Array
    outs: tuple

    @property
    def block_rows(self):
        rows = self.src.shape[0]
        br = BF16_TILE_ROWS
        while rows % br or rows // br > MAX_CAST_BLOCKS:
            br += BF16_TILE_ROWS
        return br


def _plain_cast(w):
    return _CastJob(w, ((w.shape[1], ((0, 0, w.shape[1]),)),))


def _ffn_kernel(emit_normed, jobs_pieces, x_ref, gpre_ref, gpost_ref, gnext_ref, wg_ref, wu_ref, wd_ref, *rest):
    n_jobs = len(jobs_pieces)
    n_normed = len(DILATIONS) if emit_normed else 0
    cast_in, o_ref = rest[:n_jobs], rest[n_jobs]
    normed_refs = rest[n_jobs + 1:n_jobs + 1 + n_normed]
    n_cast_out = sum(len(p) for p in jobs_pieces)
    cast_out = rest[n_jobs + 1 + n_normed:n_jobs + 1 + n_normed + n_cast_out]
    scratch = rest[n_jobs + 1 + n_normed + n_cast_out:]
    x = x_ref[...]
    xn = _rms(x, gpre_ref[...]).astype(BF16)
    g = jnp.dot(xn, wg_ref[...], preferred_element_type=F32)
    u = jnp.dot(xn, wu_ref[...], preferred_element_type=F32)
    a = (g * _sigmoid(g) * u).astype(BF16)
    f = jnp.dot(a, wd_ref[...], preferred_element_type=F32)
    h = x + 0.5 * _rms(f, gpost_ref[...])
    o_ref[...] = h
    if emit_normed:
        slab, slab4 = scratch
        u4_ref, u16_ref = normed_refs[1:]
        hn = _rms(h, gnext_ref[...])
        normed_refs[0][...] = hn.astype(BF16)
        r4 = TM_FFN // PHASE_STEP
        for c in range(D_MODEL // LANES):
            lanes = slice(c * LANES, (c + 1) * LANES)
            slab[c] = hn[:, lanes]
            for p in range(PHASE_STEP):
                part = slab[c, pl.ds(p, r4, stride=PHASE_STEP), :]
                u4_ref[p, :, lanes] = part.astype(BF16)
                slab4[c, p * r4:(p + 1) * r4, :] = part
            for p in range(PHASE_STEP):
                for q in range(PHASE_STEP):
                    u16_ref[q * PHASE_STEP + p, :, lanes] = (
                        slab4[c, pl.ds(p * r4 + q, r4 // PHASE_STEP, stride=PHASE_STEP), :].astype(BF16))
    outs = iter(cast_out)
    for src_ref, pieces_per_out in zip(cast_in, jobs_pieces):
        for pieces in pieces_per_out:
            dst_ref = next(outs)
            for dst_col, src_col, width in pieces:
                dst_ref[:, dst_col:dst_col + width] = src_ref[:, src_col:src_col + width].astype(BF16)


def _ffn(x2, g_pre, g_post, wg, wu, wd, g_next=None, seq=None, cast_jobs=()):
    n = x2.shape[0]
    steps = n // TM_FFN
    assert steps <= MAX_CAST_BLOCKS
    emit_normed = g_next is not None
    row = pl.BlockSpec((TM_FFN, D_MODEL), lambda i: (i, 0))
    in_specs = [row, _const_spec((1, D_MODEL)), _const_spec((1, D_MODEL)), _const_spec((1, D_MODEL)),
                _const_spec((D_MODEL, D_FF)), _const_spec((D_MODEL, D_FF)), _const_spec((D_FF, D_MODEL))]
    out_specs = [row]
    out_shape = [jax.ShapeDtypeStruct((n, D_MODEL), F32)]
    scratch_shapes = []
    if emit_normed:
        tiles = seq // TM_FFN
        out_specs.append(row)
        out_shape.append(jax.ShapeDtypeStruct((n, D_MODEL), BF16))
        for d in DILATIONS[1:]:
            out_specs.append(pl.BlockSpec((None, d, TM_FFN // d, D_MODEL),
                                          functools.partial(lambda i, t: (i // t, 0, i % t, 0), t=tiles)))
            out_shape.append(jax.ShapeDtypeStruct((n // seq, d, seq // d, D_MODEL), BF16))
        scratch_shapes += [pltpu.VMEM((D_MODEL // LANES, TM_FFN, LANES), F32)] * 2
    for job in cast_jobs:
        rows, br = job.src.shape[0], job.block_rows
        index_map = functools.partial(lambda i, last: (jnp.minimum(i, last), 0), last=rows // br - 1)
        in_specs.append(pl.BlockSpec((br, job.src.shape[1]), index_map))
        for cols, _ in job.outs:
            out_specs.append(pl.BlockSpec((br, cols), index_map))
            out_shape.append(jax.ShapeDtypeStruct((rows, cols), BF16))
    jobs_pieces = tuple(tuple(pieces for _, pieces in job.outs) for job in cast_jobs)
    return pl.pallas_call(
        functools.partial(_ffn_kernel, emit_normed, jobs_pieces),
        name="ffn",
        grid=(steps,),
        in_specs=in_specs,
        out_specs=out_specs,
        out_shape=out_shape,
        scratch_shapes=scratch_shapes,
        compiler_params=pltpu.CompilerParams(
            dimension_semantics=("arbitrary",), vmem_limit_bytes=VMEM_LIMIT),
    )(x2, g_pre, g_post, g_next if emit_normed else g_post, wg, wu, wd, *[job.src for job in cast_jobs])


def _lru_kernel(u_ref, wlru_ref, cw_ref, cb_ref, wx_ref, bx_ref, wa_ref, ba_ref, ap_ref,
                rec_ref, gl_ref, xbuf, a_s, b_s, gy_s, hcar, cin_s):
    @pl.when(pl.program_id(1) == 0)
    def _():
        xbuf[0:SUBLANES, :] = jnp.zeros((SUBLANES, LRU_W), F32)
        hcar[...] = jnp.zeros(hcar.shape, F32)

    slabs_per_head = LRU_HD // LANES
    ap = -ap_ref[...]
    c_softplus = LRU_C * (jnp.maximum(ap, 0.0) + jnp.log1p(jnp.exp(-jnp.abs(ap))))
    state = {}

    def proj_x(k):
        cols = slice(k * PIECE_W, (k + 1) * PIECE_W)
        xbuf[SUBLANES:SUBLANES + TM, cols] = jnp.dot(u_ref[0], wlru_ref[:, cols], preferred_element_type=F32)

    def proj_y(k):
        cols = slice(k * PIECE_W, (k + 1) * PIECE_W)
        yr = jnp.dot(u_ref[0], wlru_ref[:, LRU_W + k * PIECE_W:LRU_W + (k + 1) * PIECE_W],
                     preferred_element_type=F32)
        gy_s[:, cols] = _gelu_tanh(yr)

    def proj_gl(k):
        cols = slice(k * PIECE_W, (k + 1) * PIECE_W)
        gl_ref[0, :, cols] = jnp.dot(u_ref[0], wlru_ref[:, 2 * LRU_W + k * PIECE_W:2 * LRU_W + (k + 1) * PIECE_W],
                                     preferred_element_type=F32).astype(BF16)

    def gate_dots(i):
        xh = state["xcb", i]
        state["pre_x", i] = jnp.dot(xh, wx_ref[i], preferred_element_type=F32)
        state["pre_a", i] = jnp.dot(xh, wa_ref[i], preferred_element_type=F32)

    def conv(i):
        cols = slice(i * LRU_HD, (i + 1) * LRU_HD)
        xc = cb_ref[:, cols] + cw_ref[0:1, cols] * xbuf[SUBLANES - 3:SUBLANES - 3 + TM, cols]
        for j in range(1, CONV_W):
            off = SUBLANES - (CONV_W - 1) + j
            xc = xc + cw_ref[j:j + 1, cols] * xbuf[off:off + TM, cols]
        xbuf[0:SUBLANES, cols] = xbuf[TM:TM + SUBLANES, cols]
        state["xc", i] = xc
        state["xcb", i] = xc.astype(BF16)

    def gates(i):
        cols = slice(i * LRU_HD, (i + 1) * LRU_HD)
        gx = _sigmoid(state["pre_x", i] + bx_ref[:, cols])
        ga = _sigmoid(state["pre_a", i] + ba_ref[:, cols])
        nla = ga * c_softplus[:, cols]
        a = jnp.exp(-nla)
        t = jnp.tanh(nla)
        w = (t + t) / (1.0 + t)
        b = (w * lax.rsqrt(jnp.maximum(w, TINY))) * (gx * state["xc", i])
        for k in range(slabs_per_head):
            c = i * slabs_per_head + k
            for s in range(SUBLANES):
                a_s[c, s * PITCH:s * PITCH + SEG, :] = a[s * SEG:(s + 1) * SEG, k * LANES:(k + 1) * LANES]
                b_s[c, s * PITCH:s * PITCH + SEG, :] = b[s * SEG:(s + 1) * SEG, k * LANES:(k + 1) * LANES]

    def scan(i):
        slab_ids = range(i * slabs_per_head, (i + 1) * slabs_per_head)
        hs = {c: jnp.zeros((SUBLANES, LANES), F32) for c in slab_ids}
        cum = {c: jnp.ones((SUBLANES, LANES), F32) for c in slab_ids}
        for j in range(SEG):
            idx = pl.ds(j, SUBLANES, stride=PITCH)
            for c in slab_ids:
                aj = a_s[c, idx, :]
                hs[c] = aj * hs[c] + b_s[c, idx, :]
                cum[c] = aj * cum[c]
                a_s[c, idx, :] = cum[c]
                b_s[c, idx, :] = hs[c]
        row = lax.broadcasted_iota(jnp.int32, (SUBLANES, LANES), 0)
        for c in slab_ids:
            cols = slice(c * LANES, (c + 1) * LANES)
            h_in = jnp.broadcast_to(hcar[SUBLANES - 1:SUBLANES, cols], (SUBLANES, LANES))
            cin = h_in
            for _ in range(SUBLANES - 1):
                out = hs[c] + cum[c] * cin
                cin = jnp.where(row == 0, h_in, pltpu.roll(out, 1, 0))
            cin_s[c] = cin
            hcar[:, cols] = hs[c] + cum[c] * cin
        for c in slab_ids:
            cols = slice(c * LANES, (c + 1) * LANES)
            for s in range(SUBLANES):
                rows = slice(s * SEG, (s + 1) * SEG)
                prow = slice(s * PITCH, s * PITCH + SEG)
                h = b_s[c, prow, :] + a_s[c, prow, :] * cin_s[c, s:s + 1, :]
                rec_ref[0, rows, cols] = (h * gy_s[rows, cols]).astype(BF16)

    heads_per_piece = PIECE_W // LRU_HD
    filler = iter([functools.partial(proj_gl, k) for k in range(2 * D_MODEL // PIECE_W)])

    def fill():
        piece = next(filler, None)
        if piece is not None:
            piece()

    for k in range(LRU_W // PIECE_W):
        proj_x(k)
    for k in range(LRU_W // PIECE_W):
        proj_y(k)
        for i in range(k * heads_per_piece, (k + 1) * heads_per_piece):
            conv(i)
            fill()
            gate_dots(i)
    for i in range(LRU_HEADS):
        gates(i)
        fill()
        scan(i)
    for piece in filler:
        piece()


def _lru(u3, w_xyg, conv_w, conv_b, wx, bx, wa, ba, ap):
    bsz, seq, _ = u3.shape
    n_slab = LRU_W // LANES

    def row(w):
        return pl.BlockSpec((1, TM, w), lambda b, t: (b, t, 0))

    return pl.pallas_call(
        _lru_kernel,
        name="lru",
        grid=(bsz, seq // TM),
        in_specs=[row(D_MODEL), _const_spec((D_MODEL, 4 * LRU_W)),
                  _const_spec((CONV_W, LRU_W)), _const_spec((1, LRU_W)),
                  _const_spec((LRU_HEADS, LRU_HD, LRU_HD)), _const_spec((1, LRU_W)),
                  _const_spec((LRU_HEADS, LRU_HD, LRU_HD)), _const_spec((1, LRU_W)),
                  _const_spec((1, LRU_W))],
        out_specs=[row(LRU_W), row(2 * D_MODEL)],
        out_shape=[jax.ShapeDtypeStruct((bsz, seq, LRU_W), BF16),
                   jax.ShapeDtypeStruct((bsz, seq, 2 * D_MODEL), BF16)],
        scratch_shapes=[pltpu.VMEM((TM + SUBLANES, LRU_W), F32),
                        pltpu.VMEM((n_slab, SUBLANES * PITCH, LANES), F32),
                        pltpu.VMEM((n_slab, SUBLANES * PITCH, LANES), F32),
                        pltpu.VMEM((TM, LRU_W), F32),
                        pltpu.VMEM((SUBLANES, LRU_W), F32),
                        pltpu.VMEM((n_slab, SUBLANES, LANES), F32)],
        compiler_params=pltpu.CompilerParams(
            dimension_semantics=("arbitrary", "arbitrary"), vmem_limit_bytes=VMEM_LIMIT),
    )(u3, w_xyg, conv_w, conv_b, wx, bx, wa, ba, ap)


def _bucket_table():
    qi = np.arange(BAND)[:, None]
    kj = np.arange(BAND)[None, :]
    max_exact = REL_BUCKETS // 2
    out = np.zeros((N_GROUPS, 2, BAND, BAND), np.int32)
    for g, d in enumerate(DILATIONS):
        for half in range(2):
            steps = qi + BAND - kj if half == 0 else qi - kj
            valid = (steps >= 0) & (steps <= BAND)
            dist = np.maximum(steps, 0) * d
            nf = np.maximum(dist, 1).astype(np.float32)
            large = max_exact + (np.log(nf / np.float32(max_exact))
                                 / np.float32(math.log(REL_MAX_DISTANCE / max_exact))
                                 * np.float32(REL_BUCKETS - max_exact)).astype(np.int32)
            large = np.minimum(large, REL_BUCKETS - 1)
            bucket = np.where(dist < max_exact, dist, large)
            out[g, half] = np.where(valid, bucket, -1)
    return out


def _attn_kernel(group, pp, rows, tab_ref, bkt_ref, u_ref, w_ref, o_ref, st_ref,
                 qbuf, kbuf, vbuf, bias_s):
    first = (pl.program_id(0) == 0) & (pl.program_id(1) == 0) & (pl.program_id(2) == 0)
    c = pl.program_id(2)
    log2e = math.log2(math.e)
    vw = 2 * HEAD_DIM

    @pl.when(first)
    def _():
        for h in range(HEADS_PER_GROUP):
            for half in range(2):
                bk = bkt_ref[half]
                bias = jnp.full((BAND, BAND), -jnp.inf, F32)
                for n in range(REL_BUCKETS):
                    bias = jnp.where(bk == n, tab_ref[n, group * HEADS_PER_GROUP + h] * log2e, bias)
                bias_s[h, :, half * BAND:(half + 1) * BAND] = bias
            for ph in range(pp):
                vbuf[ph, :, h * vw + HEAD_DIM:(h + 1) * vw] = jnp.ones((BAND + rows, HEAD_DIM), BF16)

    @pl.when(c == 0)
    def _():
        for ph in range(pp):
            kbuf[ph, 0:BAND, :] = jnp.zeros((BAND, GROUP_W), BF16)
            for h in range(HEADS_PER_GROUP):
                vbuf[ph, 0:BAND, h * vw:h * vw + HEAD_DIM] = jnp.zeros((BAND, HEAD_DIM), BF16)

    lane2 = lax.broadcasted_iota(jnp.int32, (1, 2 * BAND), 1)
    pen = jnp.where((lane2 < BAND) & (c == 0), -jnp.inf, 0.0).astype(F32)
    qk_scale = log2e / math.sqrt(HEAD_DIM)
    nt = (((1,), (1,)), ((), ()))
    lane = lax.broadcasted_iota(jnp.int32, (BAND, LANES), 1)

    def project(ph, r0):
        qkv = jnp.dot(u_ref[ph, r0:r0 + PROJ_ROWS, :], w_ref[...], preferred_element_type=F32).astype(BF16)
        qbuf[ph, r0:r0 + PROJ_ROWS, :] = qkv[:, :GROUP_W]
        kbuf[ph, BAND + r0:BAND + r0 + PROJ_ROWS, :] = qkv[:, GROUP_W:2 * GROUP_W]
        for h in range(HEADS_PER_GROUP):
            vbuf[ph, BAND + r0:BAND + r0 + PROJ_ROWS, h * vw:h * vw + HEAD_DIM] = (
                qkv[:, 2 * GROUP_W + h * HEAD_DIM:2 * GROUP_W + (h + 1) * HEAD_DIM])

    def units_of(r0):
        return [(n, h) for n in range(r0 // BAND, (r0 + PROJ_ROWS) // BAND) for h in range(HEADS_PER_GROUP)]

    def attend_scores(ph, r0):
        scores = []
        for n, h in units_of(r0):
            cols = slice(h * HEAD_DIM, (h + 1) * HEAD_DIM)
            q = qbuf[ph, n * BAND:(n + 1) * BAND, cols]
            kk = kbuf[ph, n * BAND:(n + 2) * BAND, cols]
            s = lax.dot_general(q, kk, nt, preferred_element_type=F32) * qk_scale + bias_s[h]
            if n == 0:
                s = s + pen
            scores.append(s)
        return scores

    def attend_finish(ph, r0, scores):
        units = units_of(r0)
        maxes = [jnp.max(s, axis=-1, keepdims=True) for s in scores]
        probs = [jnp.exp2(s - m).astype(BF16) for s, m in zip(scores, maxes)]
        outs = [jnp.dot(p, vbuf[ph, n * BAND:(n + 2) * BAND, h * vw:(h + 1) * vw], preferred_element_type=F32)
                for (n, h), p in zip(units, probs)]
        stats = None
        for (n, h), o, m in zip(units, outs, maxes):
            o_ref[ph, n * BAND:(n + 1) * BAND, h * HEAD_DIM:(h + 1) * HEAD_DIM] = o[:, :HEAD_DIM].astype(BF16)
            den = o[:, HEAD_DIM:]
            stats = jnp.broadcast_to(m, (BAND, LANES)) if h == 0 else jnp.where(lane == h, m, stats)
            stats = jnp.where(lane == HEADS_PER_GROUP + h, den, stats)
            if h == HEADS_PER_GROUP - 1:
                st_ref[ph, n * BAND:(n + 1) * BAND, :] = stats

    sub_blocks = [(ph, r0) for ph in range(pp) for r0 in range(0, rows, PROJ_ROWS)]
    for sb in sub_blocks[:2]:
        project(*sb)
    for i, sb in enumerate(sub_blocks):
        scores = attend_scores(*sb)
        if i + 2 < len(sub_blocks):
            project(*sub_blocks[i + 2])
        attend_finish(*sb, scores)

    for ph in range(pp):
        kbuf[ph, 0:BAND, :] = kbuf[ph, rows:rows + BAND, :]
        vbuf[ph, 0:BAND, :] = vbuf[ph, rows:rows + BAND, :]


def _attn_group(group, u_g, w_qkv, table, bkt):
    bsz, d, sub, _ = u_g.shape
    rows = min(ATT_R, sub)
    pp = min(d, ATT_R // rows)
    assert sub % rows == 0 and d % pp == 0 and (pp == 1 or rows == sub) and rows % PROJ_ROWS == 0

    def spec(width):
        return pl.BlockSpec((None, pp, rows, width), lambda b, p, c: (b, p, c, 0))

    return pl.pallas_call(
        functools.partial(_attn_kernel, group, pp, rows),
        name=f"attn{group}",
        grid=(bsz, d // pp, sub // rows),
        in_specs=[pl.BlockSpec(memory_space=pltpu.SMEM),
                  pl.BlockSpec((None, 2, BAND, BAND), lambda b, p, c: (group, 0, 0, 0)),
                  spec(D_MODEL),
                  pl.BlockSpec((D_MODEL, 3 * GROUP_W), lambda b, p, c: (0, group), pipeline_mode=pl.Buffered(1))],
        out_specs=[spec(GROUP_W), spec(LANES)],
        out_shape=[jax.ShapeDtypeStruct((bsz, d, sub, GROUP_W), BF16),
                   jax.ShapeDtypeStruct((bsz, d, sub, LANES), F32)],
        scratch_shapes=[pltpu.VMEM((pp, rows, GROUP_W), BF16),
                        pltpu.VMEM((pp, BAND + rows, GROUP_W), BF16),
                        pltpu.VMEM((pp, BAND + rows, 2 * GROUP_W), BF16),
                        pltpu.VMEM((HEADS_PER_GROUP, BAND, 2 * BAND), F32)],
        compiler_params=pltpu.CompilerParams(
            dimension_semantics=("arbitrary", "arbitrary", "arbitrary"),
            vmem_limit_bytes=VMEM_LIMIT),
    )(table, bkt, u_g, w_qkv)


def _merge_kernel(h_ref, o0_ref, o1_ref, o2_ref, l0_ref, l1_ref, l2_ref, rec_ref, gl_ref,
                  wat_ref, wrec_ref, wout_ref, gpost_ref, out_ref, osc, lsc):
    rec_d = jnp.dot(rec_ref[...], wrec_ref[...], preferred_element_type=F32)
    for gi, o_ref, l_ref in ((1, o1_ref, l1_ref), (2, o2_ref, l2_ref)):
        d = DILATIONS[gi]
        for p in range(d):
            idx = pl.ds(p, TM // d, stride=d)
            lsc[gi - 1, idx, :] = l_ref[p]
            for h in range(HEADS_PER_GROUP):
                osc[(gi - 1) * HEADS_PER_GROUP + h, idx, :] = (
                    o_ref[p, :, h * HEAD_DIM:(h + 1) * HEAD_DIM].astype(F32))
    st = (l0_ref[0], lsc[0], lsc[1])
    mx = jnp.maximum(jnp.maximum(st[0], st[1]), st[2])
    e = [jnp.exp2(x - mx) for x in st]
    den = [pltpu.roll(x, LANES - HEADS_PER_GROUP, 1) for x in st]
    inv = 1.0 / (e[0] * den[0] + e[1] * den[1] + e[2] * den[2])
    w0, w1, w2 = e[0] * inv, e[1] * inv, e[2] * inv
    parts = []
    for h in range(HEADS_PER_GROUP):
        cols = slice(h * HEAD_DIM, (h + 1) * HEAD_DIM)
        parts.append(w0[:, h:h + 1] * o0_ref[0, :, cols].astype(F32)
                     + w1[:, h:h + 1] * osc[h]
                     + w2[:, h:h + 1] * osc[HEADS_PER_GROUP + h])
    attn = jnp.concatenate(parts, axis=-1).astype(BF16)
    attn_d = jnp.dot(attn, wat_ref[...], preferred_element_type=F32)
    gl = gl_ref[...].astype(F32)
    merged = _sigmoid(gl[:, :D_MODEL]) * attn_d + _sigmoid(gl[:, D_MODEL:]) * rec_d
    mo = jnp.dot(merged.astype(BF16), wout_ref[...], preferred_element_type=F32)
    out_ref[...] = h_ref[...] + _rms(mo, gpost_ref[...])


def _merge(h3, o_list, lse_list, rec3, gl3, w_attn, w_rec, w_out, g_post):
    bsz, seq, _ = h3.shape

    def row(w):
        return pl.BlockSpec((None, TM, w), lambda b, i: (b, i, 0))

    def phased(d, w):
        return pl.BlockSpec((None, d, TM // d, w), lambda b, i: (b, 0, i, 0))

    return pl.pallas_call(
        _merge_kernel,
        name="merge",
        grid=(bsz, seq // TM),
        in_specs=[row(D_MODEL)] + [phased(d, GROUP_W) for d in DILATIONS]
                 + [phased(d, LANES) for d in DILATIONS] + [row(LRU_W), row(2 * D_MODEL),
                 _const_spec((GROUP_W, D_MODEL)), _const_spec((LRU_W, D_MODEL)),
                 _const_spec((D_MODEL, D_MODEL)), _const_spec((1, D_MODEL))],
        out_specs=row(D_MODEL),
        out_shape=jax.ShapeDtypeStruct((bsz, seq, D_MODEL), F32),
        scratch_shapes=[pltpu.VMEM((2 * HEADS_PER_GROUP, TM, LANES), F32),
                        pltpu.VMEM((2, TM, LANES), F32)],
        compiler_params=pltpu.CompilerParams(
            dimension_semantics=("arbitrary", "arbitrary"), vmem_limit_bytes=VMEM_LIMIT),
    )(h3, *o_list, *lse_list, rec3, gl3, w_attn, w_rec, w_out, g_post)


def kernel(x, ffn1_norm_pre, ffn1_norm_post, ffn1_w_gate, ffn1_w_up, ffn1_w_down, mix_norm_pre, mix_norm_post, w_in, rel_bias_table, conv_w, conv_b, lru_w_x, lru_b_x, lru_w_a, lru_b_a, lru_a_param, w_attn_branch, w_rec_branch, w_out, ffn2_norm_pre, ffn2_norm_post, ffn2_w_gate, ffn2_w_up, ffn2_w_down):
    bsz, seq, _ = x.shape
    n = bsz * seq
    depth = w_in.shape[0]
    bkt = jnp.asarray(_bucket_table())
    h = x.reshape(n, D_MODEL)
    for l in range(depth):
        gw = 3 * GROUP_W
        regroup = tuple((g * gw + j * GROUP_W, j * QKV_W + g * GROUP_W, GROUP_W)
                        for g in range(N_GROUPS) for j in range(3))
        jobs = [_CastJob(w_in[l], ((3 * QKV_W, regroup), (4 * LRU_W, ((0, 3 * QKV_W, 4 * LRU_W),)))),
                _plain_cast(lru_w_x[l].reshape(LRU_W, LRU_HD)), _plain_cast(lru_w_a[l].reshape(LRU_W, LRU_HD)),
                _plain_cast(w_attn_branch[l]), _plain_cast(w_rec_branch[l]), _plain_cast(w_out[l]),
                _plain_cast(ffn2_w_gate[l]), _plain_cast(ffn2_w_up[l]), _plain_cast(ffn2_w_down[l])]
        (h, u, u_d4, u_d16, w_qkv, w_xyg, wx, wa, w_attn, w_rec, w_o, w2_gate, w2_up, w2_down) = _ffn(
            h, ffn1_norm_pre[l][None], ffn1_norm_post[l][None],
            ffn1_w_gate[l].astype(BF16), ffn1_w_up[l].astype(BF16), ffn1_w_down[l].astype(BF16),
            g_next=mix_norm_pre[l][None], seq=seq, cast_jobs=jobs)
        h3 = h.reshape(bsz, seq, D_MODEL)
        u3 = u.reshape(bsz, seq, D_MODEL)

        rec, gl = _lru(u3, w_xyg, conv_w[l], conv_b[l][None],
                       wx.reshape(LRU_HEADS, LRU_HD, LRU_HD), lru_b_x[l].reshape(1, LRU_W),
                       wa.reshape(LRU_HEADS, LRU_HD, LRU_HD), lru_b_a[l].reshape(1, LRU_W),
                       lru_a_param[l][None])
        o_list, lse_list = [], []
        for g, u_g in enumerate((u3[:, None], u_d4, u_d16)):
            o_g, lse_g = _attn_group(g, u_g, w_qkv, rel_bias_table, bkt)
            o_list.append(o_g)
            lse_list.append(lse_g)
        h3 = _merge(h3, o_list, lse_list, rec, gl, w_attn, w_rec, w_o, mix_norm_post[l][None])

        h, = _ffn(h3.reshape(n, D_MODEL), ffn2_norm_pre[l][None], ffn2_norm_post[l][None],
                  w2_gate, w2_up, w2_down)
    return h.reshape(bsz, seq, D_MODEL)
```

```python
import functools
import math
from typing import NamedTuple

import numpy as np
import jax
import jax.numpy as jnp
from jax import lax
from jax.experimental import pallas as pl
from jax.experimental.pallas import tpu as pltpu

F32 = jnp.float32
BF16 = jnp.bfloat16

D_MODEL = 1024
HEAD_DIM = 128
HEADS_PER_GROUP = 4
DILATIONS = (1, 4, 16)
BAND = 128
N_GROUPS = 3
GROUP_W = HEADS_PER_GROUP * HEAD_DIM
QKV_W = N_GROUPS * GROUP_W
LRU_W = D_MODEL
LRU_HEADS = 4
LRU_HD = LRU_W // LRU_HEADS
CONV_W = 4
LRU_C = 8.0
D_FF = 2816
REL_BUCKETS = 32
REL_MAX_DISTANCE = 2048
EPS = 1e-6
TINY = 1e-30

LANES = 128
SUBLANES = 8
BF16_TILE_ROWS = 16
MAX_CAST_BLOCKS = 64
VMEM_LIMIT = 56 * 1024 * 1024

TM_FFN = 512
TM = 512
ATT_R = 2048
SEG = TM // SUBLANES
PITCH = SEG + SUBLANES
PIECE_W = 256
PROJ_ROWS = 512
PHASE_STEP = 4
assert DILATIONS == (1, PHASE_STEP, PHASE_STEP * PHASE_STEP)


def _rms(x, g):
    ms = jnp.mean(x * x, axis=-1, keepdims=True)
    return x * lax.rsqrt(ms + EPS) * g


def _sigmoid(x):
    return 1.0 / (1.0 + jnp.exp(-x))


def _gelu_tanh(x):
    c = math.sqrt(2.0 / math.pi)
    inner = x * (c + (c * 0.044715) * (x * x))
    return (0.5 * x) * (1.0 + jnp.tanh(inner))


def _const_spec(shape):
    nd = len(shape)
    return pl.BlockSpec(shape, lambda *_: (0,) * nd, pipeline_mode=pl.Buffered(1))


class _CastJob(NamedTuple):
    src: jax.Array
    outs: tuple

    @property
    def block_rows(self):
        rows = self.src.shape[0]
        br = BF16_TILE_ROWS
        while rows % br or rows // br > MAX_CAST_BLOCKS:
            br += BF16_TILE_ROWS
        return br


def _plain_cast(w):
    return _CastJob(w, ((w.shape[1], ((0, 0, w.shape[1]),)),))


def _ffn_kernel(emit_normed, jobs_pieces, x_ref, gpre_ref, gpost_ref, wg_ref, wu_ref, wd_ref, *rest):
    n_jobs = len(jobs_pieces)
    n_normed = len(DILATIONS) if emit_normed else 0
    if emit_normed:
        gnext_ref, rest = rest[0], rest[1:]
    cast_in, o_ref = rest[:n_jobs], rest[n_jobs]
    normed_refs = rest[n_jobs + 1:n_jobs + 1 + n_normed]
    n_cast_out = sum(len(p) for p in jobs_pieces)
    cast_out = rest[n_jobs + 1 + n_normed:n_jobs + 1 + n_normed + n_cast_out]
    scratch = rest[n_jobs + 1 + n_normed + n_cast_out:]
    x = x_ref[...]
    xn = _rms(x, gpre_ref[...]).astype(BF16)
    g = jnp.dot(xn, wg_ref[...], preferred_element_type=F32)
    u = jnp.dot(xn, wu_ref[...], preferred_element_type=F32)
    a = (g * _sigmoid(g) * u).astype(BF16)
    f = jnp.dot(a, wd_ref[...], preferred_element_type=F32)
    h = x + 0.5 * _rms(f, gpost_ref[...])
    o_ref[...] = h
    if emit_normed:
        slab, slab4 = scratch
        u4_ref, u16_ref = normed_refs[1:]
        hn = _rms(h, gnext_ref[...])
        normed_refs[0][...] = hn.astype(BF16)
        r4 = TM_FFN // PHASE_STEP
        for c in range(D_MODEL // LANES):
            lanes = slice(c * LANES, (c + 1) * LANES)
            slab[c] = hn[:, lanes]
            for p in range(PHASE_STEP):
                part = slab[c, pl.ds(p, r4, stride=PHASE_STEP), :]
                u4_ref[p, :, lanes] = part.astype(BF16)
                slab4[c, p * r4:(p + 1) * r4, :] = part
            for p in range(PHASE_STEP):
                for q in range(PHASE_STEP):
                    u16_ref[q * PHASE_STEP + p, :, lanes] = (
                        slab4[c, pl.ds(p * r4 + q, r4 // PHASE_STEP, stride=PHASE_STEP), :].astype(BF16))
    outs = iter(cast_out)
    for src_ref, pieces_per_out in zip(cast_in, jobs_pieces):
        for pieces in pieces_per_out:
            dst_ref = next(outs)
            for dst_col, src_col, width in pieces:
                dst_ref[:, dst_col:dst_col + width] = src_ref[:, src_col:src_col + width].astype(BF16)


def _ffn(x2, g_pre, g_post, wg, wu, wd, g_next=None, seq=None, cast_jobs=()):
    n = x2.shape[0]
    steps = n // TM_FFN
    assert steps <= MAX_CAST_BLOCKS
    emit_normed = g_next is not None
    row = pl.BlockSpec((TM_FFN, D_MODEL), lambda i: (i, 0))
    in_specs = [row, _const_spec((1, D_MODEL)), _const_spec((1, D_MODEL)),
                _const_spec((D_MODEL, D_FF)), _const_spec((D_MODEL, D_FF)), _const_spec((D_FF, D_MODEL))]
    operands = [x2, g_pre, g_post, wg, wu, wd]
    out_specs = [row]
    out_shape = [jax.ShapeDtypeStruct((n, D_MODEL), F32)]
    scratch_shapes = []
    if emit_normed:
        tiles = seq // TM_FFN
        in_specs.append(_const_spec((1, D_MODEL)))
        operands.append(g_next)
        out_specs.append(row)
        out_shape.append(jax.ShapeDtypeStruct((n, D_MODEL), BF16))
        for d in DILATIONS[1:]:
            out_specs.append(pl.BlockSpec((None, d, TM_FFN // d, D_MODEL),
                                          functools.partial(lambda i, t: (i // t, 0, i % t, 0), t=tiles)))
            out_shape.append(jax.ShapeDtypeStruct((n // seq, d, seq // d, D_MODEL), BF16))
        scratch_shapes += [pltpu.VMEM((D_MODEL // LANES, TM_FFN, LANES), F32)] * 2
    for job in cast_jobs:
        rows, br = job.src.shape[0], job.block_rows
        index_map = functools.partial(lambda i, last: (jnp.minimum(i, last), 0), last=rows // br - 1)
        in_specs.append(pl.BlockSpec((br, job.src.shape[1]), index_map))
        operands.append(job.src)
        for cols, _ in job.outs:
            out_specs.append(pl.BlockSpec((br, cols), index_map))
            out_shape.append(jax.ShapeDtypeStruct((rows, cols), BF16))
    jobs_pieces = tuple(tuple(pieces for _, pieces in job.outs) for job in cast_jobs)
    return pl.pallas_call(
        functools.partial(_ffn_kernel, emit_normed, jobs_pieces),
        name="ffn",
        grid=(steps,),
        in_specs=in_specs,
        out_specs=out_specs,
        out_shape=out_shape,
        scratch_shapes=scratch_shapes,
        compiler_params=pltpu.CompilerParams(
            dimension_semantics=("arbitrary",), vmem_limit_bytes=VMEM_LIMIT),
    )(*operands)


def _lru_kernel(u_ref, wlru_ref, cw_ref, cb_ref, wx_ref, bx_ref, wa_ref, ba_ref, ap_ref,
                rec_ref, gl_ref, xbuf, a_s, b_s, gy_s, hcar, cin_s):
    @pl.when(pl.program_id(1) == 0)
    def _():
        xbuf[0:SUBLANES, :] = jnp.zeros((SUBLANES, LRU_W), F32)
        hcar[...] = jnp.zeros(hcar.shape, F32)

    slabs_per_head = LRU_HD // LANES
    ap = -ap_ref[...]
    c_softplus = LRU_C * (jnp.maximum(ap, 0.0) + jnp.log1p(jnp.exp(-jnp.abs(ap))))
    state = {}

    def proj_x(k):
        cols = slice(k * PIECE_W, (k + 1) * PIECE_W)
        xbuf[SUBLANES:SUBLANES + TM, cols] = jnp.dot(u_ref[0], wlru_ref[:, cols], preferred_element_type=F32)

    def proj_y(k):
        cols = slice(k * PIECE_W, (k + 1) * PIECE_W)
        yr = jnp.dot(u_ref[0], wlru_ref[:, LRU_W + k * PIECE_W:LRU_W + (k + 1) * PIECE_W],
                     preferred_element_type=F32)
        gy_s[:, cols] = _gelu_tanh(yr)

    def proj_gl(k):
        cols = slice(k * PIECE_W, (k + 1) * PIECE_W)
        gl_ref[0, :, cols] = jnp.dot(u_ref[0], wlru_ref[:, 2 * LRU_W + k * PIECE_W:2 * LRU_W + (k + 1) * PIECE_W],
                                     preferred_element_type=F32).astype(BF16)

    def gate_dots(i):
        xh = state["xcb", i]
        state["pre_x", i] = jnp.dot(xh, wx_ref[i], preferred_element_type=F32)
        state["pre_a", i] = jnp.dot(xh, wa_ref[i], preferred_element_type=F32)

    def conv(i):
        cols = slice(i * LRU_HD, (i + 1) * LRU_HD)
        xc = cb_ref[:, cols] + cw_ref[0:1, cols] * xbuf[SUBLANES - 3:SUBLANES - 3 + TM, cols]
        for j in range(1, CONV_W):
            off = SUBLANES - (CONV_W - 1) + j
            xc = xc + cw_ref[j:j + 1, cols] * xbuf[off:off + TM, cols]
        xbuf[0:SUBLANES, cols] = xbuf[TM:TM + SUBLANES, cols]
        state["xc", i] = xc
        state["xcb", i] = xc.astype(BF16)

    def gates(i):
        cols = slice(i * LRU_HD, (i + 1) * LRU_HD)
        gx = _sigmoid(state["pre_x", i] + bx_ref[:, cols])
        ga = _sigmoid(state["pre_a", i] + ba_ref[:, cols])
        nla = ga * c_softplus[:, cols]
        a = jnp.exp(-nla)
        t = jnp.tanh(nla)
        w = (t + t) / (1.0 + t)
        b = (w * lax.rsqrt(jnp.maximum(w, TINY))) * (gx * state["xc", i])
        for k in range(slabs_per_head):
            c = i * slabs_per_head + k
            for s in range(SUBLANES):
                a_s[c, s * PITCH:s * PITCH + SEG, :] = a[s * SEG:(s + 1) * SEG, k * LANES:(k + 1) * LANES]
                b_s[c, s * PITCH:s * PITCH + SEG, :] = b[s * SEG:(s + 1) * SEG, k * LANES:(k + 1) * LANES]

    def scan(i):
        slab_ids = range(i * slabs_per_head, (i + 1) * slabs_per_head)
        hs = {c: jnp.zeros((SUBLANES, LANES), F32) for c in slab_ids}
        cum = {c: jnp.ones((SUBLANES, LANES), F32) for c in slab_ids}
        for j in range(SEG):
            idx = pl.ds(j, SUBLANES, stride=PITCH)
            for c in slab_ids:
                aj = a_s[c, idx, :]
                hs[c] = aj * hs[c] + b_s[c, idx, :]
                cum[c] = aj * cum[c]
                a_s[c, idx, :] = cum[c]
                b_s[c, idx, :] = hs[c]
        row = lax.broadcasted_iota(jnp.int32, (SUBLANES, LANES), 0)
        for c in slab_ids:
            cols = slice(c * LANES, (c + 1) * LANES)
            h_in = jnp.broadcast_to(hcar[SUBLANES - 1:SUBLANES, cols], (SUBLANES, LANES))
            cin = h_in
            for _ in range(SUBLANES - 1):
                out = hs[c] + cum[c] * cin
                cin = jnp.where(row == 0, h_in, pltpu.roll(out, 1, 0))
            cin_s[c] = cin
            hcar[:, cols] = hs[c] + cum[c] * cin
        for c in slab_ids:
            cols = slice(c * LANES, (c + 1) * LANES)
            for s in range(SUBLANES):
                rows = slice(s * SEG, (s + 1) * SEG)
                prow = slice(s * PITCH, s * PITCH + SEG)
                h = b_s[c, prow, :] + a_s[c, prow, :] * cin_s[c, s:s + 1, :]
                rec_ref[0, rows, cols] = (h * gy_s[rows, cols]).astype(BF16)

    heads_per_piece = PIECE_W // LRU_HD
    filler = iter([functools.partial(proj_gl, k) for k in range(2 * D_MODEL // PIECE_W)])

    def fill():
        piece = next(filler, None)
        if piece is not None:
            piece()

    for k in range(LRU_W // PIECE_W):
        proj_x(k)
    for k in range(LRU_W // PIECE_W):
        proj_y(k)
        for i in range(k * heads_per_piece, (k + 1) * heads_per_piece):
            conv(i)
            fill()
            gate_dots(i)
    for i in range(LRU_HEADS):
        gates(i)
        fill()
        scan(i)
    for piece in filler:
        piece()


def _lru(u3, w_xyg, conv_w, conv_b, wx, bx, wa, ba, ap):
    bsz, seq, _ = u3.shape
    n_slab = LRU_W // LANES

    def row(w):
        return pl.BlockSpec((1, TM, w), lambda b, t: (b, t, 0))

    return pl.pallas_call(
        _lru_kernel,
        name="lru",
        grid=(bsz, seq // TM),
        in_specs=[row(D_MODEL), _const_spec((D_MODEL, 4 * LRU_W)),
                  _const_spec((CONV_W, LRU_W)), _const_spec((1, LRU_W)),
                  _const_spec((LRU_HEADS, LRU_HD, LRU_HD)), _const_spec((1, LRU_W)),
                  _const_spec((LRU_HEADS, LRU_HD, LRU_HD)), _const_spec((1, LRU_W)),
                  _const_spec((1, LRU_W))],
        out_specs=[row(LRU_W), row(2 * D_MODEL)],
        out_shape=[jax.ShapeDtypeStruct((bsz, seq, LRU_W), BF16),
                   jax.ShapeDtypeStruct((bsz, seq, 2 * D_MODEL), BF16)],
        scratch_shapes=[pltpu.VMEM((TM + SUBLANES, LRU_W), F32),
                        pltpu.VMEM((n_slab, SUBLANES * PITCH, LANES), F32),
                        pltpu.VMEM((n_slab, SUBLANES * PITCH, LANES), F32),
                        pltpu.VMEM((TM, LRU_W), F32),
                        pltpu.VMEM((SUBLANES, LRU_W), F32),
                        pltpu.VMEM((n_slab, SUBLANES, LANES), F32)],
        compiler_params=pltpu.CompilerParams(
            dimension_semantics=("arbitrary", "arbitrary"), vmem_limit_bytes=VMEM_LIMIT),
    )(u3, w_xyg, conv_w, conv_b, wx, bx, wa, ba, ap)


def _bucket_table():
    qi = np.arange(BAND)[:, None]
    kj = np.arange(BAND)[None, :]
    max_exact = REL_BUCKETS // 2
    out = np.zeros((N_GROUPS, 2, BAND, BAND), np.int32)
    for g, d in enumerate(DILATIONS):
        for half in range(2):
            steps = qi + BAND - kj if half == 0 else qi - kj
            valid = (steps >= 0) & (steps <= BAND)
            dist = np.maximum(steps, 0) * d
            nf = np.maximum(dist, 1).astype(np.float32)
            large = max_exact + (np.log(nf / np.float32(max_exact))
                                 / np.float32(math.log(REL_MAX_DISTANCE / max_exact))
                                 * np.float32(REL_BUCKETS - max_exact)).astype(np.int32)
            large = np.minimum(large, REL_BUCKETS - 1)
            bucket = np.where(dist < max_exact, dist, large)
            out[g, half] = np.where(valid, bucket, -1)
    return out


def _attn_kernel(group, pp, rows, tab_ref, bkt_ref, u_ref, w_ref, o_ref, st_ref,
                 qbuf, kbuf, vbuf, bias_s):
    first = (pl.program_id(0) == 0) & (pl.program_id(1) == 0) & (pl.program_id(2) == 0)
    c = pl.program_id(2)
    log2e = math.log2(math.e)
    vw = 2 * HEAD_DIM

    @pl.when(first)
    def _():
        for h in range(HEADS_PER_GROUP):
            for half in range(2):
                bk = bkt_ref[half]
                bias = jnp.full((BAND, BAND), -jnp.inf, F32)
                for n in range(REL_BUCKETS):
                    bias = jnp.where(bk == n, tab_ref[n, group * HEADS_PER_GROUP + h] * log2e, bias)
                bias_s[h, :, half * BAND:(half + 1) * BAND] = bias
            for ph in range(pp):
                vbuf[ph, :, h * vw + HEAD_DIM:(h + 1) * vw] = jnp.ones((BAND + rows, HEAD_DIM), BF16)

    @pl.when(c == 0)
    def _():
        for ph in range(pp):
            kbuf[ph, 0:BAND, :] = jnp.zeros((BAND, GROUP_W), BF16)
            for h in range(HEADS_PER_GROUP):
                vbuf[ph, 0:BAND, h * vw:h * vw + HEAD_DIM] = jnp.zeros((BAND, HEAD_DIM), BF16)

    lane2 = lax.broadcasted_iota(jnp.int32, (1, 2 * BAND), 1)
    pen = jnp.where((lane2 < BAND) & (c == 0), -jnp.inf, 0.0).astype(F32)
    qk_scale = log2e / math.sqrt(HEAD_DIM)
    nt = (((1,), (1,)), ((), ()))
    lane = lax.broadcasted_iota(jnp.int32, (BAND, LANES), 1)

    def project(ph, r0):
        qkv = jnp.dot(u_ref[ph, r0:r0 + PROJ_ROWS, :], w_ref[...], preferred_element_type=F32).astype(BF16)
        qbuf[ph, r0:r0 + PROJ_ROWS, :] = qkv[:, :GROUP_W]
        kbuf[ph, BAND + r0:BAND + r0 + PROJ_ROWS, :] = qkv[:, GROUP_W:2 * GROUP_W]
        for h in range(HEADS_PER_GROUP):
            vbuf[ph, BAND + r0:BAND + r0 + PROJ_ROWS, h * vw:h * vw + HEAD_DIM] = (
                qkv[:, 2 * GROUP_W + h * HEAD_DIM:2 * GROUP_W + (h + 1) * HEAD_DIM])

    def units_of(r0):
        return [(n, h) for n in range(r0 // BAND, (r0 + PROJ_ROWS) // BAND) for h in range(HEADS_PER_GROUP)]

    def attend_scores(ph, r0):
        scores = []
        for n, h in units_of(r0):
            cols = slice(h * HEAD_DIM, (h + 1) * HEAD_DIM)
            q = qbuf[ph, n * BAND:(n + 1) * BAND, cols]
            kk = kbuf[ph, n * BAND:(n + 2) * BAND, cols]
            s = lax.dot_general(q, kk, nt, preferred_element_type=F32) * qk_scale + bias_s[h]
            if n == 0:
                s = s + pen
            scores.append(s)
        return scores

    def attend_finish(ph, r0, scores):
        units = units_of(r0)
        maxes = [jnp.max(s, axis=-1, keepdims=True) for s in scores]
        probs = [jnp.exp2(s - m).astype(BF16) for s, m in zip(scores, maxes)]
        outs = [jnp.dot(p, vbuf[ph, n * BAND:(n + 2) * BAND, h * vw:(h + 1) * vw], preferred_element_type=F32)
                for (n, h), p in zip(units, probs)]
        stats = None
        for (n, h), o, m in zip(units, outs, maxes):
            o_ref[ph, n * BAND:(n + 1) * BAND, h * HEAD_DIM:(h + 1) * HEAD_DIM] = o[:, :HEAD_DIM].astype(BF16)
            den = o[:, HEAD_DIM:]
            stats = jnp.broadcast_to(m, (BAND, LANES)) if h == 0 else jnp.where(lane == h, m, stats)
            stats = jnp.where(lane == HEADS_PER_GROUP + h, den, stats)
            if h == HEADS_PER_GROUP - 1:
                st_ref[ph, n * BAND:(n + 1) * BAND, :] = stats

    sub_blocks = [(ph, r0) for ph in range(pp) for r0 in range(0, rows, PROJ_ROWS)]
    for sb in sub_blocks[:2]:
        project(*sb)
    for i, sb in enumerate(sub_blocks):
        scores = attend_scores(*sb)
        if i + 2 < len(sub_blocks):
            project(*sub_blocks[i + 2])
        attend_finish(*sb, scores)

    for ph in range(pp):
        kbuf[ph, 0:BAND, :] = kbuf[ph, rows:rows + BAND, :]
        vbuf[ph, 0:BAND, :] = vbuf[ph, rows:rows + BAND, :]


def _attn_group(group, u_g, w_qkv, table, bkt):
    bsz, d, sub, _ = u_g.shape
    rows = min(ATT_R, sub)
    pp = min(d, ATT_R // rows)
    assert sub % rows == 0 and d % pp == 0 and (pp == 1 or rows == sub) and rows % PROJ_ROWS == 0

    def spec(width):
        return pl.BlockSpec((None, pp, rows, width), lambda b, p, c: (b, p, c, 0))

    return pl.pallas_call(
        functools.partial(_attn_kernel, group, pp, rows),
        name=f"attn{group}",
        grid=(bsz, d // pp, sub // rows),
        in_specs=[pl.BlockSpec(memory_space=pltpu.SMEM),
                  pl.BlockSpec((None, 2, BAND, BAND), lambda b, p, c: (group, 0, 0, 0)),
                  spec(D_MODEL),
                  pl.BlockSpec((D_MODEL, 3 * GROUP_W), lambda b, p, c: (0, group), pipeline_mode=pl.Buffered(1))],
        out_specs=[spec(GROUP_W), spec(LANES)],
        out_shape=[jax.ShapeDtypeStruct((bsz, d, sub, GROUP_W), BF16),
                   jax.ShapeDtypeStruct((bsz, d, sub, LANES), F32)],
        scratch_shapes=[pltpu.VMEM((pp, rows, GROUP_W), BF16),
                        pltpu.VMEM((pp, BAND + rows, GROUP_W), BF16),
                        pltpu.VMEM((pp, BAND + rows, 2 * GROUP_W), BF16),
                        pltpu.VMEM((HEADS_PER_GROUP, BAND, 2 * BAND), F32)],
        compiler_params=pltpu.CompilerParams(
            dimension_semantics=("arbitrary", "arbitrary", "arbitrary"),
            vmem_limit_bytes=VMEM_LIMIT),
    )(table, bkt, u_g, w_qkv)


def _merge_kernel(h_ref, o0_ref, o1_ref, o2_ref, l0_ref, l1_ref, l2_ref, rec_ref, gl_ref,
                  wat_ref, wrec_ref, wout_ref, gpost_ref, out_ref, osc, lsc):
    rec_d = jnp.dot(rec_ref[...], wrec_ref[...], preferred_element_type=F32)
    for gi, o_ref, l_ref in ((1, o1_ref, l1_ref), (2, o2_ref, l2_ref)):
        d = DILATIONS[gi]
        for p in range(d):
            idx = pl.ds(p, TM // d, stride=d)
            lsc[gi - 1, idx, :] = l_ref[p]
            for h in range(HEADS_PER_GROUP):
                osc[(gi - 1) * HEADS_PER_GROUP + h, idx, :] = (
                    o_ref[p, :, h * HEAD_DIM:(h + 1) * HEAD_DIM].astype(F32))
    st = (l0_ref[0], lsc[0], lsc[1])
    mx = jnp.maximum(jnp.maximum(st[0], st[1]), st[2])
    e = [jnp.exp2(x - mx) for x in st]
    den = [pltpu.roll(x, LANES - HEADS_PER_GROUP, 1) for x in st]
    inv = 1.0 / (e[0] * den[0] + e[1] * den[1] + e[2] * den[2])
    w0, w1, w2 = e[0] * inv, e[1] * inv, e[2] * inv
    parts = []
    for h in range(HEADS_PER_GROUP):
        cols = slice(h * HEAD_DIM, (h + 1) * HEAD_DIM)
        parts.append(w0[:, h:h + 1] * o0_ref[0, :, cols].astype(F32)
                     + w1[:, h:h + 1] * osc[h]
                     + w2[:, h:h + 1] * osc[HEADS_PER_GROUP + h])
    attn = jnp.concatenate(parts, axis=-1).astype(BF16)
    attn_d = jnp.dot(attn, wat_ref[...], preferred_element_type=F32)
    gl = gl_ref[...].astype(F32)
    merged = _sigmoid(gl[:, :D_MODEL]) * attn_d + _sigmoid(gl[:, D_MODEL:]) * rec_d
    mo = jnp.dot(merged.astype(BF16), wout_ref[...], preferred_element_type=F32)
    out_ref[...] = h_ref[...] + _rms(mo, gpost_ref[...])


def _merge(h3, o_list, lse_list, rec3, gl3, w_attn, w_rec, w_out, g_post):
    bsz, seq, _ = h3.shape

    def row(w):
        return pl.BlockSpec((None, TM, w), lambda b, i: (b, i, 0))

    def phased(d, w):
        return pl.BlockSpec((None, d, TM // d, w), lambda b, i: (b, 0, i, 0))

    return pl.pallas_call(
        _merge_kernel,
        name="merge",
        grid=(bsz, seq // TM),
        in_specs=[row(D_MODEL)] + [phased(d, GROUP_W) for d in DILATIONS]
                 + [phased(d, LANES) for d in DILATIONS] + [row(LRU_W), row(2 * D_MODEL),
                 _const_spec((GROUP_W, D_MODEL)), _const_spec((LRU_W, D_MODEL)),
                 _const_spec((D_MODEL, D_MODEL)), _const_spec((1, D_MODEL))],
        out_specs=row(D_MODEL),
        out_shape=jax.ShapeDtypeStruct((bsz, seq, D_MODEL), F32),
        scratch_shapes=[pltpu.VMEM((2 * HEADS_PER_GROUP, TM, LANES), F32),
                        pltpu.VMEM((2, TM, LANES), F32)],
        compiler_params=pltpu.CompilerParams(
            dimension_semantics=("arbitrary", "arbitrary"), vmem_limit_bytes=VMEM_LIMIT),
    )(h3, *o_list, *lse_list, rec3, gl3, w_attn, w_rec, w_out, g_post)


def kernel(x, ffn1_norm_pre, ffn1_norm_post, ffn1_w_gate, ffn1_w_up, ffn1_w_down, mix_norm_pre, mix_norm_post, w_in, rel_bias_table, conv_w, conv_b, lru_w_x, lru_b_x, lru_w_a, lru_b_a, lru_a_param, w_attn_branch, w_rec_branch, w_out, ffn2_norm_pre, ffn2_norm_post, ffn2_w_gate, ffn2_w_up, ffn2_w_down):
    bsz, seq, _ = x.shape
    n = bsz * seq
    depth = w_in.shape[0]
    bkt = jnp.asarray(_bucket_table())
    h = x.reshape(n, D_MODEL)
    for l in range(depth):
        gw = 3 * GROUP_W
        regroup = tuple((g * gw + j * GROUP_W, j * QKV_W + g * GROUP_W, GROUP_W)
                        for g in range(N_GROUPS) for j in range(3))
        jobs = [_CastJob(w_in[l], ((3 * QKV_W, regroup), (4 * LRU_W, ((0, 3 * QKV_W, 4 * LRU_W),)))),
                _plain_cast(lru_w_x[l].reshape(LRU_W, LRU_HD)), _plain_cast(lru_w_a[l].reshape(LRU_W, LRU_HD)),
                _plain_cast(w_attn_branch[l]), _plain_cast(w_rec_branch[l]), _plain_cast(w_out[l]),
                _plain_cast(ffn2_w_gate[l]), _plain_cast(ffn2_w_up[l]), _plain_cast(ffn2_w_down[l])]
        (h, u, u_d4, u_d16, w_qkv, w_xyg, wx, wa, w_attn, w_rec, w_o, w2_gate, w2_up, w2_down) = _ffn(
            h, ffn1_norm_pre[l][None], ffn1_norm_post[l][None],
            ffn1_w_gate[l].astype(BF16), ffn1_w_up[l].astype(BF16), ffn1_w_down[l].astype(BF16),
            g_next=mix_norm_pre[l][None], seq=seq, cast_jobs=jobs)
        h3 = h.reshape(bsz, seq, D_MODEL)
        u3 = u.reshape(bsz, seq, D_MODEL)

        rec, gl = _lru(u3, w_xyg, conv_w[l], conv_b[l][None],
                       wx.reshape(LRU_HEADS, LRU_HD, LRU_HD), lru_b_x[l].reshape(1, LRU_W),
                       wa.reshape(LRU_HEADS, LRU_HD, LRU_HD), lru_b_a[l].reshape(1, LRU_W),
                       lru_a_param[l][None])
        o_list, lse_list = [], []
        for g, u_g in enumerate((u3[:, None], u_d4, u_d16)):
            o_g, lse_g = _attn_group(g, u_g, w_qkv, rel_bias_table, bkt)
            o_list.append(o_g)
            lse_list.append(lse_g)
        h3 = _merge(h3, o_list, lse_list, rec, gl, w_attn, w_rec, w_o, mix_norm_post[l][None])

        h, = _ffn(h3.reshape(n, D_MODEL), ffn2_norm_pre[l][None], ffn2_norm_post[l][None],
                  w2_gate, w2_up, w2_down)
    return h.reshape(bsz, seq, D_MODEL)
```

```python
import functools
import math
from typing import NamedTuple

import numpy as np
import jax
import jax.numpy as jnp
from jax import lax
from jax.experimental import pallas as pl
from jax.experimental.pallas import tpu as pltpu

F32 = jnp.float32
BF16 = jnp.bfloat16

D_MODEL = 1024
HEAD_DIM = 128
HEADS_PER_GROUP = 4
DILATIONS = (1, 4, 16)
BAND = 128
N_GROUPS = 3
GROUP_W = HEADS_PER_GROUP * HEAD_DIM
QKV_W = N_GROUPS * GROUP_W
LRU_W = D_MODEL
LRU_HEADS = 4
LRU_HD = LRU_W // LRU_HEADS
CONV_W = 4
LRU_C = 8.0
D_FF = 2816
REL_BUCKETS = 32
REL_MAX_DISTANCE = 2048
EPS = 1e-6
TINY = 1e-30

LANES = 128
SUBLANES = 8
BF16_TILE_ROWS = 16
MAX_CAST_BLOCKS = 64
VMEM_LIMIT = 56 * 1024 * 1024

TM_FFN = 512
TM = 512
ATT_R = 2048
SEG = TM // SUBLANES
PITCH = SEG + SUBLANES
PIECE_W = 256
PROJ_ROWS = 512
PHASE_STEP = 4
assert DILATIONS == (1, PHASE_STEP, PHASE_STEP * PHASE_STEP)


def _rms(x, g):
    ms = jnp.mean(x * x, axis=-1, keepdims=True)
    return x * lax.rsqrt(ms + EPS) * g


def _sigmoid(x):
    return 0.5 + 0.5 * jnp.tanh(0.5 * x)


def _gelu_tanh(x):
    c = math.sqrt(2.0 / math.pi)
    inner = x * (c + (c * 0.044715) * (x * x))
    return (0.5 * x) * (1.0 + jnp.tanh(inner))


def _const_spec(shape):
    nd = len(shape)
    return pl.BlockSpec(shape, lambda *_: (0,) * nd, pipeline_mode=pl.Buffered(1))


class _CastJob(NamedTuple):
    src: jax.Array
    outs: tuple

    @property
    def block_rows(self):
        rows = self.src.shape[0]
        br = BF16_TILE_ROWS
        while rows % br or rows // br > MAX_CAST_BLOCKS:
            br += BF16_TILE_ROWS
        return br


def _plain_cast(w):
    return _CastJob(w, ((w.shape[1], ((0, 0, w.shape[1]),)),))


def _ffn_kernel(emit_normed, jobs_pieces, x_ref, gpre_ref, gpost_ref, wg_ref, wu_ref, wd_ref, *rest):
    n_jobs = len(jobs_pieces)
    n_normed = len(DILATIONS) if emit_normed else 0
    if emit_normed:
        gnext_ref, rest = rest[0], rest[1:]
    cast_in, o_ref = rest[:n_jobs], rest[n_jobs]
    normed_refs = rest[n_jobs + 1:n_jobs + 1 + n_normed]
    n_cast_out = sum(len(p) for p in jobs_pieces)
    cast_out = rest[n_jobs + 1 + n_normed:n_jobs + 1 + n_normed + n_cast_out]
    scratch = rest[n_jobs + 1 + n_normed + n_cast_out:]
    x = x_ref[...]
    xn = _rms(x, gpre_ref[...]).astype(BF16)
    g = jnp.dot(xn, wg_ref[...], preferred_element_type=F32)
    u = jnp.dot(xn, wu_ref[...], preferred_element_type=F32)
    a = (g * _sigmoid(g) * u).astype(BF16)
    f = jnp.dot(a, wd_ref[...], preferred_element_type=F32)
    h = x + 0.5 * _rms(f, gpost_ref[...])
    o_ref[...] = h
    if emit_normed:
        slab, slab4 = scratch
        u4_ref, u16_ref = normed_refs[1:]
        hn = _rms(h, gnext_ref[...])
        normed_refs[0][...] = hn.astype(BF16)
        r4 = TM_FFN // PHASE_STEP
        for c in range(D_MODEL // LANES):
            lanes = slice(c * LANES, (c + 1) * LANES)
            slab[c] = hn[:, lanes]
            for p in range(PHASE_STEP):
                part = slab[c, pl.ds(p, r4, stride=PHASE_STEP), :]
                u4_ref[p, :, lanes] = part.astype(BF16)
                slab4[c, p * r4:(p + 1) * r4, :] = part
            for p in range(PHASE_STEP):
                for q in range(PHASE_STEP):
                    u16_ref[q * PHASE_STEP + p, :, lanes] = (
                        slab4[c, pl.ds(p * r4 + q, r4 // PHASE_STEP, stride=PHASE_STEP), :].astype(BF16))
    outs = iter(cast_out)
    for src_ref, pieces_per_out in zip(cast_in, jobs_pieces):
        for pieces in pieces_per_out:
            dst_ref = next(outs)
            for dst_col, src_col, width in pieces:
                dst_ref[:, dst_col:dst_col + width] = src_ref[:, src_col:src_col + width].astype(BF16)


def _ffn(x2, g_pre, g_post, wg, wu, wd, g_next=None, seq=None, cast_jobs=()):
    n = x2.shape[0]
    steps = n // TM_FFN
    assert steps <= MAX_CAST_BLOCKS
    emit_normed = g_next is not None
    row = pl.BlockSpec((TM_FFN, D_MODEL), lambda i: (i, 0))
    in_specs = [row, _const_spec((1, D_MODEL)), _const_spec((1, D_MODEL)),
                _const_spec((D_MODEL, D_FF)), _const_spec((D_MODEL, D_FF)), _const_spec((D_FF, D_MODEL))]
    operands = [x2, g_pre, g_post, wg, wu, wd]
    out_specs = [row]
    out_shape = [jax.ShapeDtypeStruct((n, D_MODEL), F32)]
    scratch_shapes = []
    if emit_normed:
        tiles = seq // TM_FFN
        in_specs.append(_const_spec((1, D_MODEL)))
        operands.append(g_next)
        out_specs.append(row)
        out_shape.append(jax.ShapeDtypeStruct((n, D_MODEL), BF16))
        for d in DILATIONS[1:]:
            out_specs.append(pl.BlockSpec((None, d, TM_FFN // d, D_MODEL),
                                          functools.partial(lambda i, t: (i // t, 0, i % t, 0), t=tiles)))
            out_shape.append(jax.ShapeDtypeStruct((n // seq, d, seq // d, D_MODEL), BF16))
        scratch_shapes += [pltpu.VMEM((D_MODEL // LANES, TM_FFN, LANES), F32)] * 2
    for job in cast_jobs:
        rows, br = job.src.shape[0], job.block_rows
        index_map = functools.partial(lambda i, last: (jnp.minimum(i, last), 0), last=rows // br - 1)
        in_specs.append(pl.BlockSpec((br, job.src.shape[1]), index_map))
        operands.append(job.src)
        for cols, _ in job.outs:
            out_specs.append(pl.BlockSpec((br, cols), index_map))
            out_shape.append(jax.ShapeDtypeStruct((rows, cols), BF16))
    jobs_pieces = tuple(tuple(pieces for _, pieces in job.outs) for job in cast_jobs)
    return pl.pallas_call(
        functools.partial(_ffn_kernel, emit_normed, jobs_pieces),
        name="ffn",
        grid=(steps,),
        in_specs=in_specs,
        out_specs=out_specs,
        out_shape=out_shape,
        scratch_shapes=scratch_shapes,
        compiler_params=pltpu.CompilerParams(
            dimension_semantics=("arbitrary",), vmem_limit_bytes=VMEM_LIMIT),
    )(*operands)


def _lru_kernel(u_ref, wlru_ref, cw_ref, cb_ref, wx_ref, bx_ref, wa_ref, ba_ref, ap_ref,
                rec_ref, gl_ref, xbuf, a_s, b_s, gy_s, hcar, cin_s):
    @pl.when(pl.program_id(1) == 0)
    def _():
        xbuf[0:SUBLANES, :] = jnp.zeros((SUBLANES, LRU_W), F32)
        hcar[...] = jnp.zeros(hcar.shape, F32)

    slabs_per_head = LRU_HD // LANES
    ap = -ap_ref[...]
    half_c_softplus = (0.5 * LRU_C) * (jnp.maximum(ap, 0.0) + jnp.log1p(jnp.exp(-jnp.abs(ap))))
    state = {}

    def proj_x(k):
        cols = slice(k * PIECE_W, (k + 1) * PIECE_W)
        xbuf[SUBLANES:SUBLANES + TM, cols] = jnp.dot(u_ref[0], wlru_ref[:, cols], preferred_element_type=F32)

    def proj_y(k):
        cols = slice(k * PIECE_W, (k + 1) * PIECE_W)
        yr = jnp.dot(u_ref[0], wlru_ref[:, LRU_W + k * PIECE_W:LRU_W + (k + 1) * PIECE_W],
                     preferred_element_type=F32)
        gy_s[:, cols] = _gelu_tanh(yr)

    def proj_gl(k):
        cols = slice(k * PIECE_W, (k + 1) * PIECE_W)
        gl_ref[0, :, cols] = jnp.dot(u_ref[0], wlru_ref[:, 2 * LRU_W + k * PIECE_W:2 * LRU_W + (k + 1) * PIECE_W],
                                     preferred_element_type=F32).astype(BF16)

    def gate_dots(i):
        xh = state["xcb", i]
        state["pre_x", i] = jnp.dot(xh, wx_ref[i], preferred_element_type=F32)
        state["pre_a", i] = jnp.dot(xh, wa_ref[i], preferred_element_type=F32)

    def conv(i):
        cols = slice(i * LRU_HD, (i + 1) * LRU_HD)
        xc = cb_ref[:, cols] + cw_ref[0:1, cols] * xbuf[SUBLANES - 3:SUBLANES - 3 + TM, cols]
        for j in range(1, CONV_W):
            off = SUBLANES - (CONV_W - 1) + j
            xc = xc + cw_ref[j:j + 1, cols] * xbuf[off:off + TM, cols]
        xbuf[0:SUBLANES, cols] = xbuf[TM:TM + SUBLANES, cols]
        state["xc", i] = xc
        state["xcb", i] = xc.astype(BF16)

    def gates(i):
        cols = slice(i * LRU_HD, (i + 1) * LRU_HD)
        gx = _sigmoid(state["pre_x", i] + bx_ref[:, cols])
        ga = _sigmoid(state["pre_a", i] + ba_ref[:, cols])
        t = jnp.tanh(ga * half_c_softplus[:, cols])
        r = 1.0 / (1.0 + t)
        a = (1.0 - t) * r
        t4 = 4.0 * t
        b = ((t4 * lax.rsqrt(jnp.maximum(t4, TINY))) * r) * (gx * state["xc", i])
        for k in range(slabs_per_head):
            c = i * slabs_per_head + k
            for s in range(SUBLANES):
                a_s[c, s * PITCH:s * PITCH + SEG, :] = a[s * SEG:(s + 1) * SEG, k * LANES:(k + 1) * LANES]
                b_s[c, s * PITCH:s * PITCH + SEG, :] = b[s * SEG:(s + 1) * SEG, k * LANES:(k + 1) * LANES]

    def scan(i):
        slab_ids = range(i * slabs_per_head, (i + 1) * slabs_per_head)
        hs = {c: jnp.zeros((SUBLANES, LANES), F32) for c in slab_ids}
        cum = {c: jnp.ones((SUBLANES, LANES), F32) for c in slab_ids}
        for j in range(SEG):
            idx = pl.ds(j, SUBLANES, stride=PITCH)
            for c in slab_ids:
                aj = a_s[c, idx, :]
                hs[c] = aj * hs[c] + b_s[c, idx, :]
                cum[c] = aj * cum[c]
                a_s[c, idx, :] = cum[c]
                b_s[c, idx, :] = hs[c]
        row = lax.broadcasted_iota(jnp.int32, (SUBLANES, LANES), 0)
        for c in slab_ids:
            cols = slice(c * LANES, (c + 1) * LANES)
            h_in = jnp.broadcast_to(hcar[SUBLANES - 1:SUBLANES, cols], (SUBLANES, LANES))
            cin = h_in
            for _ in range(SUBLANES - 1):
                out = hs[c] + cum[c] * cin
                cin = jnp.where(row == 0, h_in, pltpu.roll(out, 1, 0))
            cin_s[c] = cin
            hcar[:, cols] = hs[c] + cum[c] * cin
        for c in slab_ids:
            cols = slice(c * LANES, (c + 1) * LANES)
            for s in range(SUBLANES):
                rows = slice(s * SEG, (s + 1) * SEG)
                prow = slice(s * PITCH, s * PITCH + SEG)
                h = b_s[c, prow, :] + a_s[c, prow, :] * cin_s[c, s:s + 1, :]
                rec_ref[0, rows, cols] = (h * gy_s[rows, cols]).astype(BF16)

    heads_per_piece = PIECE_W // LRU_HD
    filler = iter([functools.partial(proj_gl, k) for k in range(2 * D_MODEL // PIECE_W)])

    def fill():
        piece = next(filler, None)
        if piece is not None:
            piece()

    for k in range(LRU_W // PIECE_W):
        proj_x(k)
    for k in range(LRU_W // PIECE_W):
        proj_y(k)
        for i in range(k * heads_per_piece, (k + 1) * heads_per_piece):
            conv(i)
            fill()
            gate_dots(i)
    for i in range(LRU_HEADS):
        gates(i)
        fill()
        scan(i)
    for piece in filler:
        piece()


def _lru(u3, w_xyg, conv_w, conv_b, wx, bx, wa, ba, ap):
    bsz, seq, _ = u3.shape
    n_slab = LRU_W // LANES

    def row(w):
        return pl.BlockSpec((1, TM, w), lambda b, t: (b, t, 0))

    return pl.pallas_call(
        _lru_kernel,
        name="lru",
        grid=(bsz, seq // TM),
        in_specs=[row(D_MODEL), _const_spec((D_MODEL, 4 * LRU_W)),
                  _const_spec((CONV_W, LRU_W)), _const_spec((1, LRU_W)),
                  _const_spec((LRU_HEADS, LRU_HD, LRU_HD)), _const_spec((1, LRU_W)),
                  _const_spec((LRU_HEADS, LRU_HD, LRU_HD)), _const_spec((1, LRU_W)),
                  _const_spec((1, LRU_W))],
        out_specs=[row(LRU_W), row(2 * D_MODEL)],
        out_shape=[jax.ShapeDtypeStruct((bsz, seq, LRU_W), BF16),
                   jax.ShapeDtypeStruct((bsz, seq, 2 * D_MODEL), BF16)],
        scratch_shapes=[pltpu.VMEM((TM + SUBLANES, LRU_W), F32),
                        pltpu.VMEM((n_slab, SUBLANES * PITCH, LANES), F32),
                        pltpu.VMEM((n_slab, SUBLANES * PITCH, LANES), F32),
                        pltpu.VMEM((TM, LRU_W), F32),
                        pltpu.VMEM((SUBLANES, LRU_W), F32),
                        pltpu.VMEM((n_slab, SUBLANES, LANES), F32)],
        compiler_params=pltpu.CompilerParams(
            dimension_semantics=("arbitrary", "arbitrary"), vmem_limit_bytes=VMEM_LIMIT),
    )(u3, w_xyg, conv_w, conv_b, wx, bx, wa, ba, ap)


def _bucket_table():
    qi = np.arange(BAND)[:, None]
    kj = np.arange(BAND)[None, :]
    max_exact = REL_BUCKETS // 2
    out = np.zeros((N_GROUPS, 2, BAND, BAND), np.int32)
    for g, d in enumerate(DILATIONS):
        for half in range(2):
            steps = qi + BAND - kj if half == 0 else qi - kj
            valid = (steps >= 0) & (steps <= BAND)
            dist = np.maximum(steps, 0) * d
            nf = np.maximum(dist, 1).astype(np.float32)
            large = max_exact + (np.log(nf / np.float32(max_exact))
                                 / np.float32(math.log(REL_MAX_DISTANCE / max_exact))
                                 * np.float32(REL_BUCKETS - max_exact)).astype(np.int32)
            large = np.minimum(large, REL_BUCKETS - 1)
            bucket = np.where(dist < max_exact, dist, large)
            out[g, half] = np.where(valid, bucket, -1)
    return out


def _attn_kernel(group, pp, rows, tab_ref, bkt_ref, u_ref, w_ref, o_ref, st_ref,
                 qbuf, kbuf, vbuf, bias_s):
    first = (pl.program_id(0) == 0) & (pl.program_id(1) == 0) & (pl.program_id(2) == 0)
    c = pl.program_id(2)
    log2e = math.log2(math.e)
    vw = 2 * HEAD_DIM

    @pl.when(first)
    def _():
        for h in range(HEADS_PER_GROUP):
            for half in range(2):
                bk = bkt_ref[half]
                bias = jnp.full((BAND, BAND), -jnp.inf, F32)
                for n in range(REL_BUCKETS):
                    bias = jnp.where(bk == n, tab_ref[n, group * HEADS_PER_GROUP + h] * log2e, bias)
                bias_s[h, :, half * BAND:(half + 1) * BAND] = bias
            for ph in range(pp):
                vbuf[ph, :, h * vw + HEAD_DIM:(h + 1) * vw] = jnp.ones((BAND + rows, HEAD_DIM), BF16)

    @pl.when(c == 0)
    def _():
        for ph in range(pp):
            kbuf[ph, 0:BAND, :] = jnp.zeros((BAND, GROUP_W), BF16)
            for h in range(HEADS_PER_GROUP):
                vbuf[ph, 0:BAND, h * vw:h * vw + HEAD_DIM] = jnp.zeros((BAND, HEAD_DIM), BF16)

    lane2 = lax.broadcasted_iota(jnp.int32, (1, 2 * BAND), 1)
    pen = jnp.where((lane2 < BAND) & (c == 0), -jnp.inf, 0.0).astype(F32)
    qk_scale = log2e / math.sqrt(HEAD_DIM)
    nt = (((1,), (1,)), ((), ()))
    lane = lax.broadcasted_iota(jnp.int32, (BAND, LANES), 1)

    def project(ph, r0):
        qkv = jnp.dot(u_ref[ph, r0:r0 + PROJ_ROWS, :], w_ref[...], preferred_element_type=F32).astype(BF16)
        qbuf[ph, r0:r0 + PROJ_ROWS, :] = qkv[:, :GROUP_W]
        kbuf[ph, BAND + r0:BAND + r0 + PROJ_ROWS, :] = qkv[:, GROUP_W:2 * GROUP_W]
        for h in range(HEADS_PER_GROUP):
            vbuf[ph, BAND + r0:BAND + r0 + PROJ_ROWS, h * vw:h * vw + HEAD_DIM] = (
                qkv[:, 2 * GROUP_W + h * HEAD_DIM:2 * GROUP_W + (h + 1) * HEAD_DIM])

    def units_of(r0):
        return [(n, h) for n in range(r0 // BAND, (r0 + PROJ_ROWS) // BAND) for h in range(HEADS_PER_GROUP)]

    def attend_scores(ph, r0):
        scores = []
        for n, h in units_of(r0):
            cols = slice(h * HEAD_DIM, (h + 1) * HEAD_DIM)
            q = qbuf[ph, n * BAND:(n + 1) * BAND, cols]
            kk = kbuf[ph, n * BAND:(n + 2) * BAND, cols]
            s = lax.dot_general(q, kk, nt, preferred_element_type=F32) * qk_scale + bias_s[h]
            if n == 0:
                s = s + pen
            scores.append(s)
        return scores

    def attend_finish(ph, r0, scores):
        units = units_of(r0)
        maxes = [jnp.max(s, axis=-1, keepdims=True) for s in scores]
        probs = [jnp.exp2(s - m).astype(BF16) for s, m in zip(scores, maxes)]
        outs = [jnp.dot(p, vbuf[ph, n * BAND:(n + 2) * BAND, h * vw:(h + 1) * vw], preferred_element_type=F32)
                for (n, h), p in zip(units, probs)]
        stats = None
        for (n, h), o, m in zip(units, outs, maxes):
            o_ref[ph, n * BAND:(n + 1) * BAND, h * HEAD_DIM:(h + 1) * HEAD_DIM] = o[:, :HEAD_DIM].astype(BF16)
            den = o[:, HEAD_DIM:]
            stats = jnp.broadcast_to(m, (BAND, LANES)) if h == 0 else jnp.where(lane == h, m, stats)
            stats = jnp.where(lane == HEADS_PER_GROUP + h, den, stats)
            if h == HEADS_PER_GROUP - 1:
                st_ref[ph, n * BAND:(n + 1) * BAND, :] = stats

    sub_blocks = [(ph, r0) for ph in range(pp) for r0 in range(0, rows, PROJ_ROWS)]
    for sb in sub_blocks[:2]:
        project(*sb)
    for i, sb in enumerate(sub_blocks):
        scores = attend_scores(*sb)
        if i + 2 < len(sub_blocks):
            project(*sub_blocks[i + 2])
        attend_finish(*sb, scores)

    for ph in range(pp):
        kbuf[ph, 0:BAND, :] = kbuf[ph, rows:rows + BAND, :]
        vbuf[ph, 0:BAND, :] = vbuf[ph, rows:rows + BAND, :]


def _attn_group(group, u_g, w_qkv, table, bkt):
    bsz, d, sub, _ = u_g.shape
    rows = min(ATT_R, sub)
    pp = min(d, ATT_R // rows)
    assert sub % rows == 0 and d % pp == 0 and (pp == 1 or rows == sub) and rows % PROJ_ROWS == 0

    def spec(width):
        return pl.BlockSpec((None, pp, rows, width), lambda b, p, c: (b, p, c, 0))

    return pl.pallas_call(
        functools.partial(_attn_kernel, group, pp, rows),
        name=f"attn{group}",
        grid=(bsz, d // pp, sub // rows),
        in_specs=[pl.BlockSpec(memory_space=pltpu.SMEM),
                  pl.BlockSpec((None, 2, BAND, BAND), lambda b, p, c: (group, 0, 0, 0)),
                  spec(D_MODEL),
                  pl.BlockSpec((D_MODEL, 3 * GROUP_W), lambda b, p, c: (0, group), pipeline_mode=pl.Buffered(1))],
        out_specs=[spec(GROUP_W), spec(LANES)],
        out_shape=[jax.ShapeDtypeStruct((bsz, d, sub, GROUP_W), BF16),
                   jax.ShapeDtypeStruct((bsz, d, sub, LANES), F32)],
        scratch_shapes=[pltpu.VMEM((pp, rows, GROUP_W), BF16),
                        pltpu.VMEM((pp, BAND + rows, GROUP_W), BF16),
                        pltpu.VMEM((pp, BAND + rows, 2 * GROUP_W), BF16),
                        pltpu.VMEM((HEADS_PER_GROUP, BAND, 2 * BAND), F32)],
        compiler_params=pltpu.CompilerParams(
            dimension_semantics=("arbitrary", "arbitrary", "arbitrary"),
            vmem_limit_bytes=VMEM_LIMIT),
    )(table, bkt, u_g, w_qkv)


def _merge_kernel(h_ref, o0_ref, o1_ref, o2_ref, l0_ref, l1_ref, l2_ref, rec_ref, gl_ref,
                  wat_ref, wrec_ref, wout_ref, gpost_ref, out_ref, osc, lsc):
    rec_d = jnp.dot(rec_ref[...], wrec_ref[...], preferred_element_type=F32)
    for gi, o_ref, l_ref in ((1, o1_ref, l1_ref), (2, o2_ref, l2_ref)):
        d = DILATIONS[gi]
        for p in range(d):
            idx = pl.ds(p, TM // d, stride=d)
            lsc[gi - 1, idx, :] = l_ref[p]
            for h in range(HEADS_PER_GROUP):
                osc[(gi - 1) * HEADS_PER_GROUP + h, idx, :] = (
                    o_ref[p, :, h * HEAD_DIM:(h + 1) * HEAD_DIM].astype(F32))
    st = (l0_ref[0], lsc[0], lsc[1])
    mx = jnp.maximum(jnp.maximum(st[0], st[1]), st[2])
    e = [jnp.exp2(x - mx) for x in st]
    den = [pltpu.roll(x, LANES - HEADS_PER_GROUP, 1) for x in st]
    inv = 1.0 / (e[0] * den[0] + e[1] * den[1] + e[2] * den[2])
    w0, w1, w2 = e[0] * inv, e[1] * inv, e[2] * inv
    parts = []
    for h in range(HEADS_PER_GROUP):
        cols = slice(h * HEAD_DIM, (h + 1) * HEAD_DIM)
        parts.append(w0[:, h:h + 1] * o0_ref[0, :, cols].astype(F32)
                     + w1[:, h:h + 1] * osc[h]
                     + w2[:, h:h + 1] * osc[HEADS_PER_GROUP + h])
    attn = jnp.concatenate(parts, axis=-1).astype(BF16)
    attn_d = jnp.dot(attn, wat_ref[...], preferred_element_type=F32)
    gl = gl_ref[...].astype(F32)
    merged = _sigmoid(gl[:, :D_MODEL]) * attn_d + _sigmoid(gl[:, D_MODEL:]) * rec_d
    mo = jnp.dot(merged.astype(BF16), wout_ref[...], preferred_element_type=F32)
    out_ref[...] = h_ref[...] + _rms(mo, gpost_ref[...])


def _merge(h3, o_list, lse_list, rec3, gl3, w_attn, w_rec, w_out, g_post):
    bsz, seq, _ = h3.shape

    def row(w):
        return pl.BlockSpec((None, TM, w), lambda b, i: (b, i, 0))

    def phased(d, w):
        return pl.BlockSpec((None, d, TM // d, w), lambda b, i: (b, 0, i, 0))

    return pl.pallas_call(
        _merge_kernel,
        name="merge",
        grid=(bsz, seq // TM),
        in_specs=[row(D_MODEL)] + [phased(d, GROUP_W) for d in DILATIONS]
                 + [phased(d, LANES) for d in DILATIONS] + [row(LRU_W), row(2 * D_MODEL),
                 _const_spec((GROUP_W, D_MODEL)), _const_spec((LRU_W, D_MODEL)),
                 _const_spec((D_MODEL, D_MODEL)), _const_spec((1, D_MODEL))],
        out_specs=row(D_MODEL),
        out_shape=jax.ShapeDtypeStruct((bsz, seq, D_MODEL), F32),
        scratch_shapes=[pltpu.VMEM((2 * HEADS_PER_GROUP, TM, LANES), F32),
                        pltpu.VMEM((2, TM, LANES), F32)],
        compiler_params=pltpu.CompilerParams(
            dimension_semantics=("arbitrary", "arbitrary"), vmem_limit_bytes=VMEM_LIMIT),
    )(h3, *o_list, *lse_list, rec3, gl3, w_attn, w_rec, w_out, g_post)


def kernel(x, ffn1_norm_pre, ffn1_norm_post, ffn1_w_gate, ffn1_w_up, ffn1_w_down, mix_norm_pre, mix_norm_post, w_in, rel_bias_table, conv_w, conv_b, lru_w_x, lru_b_x, lru_w_a, lru_b_a, lru_a_param, w_attn_branch, w_rec_branch, w_out, ffn2_norm_pre, ffn2_norm_post, ffn2_w_gate, ffn2_w_up, ffn2_w_down):
    bsz, seq, _ = x.shape
    n = bsz * seq
    depth = w_in.shape[0]
    bkt = jnp.asarray(_bucket_table())
    h = x.reshape(n, D_MODEL)
    for l in range(depth):
        gw = 3 * GROUP_W
        regroup = tuple((g * gw + j * GROUP_W, j * QKV_W + g * GROUP_W, GROUP_W)
                        for g in range(N_GROUPS) for j in range(3))
        jobs = [_CastJob(w_in[l], ((3 * QKV_W, regroup), (4 * LRU_W, ((0, 3 * QKV_W, 4 * LRU_W),)))),
                _plain_cast(lru_w_x[l].reshape(LRU_W, LRU_HD)), _plain_cast(lru_w_a[l].reshape(LRU_W, LRU_HD)),
                _plain_cast(w_attn_branch[l]), _plain_cast(w_rec_branch[l]), _plain_cast(w_out[l]),
                _plain_cast(ffn2_w_gate[l]), _plain_cast(ffn2_w_up[l]), _plain_cast(ffn2_w_down[l])]
        (h, u, u_d4, u_d16, w_qkv, w_xyg, wx, wa, w_attn, w_rec, w_o, w2_gate, w2_up, w2_down) = _ffn(
            h, ffn1_norm_pre[l][None], ffn1_norm_post[l][None],
            ffn1_w_gate[l].astype(BF16), ffn1_w_up[l].astype(BF16), ffn1_w_down[l].astype(BF16),
            g_next=mix_norm_pre[l][None], seq=seq, cast_jobs=jobs)
        h3 = h.reshape(bsz, seq, D_MODEL)
        u3 = u.reshape(bsz, seq, D_MODEL)

        rec, gl = _lru(u3, w_xyg, conv_w[l], conv_b[l][None],
                       wx.reshape(LRU_HEADS, LRU_HD, LRU_HD), lru_b_x[l].reshape(1, LRU_W),
                       wa.reshape(LRU_HEADS, LRU_HD, LRU_HD), lru_b_a[l].reshape(1, LRU_W),
                       lru_a_param[l][None])
        o_list, lse_list = [], []
        for g, u_g in enumerate((u3[:, None], u_d4, u_d16)):
            o_g, lse_g = _attn_group(g, u_g, w_qkv, rel_bias_table, bkt)
            o_list.append(o_g)
            lse_list.append(lse_g)
        h3 = _merge(h3, o_list, lse_list, rec, gl, w_attn, w_rec, w_o, mix_norm_post[l][None])

        h, = _ffn(h3.reshape(n, D_MODEL), ffn2_norm_pre[l][None], ffn2_norm_post[l][None],
                  w2_gate, w2_up, w2_down)
    return h.reshape(bsz, seq, D_MODEL)
```

```python
import functools
import math
from typing import NamedTuple

import numpy as np
import jax
import jax.numpy as jnp
from jax import lax
from jax.experimental import pallas as pl
from jax.experimental.pallas import tpu as pltpu

F32 = jnp.float32
BF16 = jnp.bfloat16

D_MODEL = 1024
HEAD_DIM = 128
HEADS_PER_GROUP = 4
DILATIONS = (1, 4, 16)
BAND = 128
N_GROUPS = 3
GROUP_W = HEADS_PER_GROUP * HEAD_DIM
QKV_W = N_GROUPS * GROUP_W
LRU_W = D_MODEL
LRU_HEADS = 4
LRU_HD = LRU_W // LRU_HEADS
CONV_W = 4
LRU_C = 8.0
D_FF = 2816
REL_BUCKETS = 32
REL_MAX_DISTANCE = 2048
EPS = 1e-6
TINY = 1e-30

LANES = 128
SUBLANES = 8
BF16_TILE_ROWS = 16
MAX_CAST_BLOCKS = 64
VMEM_LIMIT = 56 * 1024 * 1024

TM_FFN = 512
TM = 512
ATT_R = 2048
SEG = TM // SUBLANES
PITCH = SEG + SUBLANES
PIECE_W = 256
PROJ_ROWS = 256
PHASE_STEP = 4
assert DILATIONS == (1, PHASE_STEP, PHASE_STEP * PHASE_STEP)


def _rms(x, g):
    ms = jnp.mean(x * x, axis=-1, keepdims=True)
    return x * lax.rsqrt(ms + EPS) * g


def _sigmoid(x):
    return 0.5 + 0.5 * jnp.tanh(0.5 * x)


def _gelu_tanh(x):
    c = math.sqrt(2.0 / math.pi)
    inner = x * (c + (c * 0.044715) * (x * x))
    return (0.5 * x) * (1.0 + jnp.tanh(inner))


def _const_spec(shape):
    nd = len(shape)
    return pl.BlockSpec(shape, lambda *_: (0,) * nd, pipeline_mode=pl.Buffered(1))


class _CastJob(NamedTuple):
    src: jax.Array
    outs: tuple

    @property
    def block_rows(self):
        rows = self.src.shape[0]
        br = BF16_TILE_ROWS
        while rows % br or rows // br > MAX_CAST_BLOCKS:
            br += BF16_TILE_ROWS
        return br


def _plain_cast(w):
    return _CastJob(w, ((w.shape[1], ((0, 0, w.shape[1]),)),))


def _ffn_kernel(emit_normed, jobs_pieces, x_ref, gpre_ref, gpost_ref, wg_ref, wu_ref, wd_ref, *rest):
    n_jobs = len(jobs_pieces)
    n_normed = len(DILATIONS) if emit_normed else 0
    if emit_normed:
        gnext_ref, rest = rest[0], rest[1:]
    cast_in, o_ref = rest[:n_jobs], rest[n_jobs]
    normed_refs = rest[n_jobs + 1:n_jobs + 1 + n_normed]
    n_cast_out = sum(len(p) for p in jobs_pieces)
    cast_out = rest[n_jobs + 1 + n_normed:n_jobs + 1 + n_normed + n_cast_out]
    scratch = rest[n_jobs + 1 + n_normed + n_cast_out:]
    x = x_ref[...]
    xn = _rms(x, gpre_ref[...]).astype(BF16)
    g = jnp.dot(xn, wg_ref[...], preferred_element_type=F32)
    u = jnp.dot(xn, wu_ref[...], preferred_element_type=F32)
    a = (g * _sigmoid(g) * u).astype(BF16)
    f = jnp.dot(a, wd_ref[...], preferred_element_type=F32)
    h = x + 0.5 * _rms(f, gpost_ref[...])
    o_ref[...] = h
    if emit_normed:
        slab, slab4 = scratch
        u4_ref, u16_ref = normed_refs[1:]
        hn = _rms(h, gnext_ref[...])
        normed_refs[0][...] = hn.astype(BF16)
        r4 = TM_FFN // PHASE_STEP
        for c in range(D_MODEL // LANES):
            lanes = slice(c * LANES, (c + 1) * LANES)
            slab[c] = hn[:, lanes]
            for p in range(PHASE_STEP):
                part = slab[c, pl.ds(p, r4, stride=PHASE_STEP), :]
                u4_ref[p, :, lanes] = part.astype(BF16)
                slab4[c, p * r4:(p + 1) * r4, :] = part
            for p in range(PHASE_STEP):
                for q in range(PHASE_STEP):
                    u16_ref[q * PHASE_STEP + p, :, lanes] = (
                        slab4[c, pl.ds(p * r4 + q, r4 // PHASE_STEP, stride=PHASE_STEP), :].astype(BF16))
    outs = iter(cast_out)
    for src_ref, pieces_per_out in zip(cast_in, jobs_pieces):
        for pieces in pieces_per_out:
            dst_ref = next(outs)
            for dst_col, src_col, width in pieces:
                dst_ref[:, dst_col:dst_col + width] = src_ref[:, src_col:src_col + width].astype(BF16)


def _ffn(x2, g_pre, g_post, wg, wu, wd, g_next=None, seq=None, cast_jobs=()):
    n = x2.shape[0]
    steps = n // TM_FFN
    assert steps <= MAX_CAST_BLOCKS
    emit_normed = g_next is not None
    row = pl.BlockSpec((TM_FFN, D_MODEL), lambda i: (i, 0))
    in_specs = [row, _const_spec((1, D_MODEL)), _const_spec((1, D_MODEL)),
                _const_spec((D_MODEL, D_FF)), _const_spec((D_MODEL, D_FF)), _const_spec((D_FF, D_MODEL))]
    operands = [x2, g_pre, g_post, wg, wu, wd]
    out_specs = [row]
    out_shape = [jax.ShapeDtypeStruct((n, D_MODEL), F32)]
    scratch_shapes = []
    if emit_normed:
        tiles = seq // TM_FFN
        in_specs.append(_const_spec((1, D_MODEL)))
        operands.append(g_next)
        out_specs.append(row)
        out_shape.append(jax.ShapeDtypeStruct((n, D_MODEL), BF16))
        for d in DILATIONS[1:]:
            out_specs.append(pl.BlockSpec((None, d, TM_FFN // d, D_MODEL),
                                          functools.partial(lambda i, t: (i // t, 0, i % t, 0), t=tiles)))
            out_shape.append(jax.ShapeDtypeStruct((n // seq, d, seq // d, D_MODEL), BF16))
        scratch_shapes += [pltpu.VMEM((D_MODEL // LANES, TM_FFN, LANES), F32)] * 2
    for job in cast_jobs:
        rows, br = job.src.shape[0], job.block_rows
        index_map = functools.partial(lambda i, last: (jnp.minimum(i, last), 0), last=rows // br - 1)
        in_specs.append(pl.BlockSpec((br, job.src.shape[1]), index_map))
        operands.append(job.src)
        for cols, _ in job.outs:
            out_specs.append(pl.BlockSpec((br, cols), index_map))
            out_shape.append(jax.ShapeDtypeStruct((rows, cols), BF16))
    jobs_pieces = tuple(tuple(pieces for _, pieces in job.outs) for job in cast_jobs)
    return pl.pallas_call(
        functools.partial(_ffn_kernel, emit_normed, jobs_pieces),
        name="ffn",
        grid=(steps,),
        in_specs=in_specs,
        out_specs=out_specs,
        out_shape=out_shape,
        scratch_shapes=scratch_shapes,
        compiler_params=pltpu.CompilerParams(
            dimension_semantics=("arbitrary",), vmem_limit_bytes=VMEM_LIMIT),
    )(*operands)


def _lru_kernel(u_ref, wlru_ref, cw_ref, cb_ref, wx_ref, bx_ref, wa_ref, ba_ref, ap_ref,
                rec_ref, gl_ref, xbuf, a_s, b_s, gy_s, hcar, cin_s):
    @pl.when(pl.program_id(1) == 0)
    def _():
        xbuf[0:SUBLANES, :] = jnp.zeros((SUBLANES, LRU_W), F32)
        hcar[...] = jnp.zeros(hcar.shape, F32)

    slabs_per_head = LRU_HD // LANES
    ap = -ap_ref[...]
    half_c_softplus = (0.5 * LRU_C) * (jnp.maximum(ap, 0.0) + jnp.log1p(jnp.exp(-jnp.abs(ap))))
    state = {}

    def proj_x(k):
        cols = slice(k * PIECE_W, (k + 1) * PIECE_W)
        xbuf[SUBLANES:SUBLANES + TM, cols] = jnp.dot(u_ref[0], wlru_ref[:, cols], preferred_element_type=F32)

    def proj_y(k):
        cols = slice(k * PIECE_W, (k + 1) * PIECE_W)
        yr = jnp.dot(u_ref[0], wlru_ref[:, LRU_W + k * PIECE_W:LRU_W + (k + 1) * PIECE_W],
                     preferred_element_type=F32)
        gy_s[:, cols] = _gelu_tanh(yr)

    def proj_gl(k):
        cols = slice(k * PIECE_W, (k + 1) * PIECE_W)
        gl_ref[0, :, cols] = jnp.dot(u_ref[0], wlru_ref[:, 2 * LRU_W + k * PIECE_W:2 * LRU_W + (k + 1) * PIECE_W],
                                     preferred_element_type=F32).astype(BF16)

    def gate_dots(i):
        xh = state["xcb", i]
        state["pre_x", i] = jnp.dot(xh, wx_ref[i], preferred_element_type=F32)
        state["pre_a", i] = jnp.dot(xh, wa_ref[i], preferred_element_type=F32)

    def conv(i):
        cols = slice(i * LRU_HD, (i + 1) * LRU_HD)
        xc = cb_ref[:, cols]
        for j in range(CONV_W):
            off = SUBLANES - (CONV_W - 1) + j
            xc = xc + cw_ref[j:j + 1, cols] * xbuf[off:off + TM, cols]
        xbuf[0:SUBLANES, cols] = xbuf[TM:TM + SUBLANES, cols]
        state["xc", i] = xc
        state["xcb", i] = xc.astype(BF16)

    def gates(i):
        cols = slice(i * LRU_HD, (i + 1) * LRU_HD)
        gx = _sigmoid(state["pre_x", i] + bx_ref[:, cols])
        ga = _sigmoid(state["pre_a", i] + ba_ref[:, cols])
        t = jnp.tanh(ga * half_c_softplus[:, cols])
        r = 1.0 / (1.0 + t)
        a = (1.0 - t) * r
        t4 = 4.0 * t
        b = ((t4 * lax.rsqrt(jnp.maximum(t4, TINY))) * r) * (gx * state["xc", i])
        for k in range(slabs_per_head):
            c = i * slabs_per_head + k
            for s in range(SUBLANES):
                a_s[c, s * PITCH:s * PITCH + SEG, :] = a[s * SEG:(s + 1) * SEG, k * LANES:(k + 1) * LANES]
                b_s[c, s * PITCH:s * PITCH + SEG, :] = b[s * SEG:(s + 1) * SEG, k * LANES:(k + 1) * LANES]

    def scan(i):
        slab_ids = range(i * slabs_per_head, (i + 1) * slabs_per_head)
        hs = {c: jnp.zeros((SUBLANES, LANES), F32) for c in slab_ids}
        cum = {c: jnp.ones((SUBLANES, LANES), F32) for c in slab_ids}
        for j in range(SEG):
            idx = pl.ds(j, SUBLANES, stride=PITCH)
            for c in slab_ids:
                aj = a_s[c, idx, :]
                hs[c] = aj * hs[c] + b_s[c, idx, :]
                cum[c] = aj * cum[c]
                a_s[c, idx, :] = cum[c]
                b_s[c, idx, :] = hs[c]
        row = lax.broadcasted_iota(jnp.int32, (SUBLANES, LANES), 0)
        for c in slab_ids:
            cols = slice(c * LANES, (c + 1) * LANES)
            h_in = jnp.broadcast_to(hcar[SUBLANES - 1:SUBLANES, cols], (SUBLANES, LANES))
            cin = h_in
            for _ in range(SUBLANES - 1):
                out = hs[c] + cum[c] * cin
                cin = jnp.where(row == 0, h_in, pltpu.roll(out, 1, 0))
            cin_s[c] = cin
            hcar[:, cols] = hs[c] + cum[c] * cin
        for c in slab_ids:
            cols = slice(c * LANES, (c + 1) * LANES)
            for s in range(SUBLANES):
                rows = slice(s * SEG, (s + 1) * SEG)
                prow = slice(s * PITCH, s * PITCH + SEG)
                h = b_s[c, prow, :] + a_s[c, prow, :] * cin_s[c, s:s + 1, :]
                rec_ref[0, rows, cols] = (h * gy_s[rows, cols]).astype(BF16)

    heads_per_piece = PIECE_W // LRU_HD
    filler = iter([functools.partial(proj_gl, k) for k in range(2 * D_MODEL // PIECE_W)])

    def fill():
        piece = next(filler, None)
        if piece is not None:
            piece()

    for k in range(LRU_W // PIECE_W):
        proj_x(k)
    for k in range(LRU_W // PIECE_W):
        proj_y(k)
        for i in range(k * heads_per_piece, (k + 1) * heads_per_piece):
            conv(i)
            fill()
            gate_dots(i)
    for i in range(LRU_HEADS):
        gates(i)
        fill()
        scan(i)
    for piece in filler:
        piece()


def _lru(u3, w_xyg, conv_w, conv_b, wx, bx, wa, ba, ap):
    bsz, seq, _ = u3.shape
    n_slab = LRU_W // LANES

    def row(w):
        return pl.BlockSpec((1, TM, w), lambda b, t: (b, t, 0))

    return pl.pallas_call(
        _lru_kernel,
        name="lru",
        grid=(bsz, seq // TM),
        in_specs=[row(D_MODEL), _const_spec((D_MODEL, 4 * LRU_W)),
                  _const_spec((CONV_W, LRU_W)), _const_spec((1, LRU_W)),
                  _const_spec((LRU_HEADS, LRU_HD, LRU_HD)), _const_spec((1, LRU_W)),
                  _const_spec((LRU_HEADS, LRU_HD, LRU_HD)), _const_spec((1, LRU_W)),
                  _const_spec((1, LRU_W))],
        out_specs=[row(LRU_W), row(2 * D_MODEL)],
        out_shape=[jax.ShapeDtypeStruct((bsz, seq, LRU_W), BF16),
                   jax.ShapeDtypeStruct((bsz, seq, 2 * D_MODEL), BF16)],
        scratch_shapes=[pltpu.VMEM((TM + SUBLANES, LRU_W), F32),
                        pltpu.VMEM((n_slab, SUBLANES * PITCH, LANES), F32),
                        pltpu.VMEM((n_slab, SUBLANES * PITCH, LANES), F32),
                        pltpu.VMEM((TM, LRU_W), F32),
                        pltpu.VMEM((SUBLANES, LRU_W), F32),
                        pltpu.VMEM((n_slab, SUBLANES, LANES), F32)],
        compiler_params=pltpu.CompilerParams(
            dimension_semantics=("arbitrary", "arbitrary"), vmem_limit_bytes=VMEM_LIMIT),
    )(u3, w_xyg, conv_w, conv_b, wx, bx, wa, ba, ap)


def _bucket_table():
    qi = np.arange(BAND)[:, None]
    kj = np.arange(BAND)[None, :]
    max_exact = REL_BUCKETS // 2
    out = np.zeros((N_GROUPS, 2, BAND, BAND), np.int32)
    for g, d in enumerate(DILATIONS):
        for half in range(2):
            steps = qi + BAND - kj if half == 0 else qi - kj
            valid = (steps >= 0) & (steps <= BAND)
            dist = np.maximum(steps, 0) * d
            nf = np.maximum(dist, 1).astype(np.float32)
            large = max_exact + (np.log(nf / np.float32(max_exact))
                                 / np.float32(math.log(REL_MAX_DISTANCE / max_exact))
                                 * np.float32(REL_BUCKETS - max_exact)).astype(np.int32)
            large = np.minimum(large, REL_BUCKETS - 1)
            bucket = np.where(dist < max_exact, dist, large)
            out[g, half] = np.where(valid, bucket, -1)
    return out


def _attn_kernel(group, pp, rows, tab_ref, bkt_ref, u_ref, w_ref, o_ref, st_ref,
                 qbuf, kbuf, vbuf, bias_s):
    first = (pl.program_id(0) == 0) & (pl.program_id(1) == 0) & (pl.program_id(2) == 0)
    c = pl.program_id(2)
    log2e = math.log2(math.e)
    vw = 2 * HEAD_DIM

    @pl.when(first)
    def _():
        for h in range(HEADS_PER_GROUP):
            for half in range(2):
                bk = bkt_ref[half]
                bias = jnp.full((BAND, BAND), -jnp.inf, F32)
                for n in range(REL_BUCKETS):
                    bias = jnp.where(bk == n, tab_ref[n, group * HEADS_PER_GROUP + h] * log2e, bias)
                bias_s[h, :, half * BAND:(half + 1) * BAND] = bias
            for ph in range(pp):
                vbuf[ph, :, h * vw + HEAD_DIM:(h + 1) * vw] = jnp.ones((BAND + rows, HEAD_DIM), BF16)

    @pl.when(c == 0)
    def _():
        for ph in range(pp):
            kbuf[ph, 0:BAND, :] = jnp.zeros((BAND, GROUP_W), BF16)
            for h in range(HEADS_PER_GROUP):
                vbuf[ph, 0:BAND, h * vw:h * vw + HEAD_DIM] = jnp.zeros((BAND, HEAD_DIM), BF16)

    lane2 = lax.broadcasted_iota(jnp.int32, (1, 2 * BAND), 1)
    pen = jnp.where((lane2 < BAND) & (c == 0), -jnp.inf, 0.0).astype(F32)
    qk_scale = log2e / math.sqrt(HEAD_DIM)
    nt = (((1,), (1,)), ((), ()))
    lane = lax.broadcasted_iota(jnp.int32, (BAND, LANES), 1)

    def project(ph, r0):
        qkv = jnp.dot(u_ref[ph, r0:r0 + PROJ_ROWS, :], w_ref[...], preferred_element_type=F32).astype(BF16)
        qbuf[ph, r0:r0 + PROJ_ROWS, :] = qkv[:, :GROUP_W]
        kbuf[ph, BAND + r0:BAND + r0 + PROJ_ROWS, :] = qkv[:, GROUP_W:2 * GROUP_W]
        for h in range(HEADS_PER_GROUP):
            vbuf[ph, BAND + r0:BAND + r0 + PROJ_ROWS, h * vw:h * vw + HEAD_DIM] = (
                qkv[:, 2 * GROUP_W + h * HEAD_DIM:2 * GROUP_W + (h + 1) * HEAD_DIM])

    def units_of(r0):
        return [(n, h) for n in range(r0 // BAND, (r0 + PROJ_ROWS) // BAND) for h in range(HEADS_PER_GROUP)]

    def attend_scores(ph, r0):
        scores = []
        for n, h in units_of(r0):
            cols = slice(h * HEAD_DIM, (h + 1) * HEAD_DIM)
            q = qbuf[ph, n * BAND:(n + 1) * BAND, cols]
            kk = kbuf[ph, n * BAND:(n + 2) * BAND, cols]
            s = lax.dot_general(q, kk, nt, preferred_element_type=F32) * qk_scale + bias_s[h]
            if n == 0:
                s = s + pen
            scores.append(s)
        return scores

    def attend_finish(ph, r0, scores):
        units = units_of(r0)
        maxes = [jnp.max(s, axis=-1, keepdims=True) for s in scores]
        probs = [jnp.exp2(s - m).astype(BF16) for s, m in zip(scores, maxes)]
        outs = [jnp.dot(p, vbuf[ph, n * BAND:(n + 2) * BAND, h * vw:(h + 1) * vw], preferred_element_type=F32)
                for (n, h), p in zip(units, probs)]
        stats = None
        for (n, h), o, m in zip(units, outs, maxes):
            o_ref[ph, n * BAND:(n + 1) * BAND, h * HEAD_DIM:(h + 1) * HEAD_DIM] = o[:, :HEAD_DIM].astype(BF16)
            den = o[:, HEAD_DIM:]
            stats = jnp.broadcast_to(m, (BAND, LANES)) if h == 0 else jnp.where(lane == h, m, stats)
            stats = jnp.where(lane == HEADS_PER_GROUP + h, den, stats)
            if h == HEADS_PER_GROUP - 1:
                st_ref[ph, n * BAND:(n + 1) * BAND, :] = stats

    sub_blocks = [(ph, r0) for ph in range(pp) for r0 in range(0, rows, PROJ_ROWS)]
    for sb in sub_blocks[:2]:
        project(*sb)
    for i, sb in enumerate(sub_blocks):
        scores = attend_scores(*sb)
        if i + 2 < len(sub_blocks):
            project(*sub_blocks[i + 2])
        attend_finish(*sb, scores)

    for ph in range(pp):
        kbuf[ph, 0:BAND, :] = kbuf[ph, rows:rows + BAND, :]
        vbuf[ph, 0:BAND, :] = vbuf[ph, rows:rows + BAND, :]


def _attn_group(group, u_g, w_qkv, table, bkt):
    bsz, d, sub, _ = u_g.shape
    rows = min(ATT_R, sub)
    pp = min(d, ATT_R // rows)
    assert sub % rows == 0 and d % pp == 0 and (pp == 1 or rows == sub) and rows % PROJ_ROWS == 0

    def spec(width):
        return pl.BlockSpec((None, pp, rows, width), lambda b, p, c: (b, p, c, 0))

    return pl.pallas_call(
        functools.partial(_attn_kernel, group, pp, rows),
        name=f"attn{group}",
        grid=(bsz, d // pp, sub // rows),
        in_specs=[pl.BlockSpec(memory_space=pltpu.SMEM),
                  pl.BlockSpec((None, 2, BAND, BAND), lambda b, p, c: (group, 0, 0, 0)),
                  spec(D_MODEL),
                  pl.BlockSpec((D_MODEL, 3 * GROUP_W), lambda b, p, c: (0, group), pipeline_mode=pl.Buffered(1))],
        out_specs=[spec(GROUP_W), spec(LANES)],
        out_shape=[jax.ShapeDtypeStruct((bsz, d, sub, GROUP_W), BF16),
                   jax.ShapeDtypeStruct((bsz, d, sub, LANES), F32)],
        scratch_shapes=[pltpu.VMEM((pp, rows, GROUP_W), BF16),
                        pltpu.VMEM((pp, BAND + rows, GROUP_W), BF16),
                        pltpu.VMEM((pp, BAND + rows, 2 * GROUP_W), BF16),
                        pltpu.VMEM((HEADS_PER_GROUP, BAND, 2 * BAND), F32)],
        compiler_params=pltpu.CompilerParams(
            dimension_semantics=("arbitrary", "arbitrary", "arbitrary"),
            vmem_limit_bytes=VMEM_LIMIT),
    )(table, bkt, u_g, w_qkv)


def _merge_kernel(h_ref, o0_ref, o1_ref, o2_ref, l0_ref, l1_ref, l2_ref, rec_ref, gl_ref,
                  wat_ref, wrec_ref, wout_ref, gpost_ref, out_ref, osc, lsc):
    rec_d = jnp.dot(rec_ref[...], wrec_ref[...], preferred_element_type=F32)
    for gi, o_ref, l_ref in ((1, o1_ref, l1_ref), (2, o2_ref, l2_ref)):
        d = DILATIONS[gi]
        for p in range(d):
            idx = pl.ds(p, TM // d, stride=d)
            lsc[gi - 1, idx, :] = l_ref[p]
            for h in range(HEADS_PER_GROUP):
                osc[(gi - 1) * HEADS_PER_GROUP + h, idx, :] = (
                    o_ref[p, :, h * HEAD_DIM:(h + 1) * HEAD_DIM].astype(F32))
    st = (l0_ref[0], lsc[0], lsc[1])
    mx = jnp.maximum(jnp.maximum(st[0], st[1]), st[2])
    e = [jnp.exp2(x - mx) for x in st]
    den = [pltpu.roll(x, LANES - HEADS_PER_GROUP, 1) for x in st]
    inv = 1.0 / (e[0] * den[0] + e[1] * den[1] + e[2] * den[2])
    w0, w1, w2 = e[0] * inv, e[1] * inv, e[2] * inv
    parts = []
    for h in range(HEADS_PER_GROUP):
        cols = slice(h * HEAD_DIM, (h + 1) * HEAD_DIM)
        parts.append(w0[:, h:h + 1] * o0_ref[0, :, cols].astype(F32)
                     + w1[:, h:h + 1] * osc[h]
                     + w2[:, h:h + 1] * osc[HEADS_PER_GROUP + h])
    attn = jnp.concatenate(parts, axis=-1).astype(BF16)
    attn_d = jnp.dot(attn, wat_ref[...], preferred_element_type=F32)
    gl = gl_ref[...].astype(F32)
    merged = _sigmoid(gl[:, :D_MODEL]) * attn_d + _sigmoid(gl[:, D_MODEL:]) * rec_d
    mo = jnp.dot(merged.astype(BF16), wout_ref[...], preferred_element_type=F32)
    out_ref[...] = h_ref[...] + _rms(mo, gpost_ref[...])


def _merge(h3, o_list, lse_list, rec3, gl3, w_attn, w_rec, w_out, g_post):
    bsz, seq, _ = h3.shape

    def row(w):
        return pl.BlockSpec((None, TM, w), lambda b, i: (b, i, 0))

    def phased(d, w):
        return pl.BlockSpec((None, d, TM // d, w), lambda b, i: (b, 0, i, 0))

    return pl.pallas_call(
        _merge_kernel,
        name="merge",
        grid=(bsz, seq // TM),
        in_specs=[row(D_MODEL)] + [phased(d, GROUP_W) for d in DILATIONS]
                 + [phased(d, LANES) for d in DILATIONS] + [row(LRU_W), row(2 * D_MODEL),
                 _const_spec((GROUP_W, D_MODEL)), _const_spec((LRU_W, D_MODEL)),
                 _const_spec((D_MODEL, D_MODEL)), _const_spec((1, D_MODEL))],
        out_specs=row(D_MODEL),
        out_shape=jax.ShapeDtypeStruct((bsz, seq, D_MODEL), F32),
        scratch_shapes=[pltpu.VMEM((2 * HEADS_PER_GROUP, TM, LANES), F32),
                        pltpu.VMEM((2, TM, LANES), F32)],
        compiler_params=pltpu.CompilerParams(
            dimension_semantics=("arbitrary", "arbitrary"), vmem_limit_bytes=VMEM_LIMIT),
    )(h3, *o_list, *lse_list, rec3, gl3, w_attn, w_rec, w_out, g_post)


def kernel(x, ffn1_norm_pre, ffn1_norm_post, ffn1_w_gate, ffn1_w_up, ffn1_w_down, mix_norm_pre, mix_norm_post, w_in, rel_bias_table, conv_w, conv_b, lru_w_x, lru_b_x, lru_w_a, lru_b_a, lru_a_param, w_attn_branch, w_rec_branch, w_out, ffn2_norm_pre, ffn2_norm_post, ffn2_w_gate, ffn2_w_up, ffn2_w_down):
    bsz, seq, _ = x.shape
    n = bsz * seq
    depth = w_in.shape[0]
    bkt = jnp.asarray(_bucket_table())
    h = x.reshape(n, D_MODEL)
    for l in range(depth):
        gw = 3 * GROUP_W
        regroup = tuple((g * gw + j * GROUP_W, j * QKV_W + g * GROUP_W, GROUP_W)
                        for g in range(N_GROUPS) for j in range(3))
        jobs = [_CastJob(w_in[l], ((3 * QKV_W, regroup), (4 * LRU_W, ((0, 3 * QKV_W, 4 * LRU_W),)))),
                _plain_cast(lru_w_x[l].reshape(LRU_W, LRU_HD)), _plain_cast(lru_w_a[l].reshape(LRU_W, LRU_HD)),
                _plain_cast(w_attn_branch[l]), _plain_cast(w_rec_branch[l]), _plain_cast(w_out[l]),
                _plain_cast(ffn2_w_gate[l]), _plain_cast(ffn2_w_up[l]), _plain_cast(ffn2_w_down[l])]
        (h, u, u_d4, u_d16, w_qkv, w_xyg, wx, wa, w_attn, w_rec, w_o, w2_gate, w2_up, w2_down) = _ffn(
            h, ffn1_norm_pre[l][None], ffn1_norm_post[l][None],
            ffn1_w_gate[l].astype(BF16), ffn1_w_up[l].astype(BF16), ffn1_w_down[l].astype(BF16),
            g_next=mix_norm_pre[l][None], seq=seq, cast_jobs=jobs)
        h3 = h.reshape(bsz, seq, D_MODEL)
        u3 = u.reshape(bsz, seq, D_MODEL)

        rec, gl = _lru(u3, w_xyg, conv_w[l], conv_b[l][None],
                       wx.reshape(LRU_HEADS, LRU_HD, LRU_HD), lru_b_x[l].reshape(1, LRU_W),
                       wa.reshape(LRU_HEADS, LRU_HD, LRU_HD), lru_b_a[l].reshape(1, LRU_W),
                       lru_a_param[l][None])
        o_list, lse_list = [], []
        for g, u_g in enumerate((u3[:, None], u_d4, u_d16)):
            o_g, lse_g = _attn_group(g, u_g, w_qkv, rel_bias_table, bkt)
            o_list.append(o_g)
            lse_list.append(lse_g)
        h3 = _merge(h3, o_list, lse_list, rec, gl, w_attn, w_rec, w_o, mix_norm_post[l][None])

        h, = _ffn(h3.reshape(n, D_MODEL), ffn2_norm_pre[l][None], ffn2_norm_post[l][None],
                  w2_gate, w2_up, w2_down)
    return h.reshape(bsz, seq, D_MODEL)
```

```python
import functools
import math
from typing import NamedTuple

import numpy as np
import jax
import jax.numpy as jnp
from jax import lax
from jax.experimental import pallas as pl
from jax.experimental.pallas import tpu as pltpu

F32 = jnp.float32
BF16 = jnp.bfloat16

D_MODEL = 1024
HEAD_DIM = 128
HEADS_PER_GROUP = 4
DILATIONS = (1, 4, 16)
BAND = 128
N_GROUPS = 3
GROUP_W = HEADS_PER_GROUP * HEAD_DIM
QKV_W = N_GROUPS * GROUP_W
LRU_W = D_MODEL
LRU_HEADS = 4
LRU_HD = LRU_W // LRU_HEADS
CONV_W = 4
LRU_C = 8.0
D_FF = 2816
REL_BUCKETS = 32
REL_MAX_DISTANCE = 2048
EPS = 1e-6
TINY = 1e-30

LANES = 128
SUBLANES = 8
BF16_TILE_ROWS = 16
MAX_CAST_BLOCKS = 64
VMEM_LIMIT = 56 * 1024 * 1024

TM_FFN = 512
TM = 512
ATT_R = 2048
SEG = TM // SUBLANES
PITCH = SEG + SUBLANES
PIECE_W = 256
PROJ_ROWS = 256
PHASE_STEP = 4
assert DILATIONS == (1, PHASE_STEP, PHASE_STEP * PHASE_STEP)


def _rms(x, g):
    ms = jnp.mean(x * x, axis=-1, keepdims=True)
    return x * lax.rsqrt(ms + EPS) * g


def _sigmoid(x):
    return 0.5 + 0.5 * jnp.tanh(0.5 * x)


def _gelu_tanh(x):
    c = math.sqrt(2.0 / math.pi)
    inner = x * (c + (c * 0.044715) * (x * x))
    return (0.5 * x) * (1.0 + jnp.tanh(inner))


def _const_spec(shape):
    nd = len(shape)
    return pl.BlockSpec(shape, lambda *_: (0,) * nd, pipeline_mode=pl.Buffered(1))


class _CastJob(NamedTuple):
    src: jax.Array
    outs: tuple

    @property
    def block_rows(self):
        rows = self.src.shape[0]
        br = BF16_TILE_ROWS
        while rows % br or rows // br > MAX_CAST_BLOCKS:
            br += BF16_TILE_ROWS
        return br


def _plain_cast(w):
    return _CastJob(w, ((w.shape[1], ((0, 0, w.shape[1]),)),))


def _ffn_kernel(emit_normed, jobs_pieces, x_ref, gpre_ref, gpost_ref, wg_ref, wu_ref, wd_ref, *rest):
    n_jobs = len(jobs_pieces)
    n_normed = len(DILATIONS) if emit_normed else 0
    if emit_normed:
        gnext_ref, rest = rest[0], rest[1:]
    cast_in, o_ref = rest[:n_jobs], rest[n_jobs]
    normed_refs = rest[n_jobs + 1:n_jobs + 1 + n_normed]
    n_cast_out = sum(len(p) for p in jobs_pieces)
    cast_out = rest[n_jobs + 1 + n_normed:n_jobs + 1 + n_normed + n_cast_out]
    scratch = rest[n_jobs + 1 + n_normed + n_cast_out:]
    x = x_ref[...]
    xn = _rms(x, gpre_ref[...]).astype(BF16)
    g = jnp.dot(xn, wg_ref[...], preferred_element_type=F32)
    u = jnp.dot(xn, wu_ref[...], preferred_element_type=F32)
    a = (g * _sigmoid(g) * u).astype(BF16)
    f = jnp.dot(a, wd_ref[...], preferred_element_type=F32)
    h = x + 0.5 * _rms(f, gpost_ref[...])
    o_ref[...] = h
    if emit_normed:
        slab, slab4 = scratch
        u4_ref, u16_ref = normed_refs[1:]
        hn = _rms(h, gnext_ref[...])
        normed_refs[0][...] = hn.astype(BF16)
        r4 = TM_FFN // PHASE_STEP
        for c in range(D_MODEL // LANES):
            lanes = slice(c * LANES, (c + 1) * LANES)
            slab[c] = hn[:, lanes]
            for p in range(PHASE_STEP):
                part = slab[c, pl.ds(p, r4, stride=PHASE_STEP), :]
                u4_ref[p, :, lanes] = part.astype(BF16)
                slab4[c, p * r4:(p + 1) * r4, :] = part
            for p in range(PHASE_STEP):
                for q in range(PHASE_STEP):
                    u16_ref[q * PHASE_STEP + p, :, lanes] = (
                        slab4[c, pl.ds(p * r4 + q, r4 // PHASE_STEP, stride=PHASE_STEP), :].astype(BF16))
    outs = iter(cast_out)
    for src_ref, pieces_per_out in zip(cast_in, jobs_pieces):
        for pieces in pieces_per_out:
            dst_ref = next(outs)
            for dst_col, src_col, width in pieces:
                dst_ref[:, dst_col:dst_col + width] = src_ref[:, src_col:src_col + width].astype(BF16)


def _ffn(x2, g_pre, g_post, wg, wu, wd, g_next=None, seq=None, cast_jobs=()):
    n = x2.shape[0]
    steps = n // TM_FFN
    assert steps <= MAX_CAST_BLOCKS
    emit_normed = g_next is not None
    row = pl.BlockSpec((TM_FFN, D_MODEL), lambda i: (i, 0))
    in_specs = [row, _const_spec((1, D_MODEL)), _const_spec((1, D_MODEL)),
                _const_spec((D_MODEL, D_FF)), _const_spec((D_MODEL, D_FF)), _const_spec((D_FF, D_MODEL))]
    operands = [x2, g_pre, g_post, wg, wu, wd]
    out_specs = [row]
    out_shape = [jax.ShapeDtypeStruct((n, D_MODEL), F32)]
    scratch_shapes = []
    if emit_normed:
        tiles = seq // TM_FFN
        in_specs.append(_const_spec((1, D_MODEL)))
        operands.append(g_next)
        out_specs.append(row)
        out_shape.append(jax.ShapeDtypeStruct((n, D_MODEL), BF16))
        for d in DILATIONS[1:]:
            out_specs.append(pl.BlockSpec((None, d, TM_FFN // d, D_MODEL),
                                          functools.partial(lambda i, t: (i // t, 0, i % t, 0), t=tiles)))
            out_shape.append(jax.ShapeDtypeStruct((n // seq, d, seq // d, D_MODEL), BF16))
        scratch_shapes += [pltpu.VMEM((D_MODEL // LANES, TM_FFN, LANES), F32)] * 2
    for job in cast_jobs:
        rows, br = job.src.shape[0], job.block_rows
        index_map = functools.partial(lambda i, last: (jnp.minimum(i, last), 0), last=rows // br - 1)
        in_specs.append(pl.BlockSpec((br, job.src.shape[1]), index_map))
        operands.append(job.src)
        for cols, _ in job.outs:
            out_specs.append(pl.BlockSpec((br, cols), index_map))
            out_shape.append(jax.ShapeDtypeStruct((rows, cols), BF16))
    jobs_pieces = tuple(tuple(pieces for _, pieces in job.outs) for job in cast_jobs)
    return pl.pallas_call(
        functools.partial(_ffn_kernel, emit_normed, jobs_pieces),
        name="ffn",
        grid=(steps,),
        in_specs=in_specs,
        out_specs=out_specs,
        out_shape=out_shape,
        scratch_shapes=scratch_shapes,
        compiler_params=pltpu.CompilerParams(
            dimension_semantics=("arbitrary",), vmem_limit_bytes=VMEM_LIMIT),
    )(*operands)


def _lru_kernel(u_ref, wlru_ref, cw_ref, cb_ref, wx_ref, bx_ref, wa_ref, ba_ref, ap_ref,
                rec_ref, gl_ref, xbuf, a_s, b_s, gy_s, hcar, cin_s):
    @pl.when(pl.program_id(1) == 0)
    def _():
        xbuf[0:SUBLANES, :] = jnp.zeros((SUBLANES, LRU_W), F32)
        hcar[...] = jnp.zeros(hcar.shape, F32)

    slabs_per_head = LRU_HD // LANES
    ap = -ap_ref[...]
    quarter_c_softplus = (0.25 * LRU_C) * (jnp.maximum(ap, 0.0) + jnp.log1p(jnp.exp(-jnp.abs(ap))))
    state = {}

    def proj_x(k):
        cols = slice(k * PIECE_W, (k + 1) * PIECE_W)
        xbuf[SUBLANES:SUBLANES + TM, cols] = jnp.dot(u_ref[0], wlru_ref[:, cols], preferred_element_type=F32)

    def proj_y(k):
        cols = slice(k * PIECE_W, (k + 1) * PIECE_W)
        yr = jnp.dot(u_ref[0], wlru_ref[:, LRU_W + k * PIECE_W:LRU_W + (k + 1) * PIECE_W],
                     preferred_element_type=F32)
        gy_s[:, cols] = _gelu_tanh(yr)

    def proj_gl(k):
        cols = slice(k * PIECE_W, (k + 1) * PIECE_W)
        gl_ref[0, :, cols] = jnp.dot(u_ref[0], wlru_ref[:, 2 * LRU_W + k * PIECE_W:2 * LRU_W + (k + 1) * PIECE_W],
                                     preferred_element_type=F32).astype(BF16)

    def gate_dots(i):
        xh = state["xcb", i]
        state["pre_x", i] = jnp.dot(xh, wx_ref[i], preferred_element_type=F32)
        state["pre_a", i] = jnp.dot(xh, wa_ref[i], preferred_element_type=F32)

    def conv(i):
        cols = slice(i * LRU_HD, (i + 1) * LRU_HD)
        xc = cb_ref[:, cols]
        for j in range(CONV_W):
            off = SUBLANES - (CONV_W - 1) + j
            xc = xc + cw_ref[j:j + 1, cols] * xbuf[off:off + TM, cols]
        xbuf[0:SUBLANES, cols] = xbuf[TM:TM + SUBLANES, cols]
        state["xc", i] = xc
        state["xcb", i] = xc.astype(BF16)

    def gates(i):
        cols = slice(i * LRU_HD, (i + 1) * LRU_HD)
        tx = jnp.tanh(0.5 * (state["pre_x", i] + bx_ref[:, cols]))
        ta = jnp.tanh(0.5 * (state["pre_a", i] + ba_ref[:, cols]))
        qc = quarter_c_softplus[:, cols]
        t = jnp.tanh(qc + qc * ta)
        r = 1.0 / (1.0 + t)
        a = (1.0 - t) * r
        b = ((t * lax.rsqrt(jnp.maximum(t, TINY))) * r) * ((1.0 + tx) * state["xc", i])
        for k in range(slabs_per_head):
            c = i * slabs_per_head + k
            for s in range(SUBLANES):
                a_s[c, s * PITCH:s * PITCH + SEG, :] = a[s * SEG:(s + 1) * SEG, k * LANES:(k + 1) * LANES]
                b_s[c, s * PITCH:s * PITCH + SEG, :] = b[s * SEG:(s + 1) * SEG, k * LANES:(k + 1) * LANES]

    def scan(i):
        slab_ids = range(i * slabs_per_head, (i + 1) * slabs_per_head)
        hs = {c: jnp.zeros((SUBLANES, LANES), F32) for c in slab_ids}
        cum = {c: jnp.ones((SUBLANES, LANES), F32) for c in slab_ids}
        for j in range(SEG):
            idx = pl.ds(j, SUBLANES, stride=PITCH)
            for c in slab_ids:
                aj = a_s[c, idx, :]
                hs[c] = aj * hs[c] + b_s[c, idx, :]
                cum[c] = aj * cum[c]
                a_s[c, idx, :] = cum[c]
                b_s[c, idx, :] = hs[c]
        row = lax.broadcasted_iota(jnp.int32, (SUBLANES, LANES), 0)
        for c in slab_ids:
            cols = slice(c * LANES, (c + 1) * LANES)
            h_in = jnp.broadcast_to(hcar[SUBLANES - 1:SUBLANES, cols], (SUBLANES, LANES))
            cin = h_in
            for _ in range(SUBLANES - 1):
                out = hs[c] + cum[c] * cin
                cin = jnp.where(row == 0, h_in, pltpu.roll(out, 1, 0))
            cin_s[c] = cin
            hcar[:, cols] = hs[c] + cum[c] * cin
        for c in slab_ids:
            cols = slice(c * LANES, (c + 1) * LANES)
            for s in range(SUBLANES):
                rows = slice(s * SEG, (s + 1) * SEG)
                prow = slice(s * PITCH, s * PITCH + SEG)
                h = b_s[c, prow, :] + a_s[c, prow, :] * cin_s[c, s:s + 1, :]
                rec_ref[0, rows, cols] = (h * gy_s[rows, cols]).astype(BF16)

    heads_per_piece = PIECE_W // LRU_HD
    filler = iter([functools.partial(proj_gl, k) for k in range(2 * D_MODEL // PIECE_W)])

    def fill():
        piece = next(filler, None)
        if piece is not None:
            piece()

    for k in range(LRU_W // PIECE_W):
        proj_x(k)
    for k in range(LRU_W // PIECE_W):
        proj_y(k)
        for i in range(k * heads_per_piece, (k + 1) * heads_per_piece):
            conv(i)
            fill()
            gate_dots(i)
    for i in range(LRU_HEADS):
        gates(i)
        fill()
        scan(i)
    for piece in filler:
        piece()


def _lru(u3, w_xyg, conv_w, conv_b, wx, bx, wa, ba, ap):
    bsz, seq, _ = u3.shape
    n_slab = LRU_W // LANES

    def row(w):
        return pl.BlockSpec((1, TM, w), lambda b, t: (b, t, 0))

    return pl.pallas_call(
        _lru_kernel,
        name="lru",
        grid=(bsz, seq // TM),
        in_specs=[row(D_MODEL), _const_spec((D_MODEL, 4 * LRU_W)),
                  _const_spec((CONV_W, LRU_W)), _const_spec((1, LRU_W)),
                  _const_spec((LRU_HEADS, LRU_HD, LRU_HD)), _const_spec((1, LRU_W)),
                  _const_spec((LRU_HEADS, LRU_HD, LRU_HD)), _const_spec((1, LRU_W)),
                  _const_spec((1, LRU_W))],
        out_specs=[row(LRU_W), row(2 * D_MODEL)],
        out_shape=[jax.ShapeDtypeStruct((bsz, seq, LRU_W), BF16),
                   jax.ShapeDtypeStruct((bsz, seq, 2 * D_MODEL), BF16)],
        scratch_shapes=[pltpu.VMEM((TM + SUBLANES, LRU_W), F32),
                        pltpu.VMEM((n_slab, SUBLANES * PITCH, LANES), F32),
                        pltpu.VMEM((n_slab, SUBLANES * PITCH, LANES), F32),
                        pltpu.VMEM((TM, LRU_W), F32),
                        pltpu.VMEM((SUBLANES, LRU_W), F32),
                        pltpu.VMEM((n_slab, SUBLANES, LANES), F32)],
        compiler_params=pltpu.CompilerParams(
            dimension_semantics=("arbitrary", "arbitrary"), vmem_limit_bytes=VMEM_LIMIT),
    )(u3, w_xyg, conv_w, conv_b, wx, bx, wa, ba, ap)


def _bucket_table():
    qi = np.arange(BAND)[:, None]
    kj = np.arange(BAND)[None, :]
    max_exact = REL_BUCKETS // 2
    out = np.zeros((N_GROUPS, 2, BAND, BAND), np.int32)
    for g, d in enumerate(DILATIONS):
        for half in range(2):
            steps = qi + BAND - kj if half == 0 else qi - kj
            valid = (steps >= 0) & (steps <= BAND)
            dist = np.maximum(steps, 0) * d
            nf = np.maximum(dist, 1).astype(np.float32)
            large = max_exact + (np.log(nf / np.float32(max_exact))
                                 / np.float32(math.log(REL_MAX_DISTANCE / max_exact))
                                 * np.float32(REL_BUCKETS - max_exact)).astype(np.int32)
            large = np.minimum(large, REL_BUCKETS - 1)
            bucket = np.where(dist < max_exact, dist, large)
            out[g, half] = np.where(valid, bucket, -1)
    return out


def _attn_kernel(group, pp, rows, tab_ref, bkt_ref, u_ref, w_ref, o_ref, st_ref,
                 qbuf, kbuf, vbuf, bias_s):
    first = (pl.program_id(0) == 0) & (pl.program_id(1) == 0) & (pl.program_id(2) == 0)
    c = pl.program_id(2)
    log2e = math.log2(math.e)
    vw = 2 * HEAD_DIM

    @pl.when(first)
    def _():
        for h in range(HEADS_PER_GROUP):
            for half in range(2):
                bk = bkt_ref[half]
                bias = jnp.full((BAND, BAND), -jnp.inf, F32)
                for n in range(REL_BUCKETS):
                    bias = jnp.where(bk == n, tab_ref[n, group * HEADS_PER_GROUP + h] * log2e, bias)
                bias_s[h, :, half * BAND:(half + 1) * BAND] = bias
            for ph in range(pp):
                vbuf[ph, :, h * vw + HEAD_DIM:(h + 1) * vw] = jnp.ones((BAND + rows, HEAD_DIM), BF16)

    @pl.when(c == 0)
    def _():
        for ph in range(pp):
            kbuf[ph, 0:BAND, :] = jnp.zeros((BAND, GROUP_W), BF16)
            for h in range(HEADS_PER_GROUP):
                vbuf[ph, 0:BAND, h * vw:h * vw + HEAD_DIM] = jnp.zeros((BAND, HEAD_DIM), BF16)

    lane2 = lax.broadcasted_iota(jnp.int32, (1, 2 * BAND), 1)
    pen = jnp.where((lane2 < BAND) & (c == 0), -jnp.inf, 0.0).astype(F32)
    qk_scale = log2e / math.sqrt(HEAD_DIM)
    nt = (((1,), (1,)), ((), ()))
    lane = lax.broadcasted_iota(jnp.int32, (BAND, LANES), 1)

    def project(ph, r0):
        qkv = jnp.dot(u_ref[ph, r0:r0 + PROJ_ROWS, :], w_ref[...], preferred_element_type=F32).astype(BF16)
        qbuf[ph, r0:r0 + PROJ_ROWS, :] = qkv[:, :GROUP_W]
        kbuf[ph, BAND + r0:BAND + r0 + PROJ_ROWS, :] = qkv[:, GROUP_W:2 * GROUP_W]
        for h in range(HEADS_PER_GROUP):
            vbuf[ph, BAND + r0:BAND + r0 + PROJ_ROWS, h * vw:h * vw + HEAD_DIM] = (
                qkv[:, 2 * GROUP_W + h * HEAD_DIM:2 * GROUP_W + (h + 1) * HEAD_DIM])

    def units_of(r0):
        return [(n, h) for n in range(r0 // BAND, (r0 + PROJ_ROWS) // BAND) for h in range(HEADS_PER_GROUP)]

    def attend_scores(ph, r0):
        scores = []
        for n, h in units_of(r0):
            cols = slice(h * HEAD_DIM, (h + 1) * HEAD_DIM)
            q = qbuf[ph, n * BAND:(n + 1) * BAND, cols]
            kk = kbuf[ph, n * BAND:(n + 2) * BAND, cols]
            s = lax.dot_general(q, kk, nt, preferred_element_type=F32) * qk_scale + bias_s[h]
            if n == 0:
                s = s + pen
            scores.append(s)
        return scores

    def attend_finish(ph, r0, scores):
        units = units_of(r0)
        maxes = [jnp.max(s, axis=-1, keepdims=True) for s in scores]
        probs = [jnp.exp2(s - m).astype(BF16) for s, m in zip(scores, maxes)]
        outs = [jnp.dot(p, vbuf[ph, n * BAND:(n + 2) * BAND, h * vw:(h + 1) * vw], preferred_element_type=F32)
                for (n, h), p in zip(units, probs)]
        stats = None
        for (n, h), o, m in zip(units, outs, maxes):
            o_ref[ph, n * BAND:(n + 1) * BAND, h * HEAD_DIM:(h + 1) * HEAD_DIM] = o[:, :HEAD_DIM].astype(BF16)
            den = o[:, HEAD_DIM:]
            stats = jnp.broadcast_to(m, (BAND, LANES)) if h == 0 else jnp.where(lane == h, m, stats)
            stats = jnp.where(lane == HEADS_PER_GROUP + h, den, stats)
            if h == HEADS_PER_GROUP - 1:
                st_ref[ph, n * BAND:(n + 1) * BAND, :] = stats

    sub_blocks = [(ph, r0) for ph in range(pp) for r0 in range(0, rows, PROJ_ROWS)]
    for sb in sub_blocks[:2]:
        project(*sb)
    for i, sb in enumerate(sub_blocks):
        scores = attend_scores(*sb)
        if i + 2 < len(sub_blocks):
            project(*sub_blocks[i + 2])
        attend_finish(*sb, scores)

    for ph in range(pp):
        kbuf[ph, 0:BAND, :] = kbuf[ph, rows:rows + BAND, :]
        vbuf[ph, 0:BAND, :] = vbuf[ph, rows:rows + BAND, :]


def _attn_group(group, u_g, w_qkv, table, bkt):
    bsz, d, sub, _ = u_g.shape
    rows = min(ATT_R, sub)
    pp = min(d, ATT_R // rows)
    assert sub % rows == 0 and d % pp == 0 and (pp == 1 or rows == sub) and rows % PROJ_ROWS == 0

    def spec(width):
        return pl.BlockSpec((None, pp, rows, width), lambda b, p, c: (b, p, c, 0))

    return pl.pallas_call(
        functools.partial(_attn_kernel, group, pp, rows),
        name=f"attn{group}",
        grid=(bsz, d // pp, sub // rows),
        in_specs=[pl.BlockSpec(memory_space=pltpu.SMEM),
                  pl.BlockSpec((None, 2, BAND, BAND), lambda b, p, c: (group, 0, 0, 0)),
                  spec(D_MODEL),
                  pl.BlockSpec((D_MODEL, 3 * GROUP_W), lambda b, p, c: (0, group), pipeline_mode=pl.Buffered(1))],
        out_specs=[spec(GROUP_W), spec(LANES)],
        out_shape=[jax.ShapeDtypeStruct((bsz, d, sub, GROUP_W), BF16),
                   jax.ShapeDtypeStruct((bsz, d, sub, LANES), F32)],
        scratch_shapes=[pltpu.VMEM((pp, rows, GROUP_W), BF16),
                        pltpu.VMEM((pp, BAND + rows, GROUP_W), BF16),
                        pltpu.VMEM((pp, BAND + rows, 2 * GROUP_W), BF16),
                        pltpu.VMEM((HEADS_PER_GROUP, BAND, 2 * BAND), F32)],
        compiler_params=pltpu.CompilerParams(
            dimension_semantics=("arbitrary", "arbitrary", "arbitrary"),
            vmem_limit_bytes=VMEM_LIMIT),
    )(table, bkt, u_g, w_qkv)


def _merge_kernel(h_ref, o0_ref, o1_ref, o2_ref, l0_ref, l1_ref, l2_ref, rec_ref, gl_ref,
                  wat_ref, wrec_ref, wout_ref, gpost_ref, out_ref, osc, lsc):
    rec_d = jnp.dot(rec_ref[...], wrec_ref[...], preferred_element_type=F32)
    for gi, o_ref, l_ref in ((1, o1_ref, l1_ref), (2, o2_ref, l2_ref)):
        d = DILATIONS[gi]
        for p in range(d):
            idx = pl.ds(p, TM // d, stride=d)
            lsc[gi - 1, idx, :] = l_ref[p]
            for h in range(HEADS_PER_GROUP):
                osc[(gi - 1) * HEADS_PER_GROUP + h, idx, :] = (
                    o_ref[p, :, h * HEAD_DIM:(h + 1) * HEAD_DIM].astype(F32))
    st = (l0_ref[0], lsc[0], lsc[1])
    mx = jnp.maximum(jnp.maximum(st[0], st[1]), st[2])
    e = [jnp.exp2(x - mx) for x in st]
    den = [pltpu.roll(x, LANES - HEADS_PER_GROUP, 1) for x in st]
    inv = 1.0 / (e[0] * den[0] + e[1] * den[1] + e[2] * den[2])
    w0, w1, w2 = e[0] * inv, e[1] * inv, e[2] * inv
    parts = []
    for h in range(HEADS_PER_GROUP):
        cols = slice(h * HEAD_DIM, (h + 1) * HEAD_DIM)
        parts.append(w0[:, h:h + 1] * o0_ref[0, :, cols].astype(F32)
                     + w1[:, h:h + 1] * osc[h]
                     + w2[:, h:h + 1] * osc[HEADS_PER_GROUP + h])
    attn = jnp.concatenate(parts, axis=-1).astype(BF16)
    attn_d = jnp.dot(attn, wat_ref[...], preferred_element_type=F32)
    tg = jnp.tanh(0.5 * gl_ref[...].astype(F32))
    merged = 0.5 * ((1.0 + tg[:, :D_MODEL]) * attn_d + (1.0 + tg[:, D_MODEL:]) * rec_d)
    mo = jnp.dot(merged.astype(BF16), wout_ref[...], preferred_element_type=F32)
    out_ref[...] = h_ref[...] + _rms(mo, gpost_ref[...])


def _merge(h3, o_list, lse_list, rec3, gl3, w_attn, w_rec, w_out, g_post):
    bsz, seq, _ = h3.shape

    def row(w):
        return pl.BlockSpec((None, TM, w), lambda b, i: (b, i, 0))

    def phased(d, w):
        return pl.BlockSpec((None, d, TM // d, w), lambda b, i: (b, 0, i, 0))

    return pl.pallas_call(
        _merge_kernel,
        name="merge",
        grid=(bsz, seq // TM),
        in_specs=[row(D_MODEL)] + [phased(d, GROUP_W) for d in DILATIONS]
                 + [phased(d, LANES) for d in DILATIONS] + [row(LRU_W), row(2 * D_MODEL),
                 _const_spec((GROUP_W, D_MODEL)), _const_spec((LRU_W, D_MODEL)),
                 _const_spec((D_MODEL, D_MODEL)), _const_spec((1, D_MODEL))],
        out_specs=row(D_MODEL),
        out_shape=jax.ShapeDtypeStruct((bsz, seq, D_MODEL), F32),
        scratch_shapes=[pltpu.VMEM((2 * HEADS_PER_GROUP, TM, LANES), F32),
                        pltpu.VMEM((2, TM, LANES), F32)],
        compiler_params=pltpu.CompilerParams(
            dimension_semantics=("arbitrary", "arbitrary"), vmem_limit_bytes=VMEM_LIMIT),
    )(h3, *o_list, *lse_list, rec3, gl3, w_attn, w_rec, w_out, g_post)


def kernel(x, ffn1_norm_pre, ffn1_norm_post, ffn1_w_gate, ffn1_w_up, ffn1_w_down, mix_norm_pre, mix_norm_post, w_in, rel_bias_table, conv_w, conv_b, lru_w_x, lru_b_x, lru_w_a, lru_b_a, lru_a_param, w_attn_branch, w_rec_branch, w_out, ffn2_norm_pre, ffn2_norm_post, ffn2_w_gate, ffn2_w_up, ffn2_w_down):
    bsz, seq, _ = x.shape
    n = bsz * seq
    depth = w_in.shape[0]
    bkt = jnp.asarray(_bucket_table())
    h = x.reshape(n, D_MODEL)
    for l in range(depth):
        gw = 3 * GROUP_W
        regroup = tuple((g * gw + j * GROUP_W, j * QKV_W + g * GROUP_W, GROUP_W)
                        for g in range(N_GROUPS) for j in range(3))
        jobs = [_CastJob(w_in[l], ((3 * QKV_W, regroup), (4 * LRU_W, ((0, 3 * QKV_W, 4 * LRU_W),)))),
                _plain_cast(lru_w_x[l].reshape(LRU_W, LRU_HD)), _plain_cast(lru_w_a[l].reshape(LRU_W, LRU_HD)),
                _plain_cast(w_attn_branch[l]), _plain_cast(w_rec_branch[l]), _plain_cast(w_out[l]),
                _plain_cast(ffn2_w_gate[l]), _plain_cast(ffn2_w_up[l]), _plain_cast(ffn2_w_down[l])]
        (h, u, u_d4, u_d16, w_qkv, w_xyg, wx, wa, w_attn, w_rec, w_o, w2_gate, w2_up, w2_down) = _ffn(
            h, ffn1_norm_pre[l][None], ffn1_norm_post[l][None],
            ffn1_w_gate[l].astype(BF16), ffn1_w_up[l].astype(BF16), ffn1_w_down[l].astype(BF16),
            g_next=mix_norm_pre[l][None], seq=seq, cast_jobs=jobs)
        h3 = h.reshape(bsz, seq, D_MODEL)
        u3 = u.reshape(bsz, seq, D_MODEL)

        rec, gl = _lru(u3, w_xyg, conv_w[l], conv_b[l][None],
                       wx.reshape(LRU_HEADS, LRU_HD, LRU_HD), lru_b_x[l].reshape(1, LRU_W),
                       wa.reshape(LRU_HEADS, LRU_HD, LRU_HD), lru_b_a[l].reshape(1, LRU_W),
                       lru_a_param[l][None])
        o_list, lse_list = [], []
        for g, u_g in enumerate((u3[:, None], u_d4, u_d16)):
            o_g, lse_g = _attn_group(g, u_g, w_qkv, rel_bias_table, bkt)
            o_list.append(o_g)
            lse_list.append(lse_g)
        h3 = _merge(h3, o_list, lse_list, rec, gl, w_attn, w_rec, w_o, mix_norm_post[l][None])

        h, = _ffn(h3.reshape(n, D_MODEL), ffn2_norm_pre[l][None], ffn2_norm_post[l][None],
                  w2_gate, w2_up, w2_down)
    return h.reshape(bsz, seq, D_MODEL)
```

```python
import functools
import math
from typing import NamedTuple

import numpy as np
import jax
import jax.numpy as jnp
from jax import lax
from jax.experimental import pallas as pl
from jax.experimental.pallas import tpu as pltpu

F32 = jnp.float32
BF16 = jnp.bfloat16

D_MODEL = 1024
HEAD_DIM = 128
HEADS_PER_GROUP = 4
DILATIONS = (1, 4, 16)
BAND = 128
N_GROUPS = 3
GROUP_W = HEADS_PER_GROUP * HEAD_DIM
QKV_W = N_GROUPS * GROUP_W
LRU_W = D_MODEL
LRU_HEADS = 4
LRU_HD = LRU_W // LRU_HEADS
CONV_W = 4
LRU_C = 8.0
D_FF = 2816
REL_BUCKETS = 32
REL_MAX_DISTANCE = 2048
EPS = 1e-6
TINY = 1e-30

LANES = 128
SUBLANES = 8
BF16_TILE_ROWS = 16
MAX_CAST_BLOCKS = 64
VMEM_LIMIT = 56 * 1024 * 1024

TM_FFN = 512
TM = 512
ATT_R = 2048
SEG = TM // SUBLANES
PITCH = SEG + SUBLANES
PIECE_W = 256
PROJ_ROWS = 256
PHASE_STEP = 4
assert DILATIONS == (1, PHASE_STEP, PHASE_STEP * PHASE_STEP)


def _rms(x, g):
    ms = jnp.mean(x * x, axis=-1, keepdims=True)
    return x * lax.rsqrt(ms + EPS) * g


def _gelu_tanh(x):
    c = math.sqrt(2.0 / math.pi)
    inner = x * (c + (c * 0.044715) * (x * x))
    return (0.5 * x) * (1.0 + jnp.tanh(inner))


def _const_spec(shape):
    nd = len(shape)
    return pl.BlockSpec(shape, lambda *_: (0,) * nd, pipeline_mode=pl.Buffered(1))


class _CastJob(NamedTuple):
    src: jax.Array
    outs: tuple

    @property
    def block_rows(self):
        rows = self.src.shape[0]
        br = BF16_TILE_ROWS
        while rows % br or rows // br > MAX_CAST_BLOCKS:
            br += BF16_TILE_ROWS
        return br


def _plain_cast(w):
    return _CastJob(w, ((w.shape[1], ((0, 0, w.shape[1]),)),))


def _ffn_kernel(emit_normed, jobs_pieces, x_ref, gpre_ref, gpost_ref, wg_ref, wu_ref, wd_ref, *rest):
    n_jobs = len(jobs_pieces)
    n_normed = len(DILATIONS) if emit_normed else 0
    if emit_normed:
        gnext_ref, rest = rest[0], rest[1:]
    cast_in, o_ref = rest[:n_jobs], rest[n_jobs]
    normed_refs = rest[n_jobs + 1:n_jobs + 1 + n_normed]
    n_cast_out = sum(len(p) for p in jobs_pieces)
    cast_out = rest[n_jobs + 1 + n_normed:n_jobs + 1 + n_normed + n_cast_out]
    scratch = rest[n_jobs + 1 + n_normed + n_cast_out:]
    x = x_ref[...]
    xn = _rms(x, gpre_ref[...]).astype(BF16)
    g = jnp.dot(xn, wg_ref[...], preferred_element_type=F32)
    u = jnp.dot(xn, wu_ref[...], preferred_element_type=F32)
    hg = 0.5 * g
    a = ((hg * u) * (1.0 + jnp.tanh(hg))).astype(BF16)
    half = TM_FFN // 2
    halves = []
    for r0 in (0, half):
        f = jnp.dot(a[r0:r0 + half], wd_ref[...], preferred_element_type=F32)
        halves.append(x[r0:r0 + half] + 0.5 * _rms(f, gpost_ref[...]))
        o_ref[r0:r0 + half, :] = halves[-1]
    h = jnp.concatenate(halves, axis=0)
    if emit_normed:
        slab, slab4 = scratch
        u4_ref, u16_ref = normed_refs[1:]
        hn = _rms(h, gnext_ref[...])
        normed_refs[0][...] = hn.astype(BF16)
        r4 = TM_FFN // PHASE_STEP
        for c in range(D_MODEL // LANES):
            lanes = slice(c * LANES, (c + 1) * LANES)
            slab[c] = hn[:, lanes]
            for p in range(PHASE_STEP):
                part = slab[c, pl.ds(p, r4, stride=PHASE_STEP), :]
                u4_ref[p, :, lanes] = part.astype(BF16)
                slab4[c, p * r4:(p + 1) * r4, :] = part
            for p in range(PHASE_STEP):
                for q in range(PHASE_STEP):
                    u16_ref[q * PHASE_STEP + p, :, lanes] = (
                        slab4[c, pl.ds(p * r4 + q, r4 // PHASE_STEP, stride=PHASE_STEP), :].astype(BF16))
    outs = iter(cast_out)
    for src_ref, pieces_per_out in zip(cast_in, jobs_pieces):
        for pieces in pieces_per_out:
            dst_ref = next(outs)
            for dst_col, src_col, width in pieces:
                dst_ref[:, dst_col:dst_col + width] = src_ref[:, src_col:src_col + width].astype(BF16)


def _ffn(x2, g_pre, g_post, wg, wu, wd, g_next=None, seq=None, cast_jobs=()):
    n = x2.shape[0]
    steps = n // TM_FFN
    assert steps <= MAX_CAST_BLOCKS
    emit_normed = g_next is not None
    row = pl.BlockSpec((TM_FFN, D_MODEL), lambda i: (i, 0))
    in_specs = [row, _const_spec((1, D_MODEL)), _const_spec((1, D_MODEL)),
                _const_spec((D_MODEL, D_FF)), _const_spec((D_MODEL, D_FF)), _const_spec((D_FF, D_MODEL))]
    operands = [x2, g_pre, g_post, wg, wu, wd]
    out_specs = [row]
    out_shape = [jax.ShapeDtypeStruct((n, D_MODEL), F32)]
    scratch_shapes = []
    if emit_normed:
        tiles = seq // TM_FFN
        in_specs.append(_const_spec((1, D_MODEL)))
        operands.append(g_next)
        out_specs.append(row)
        out_shape.append(jax.ShapeDtypeStruct((n, D_MODEL), BF16))
        for d in DILATIONS[1:]:
            out_specs.append(pl.BlockSpec((None, d, TM_FFN // d, D_MODEL),
                                          functools.partial(lambda i, t: (i // t, 0, i % t, 0), t=tiles)))
            out_shape.append(jax.ShapeDtypeStruct((n // seq, d, seq // d, D_MODEL), BF16))
        scratch_shapes += [pltpu.VMEM((D_MODEL // LANES, TM_FFN, LANES), F32)] * 2
    for job in cast_jobs:
        rows, br = job.src.shape[0], job.block_rows
        index_map = functools.partial(lambda i, last: (jnp.minimum(i, last), 0), last=rows // br - 1)
        in_specs.append(pl.BlockSpec((br, job.src.shape[1]), index_map))
        operands.append(job.src)
        for cols, _ in job.outs:
            out_specs.append(pl.BlockSpec((br, cols), index_map))
            out_shape.append(jax.ShapeDtypeStruct((rows, cols), BF16))
    jobs_pieces = tuple(tuple(pieces for _, pieces in job.outs) for job in cast_jobs)
    return pl.pallas_call(
        functools.partial(_ffn_kernel, emit_normed, jobs_pieces),
        name="ffn",
        grid=(steps,),
        in_specs=in_specs,
        out_specs=out_specs,
        out_shape=out_shape,
        scratch_shapes=scratch_shapes,
        compiler_params=pltpu.CompilerParams(
            dimension_semantics=("arbitrary",), vmem_limit_bytes=VMEM_LIMIT),
    )(*operands)


def _lru_kernel(u_ref, wlru_ref, cw_ref, cb_ref, wx_ref, bx_ref, wa_ref, ba_ref, ap_ref,
                rec_ref, gl_ref, xbuf, a_s, b_s, gy_s, hcar, cin_s):
    @pl.when(pl.program_id(1) == 0)
    def _():
        xbuf[0:SUBLANES, :] = jnp.zeros((SUBLANES, LRU_W), F32)
        hcar[...] = jnp.zeros(hcar.shape, F32)

    slabs_per_head = LRU_HD // LANES
    ap = -ap_ref[...]
    quarter_c_softplus = (0.25 * LRU_C) * (jnp.maximum(ap, 0.0) + jnp.log1p(jnp.exp(-jnp.abs(ap))))
    state = {}

    def proj_x(k):
        cols = slice(k * PIECE_W, (k + 1) * PIECE_W)
        xbuf[SUBLANES:SUBLANES + TM, cols] = jnp.dot(u_ref[0], wlru_ref[:, cols], preferred_element_type=F32)

    def proj_y(k):
        cols = slice(k * PIECE_W, (k + 1) * PIECE_W)
        yr = jnp.dot(u_ref[0], wlru_ref[:, LRU_W + k * PIECE_W:LRU_W + (k + 1) * PIECE_W],
                     preferred_element_type=F32)
        gy_s[:, cols] = _gelu_tanh(yr)

    def proj_gl(k):
        cols = slice(k * PIECE_W, (k + 1) * PIECE_W)
        gl_ref[0, :, cols] = jnp.dot(u_ref[0], wlru_ref[:, 2 * LRU_W + k * PIECE_W:2 * LRU_W + (k + 1) * PIECE_W],
                                     preferred_element_type=F32).astype(BF16)

    def gate_dots(i):
        xh = state["xcb", i]
        state["pre_x", i] = jnp.dot(xh, wx_ref[i], preferred_element_type=F32)
        state["pre_a", i] = jnp.dot(xh, wa_ref[i], preferred_element_type=F32)

    def conv(i):
        cols = slice(i * LRU_HD, (i + 1) * LRU_HD)
        xc = cb_ref[:, cols]
        for j in range(CONV_W):
            off = SUBLANES - (CONV_W - 1) + j
            xc = xc + cw_ref[j:j + 1, cols] * xbuf[off:off + TM, cols]
        xbuf[0:SUBLANES, cols] = xbuf[TM:TM + SUBLANES, cols]
        state["xc", i] = xc
        state["xcb", i] = xc.astype(BF16)

    def gates(i):
        cols = slice(i * LRU_HD, (i + 1) * LRU_HD)
        tx = jnp.tanh(0.5 * (state["pre_x", i] + bx_ref[:, cols]))
        ta = jnp.tanh(0.5 * (state["pre_a", i] + ba_ref[:, cols]))
        qc = quarter_c_softplus[:, cols]
        t = jnp.tanh(qc + qc * ta)
        r = 1.0 / (1.0 + t)
        a = (1.0 - t) * r
        b = ((t * lax.rsqrt(jnp.maximum(t, TINY))) * r) * ((1.0 + tx) * state["xc", i])
        for k in range(slabs_per_head):
            c = i * slabs_per_head + k
            for s in range(SUBLANES):
                a_s[c, s * PITCH:s * PITCH + SEG, :] = a[s * SEG:(s + 1) * SEG, k * LANES:(k + 1) * LANES]
                b_s[c, s * PITCH:s * PITCH + SEG, :] = b[s * SEG:(s + 1) * SEG, k * LANES:(k + 1) * LANES]

    def scan(i):
        slab_ids = range(i * slabs_per_head, (i + 1) * slabs_per_head)
        hs = {c: jnp.zeros((SUBLANES, LANES), F32) for c in slab_ids}
        cum = {c: jnp.ones((SUBLANES, LANES), F32) for c in slab_ids}
        for j in range(SEG):
            idx = pl.ds(j, SUBLANES, stride=PITCH)
            for c in slab_ids:
                aj = a_s[c, idx, :]
                hs[c] = aj * hs[c] + b_s[c, idx, :]
                cum[c] = aj * cum[c]
                a_s[c, idx, :] = cum[c]
                b_s[c, idx, :] = hs[c]
        row = lax.broadcasted_iota(jnp.int32, (SUBLANES, LANES), 0)
        for c in slab_ids:
            cols = slice(c * LANES, (c + 1) * LANES)
            h_in = jnp.broadcast_to(hcar[SUBLANES - 1:SUBLANES, cols], (SUBLANES, LANES))
            cin = h_in
            for _ in range(SUBLANES - 1):
                out = hs[c] + cum[c] * cin
                cin = jnp.where(row == 0, h_in, pltpu.roll(out, 1, 0))
            cin_s[c] = cin
            hcar[:, cols] = hs[c] + cum[c] * cin
        for c in slab_ids:
            cols = slice(c * LANES, (c + 1) * LANES)
            for s in range(SUBLANES):
                rows = slice(s * SEG, (s + 1) * SEG)
                prow = slice(s * PITCH, s * PITCH + SEG)
                h = b_s[c, prow, :] + a_s[c, prow, :] * cin_s[c, s:s + 1, :]
                rec_ref[0, rows, cols] = (h * gy_s[rows, cols]).astype(BF16)

    heads_per_piece = PIECE_W // LRU_HD
    filler = iter([functools.partial(proj_gl, k) for k in range(2 * D_MODEL // PIECE_W)])

    def fill():
        piece = next(filler, None)
        if piece is not None:
            piece()

    for k in range(LRU_W // PIECE_W):
        proj_x(k)
    for k in range(LRU_W // PIECE_W):
        proj_y(k)
        for i in range(k * heads_per_piece, (k + 1) * heads_per_piece):
            conv(i)
            fill()
            gate_dots(i)
    for i in range(LRU_HEADS):
        gates(i)
        fill()
        scan(i)
    for piece in filler:
        piece()


def _lru(u3, w_xyg, conv_w, conv_b, wx, bx, wa, ba, ap):
    bsz, seq, _ = u3.shape
    n_slab = LRU_W // LANES

    def row(w):
        return pl.BlockSpec((1, TM, w), lambda b, t: (b, t, 0))

    return pl.pallas_call(
        _lru_kernel,
        name="lru",
        grid=(bsz, seq // TM),
        in_specs=[row(D_MODEL), _const_spec((D_MODEL, 4 * LRU_W)),
                  _const_spec((CONV_W, LRU_W)), _const_spec((1, LRU_W)),
                  _const_spec((LRU_HEADS, LRU_HD, LRU_HD)), _const_spec((1, LRU_W)),
                  _const_spec((LRU_HEADS, LRU_HD, LRU_HD)), _const_spec((1, LRU_W)),
                  _const_spec((1, LRU_W))],
        out_specs=[row(LRU_W), row(2 * D_MODEL)],
        out_shape=[jax.ShapeDtypeStruct((bsz, seq, LRU_W), BF16),
                   jax.ShapeDtypeStruct((bsz, seq, 2 * D_MODEL), BF16)],
        scratch_shapes=[pltpu.VMEM((TM + SUBLANES, LRU_W), F32),
                        pltpu.VMEM((n_slab, SUBLANES * PITCH, LANES), F32),
                        pltpu.VMEM((n_slab, SUBLANES * PITCH, LANES), F32),
                        pltpu.VMEM((TM, LRU_W), F32),
                        pltpu.VMEM((SUBLANES, LRU_W), F32),
                        pltpu.VMEM((n_slab, SUBLANES, LANES), F32)],
        compiler_params=pltpu.CompilerParams(
            dimension_semantics=("arbitrary", "arbitrary"), vmem_limit_bytes=VMEM_LIMIT),
    )(u3, w_xyg, conv_w, conv_b, wx, bx, wa, ba, ap)


def _bucket_table():
    qi = np.arange(BAND)[:, None]
    kj = np.arange(BAND)[None, :]
    max_exact = REL_BUCKETS // 2
    out = np.zeros((N_GROUPS, 2, BAND, BAND), np.int32)
    for g, d in enumerate(DILATIONS):
        for half in range(2):
            steps = qi + BAND - kj if half == 0 else qi - kj
            valid = (steps >= 0) & (steps <= BAND)
            dist = np.maximum(steps, 0) * d
            nf = np.maximum(dist, 1).astype(np.float32)
            large = max_exact + (np.log(nf / np.float32(max_exact))
                                 / np.float32(math.log(REL_MAX_DISTANCE / max_exact))
                                 * np.float32(REL_BUCKETS - max_exact)).astype(np.int32)
            large = np.minimum(large, REL_BUCKETS - 1)
            bucket = np.where(dist < max_exact, dist, large)
            out[g, half] = np.where(valid, bucket, -1)
    return out


def _attn_kernel(group, pp, rows, tab_ref, bkt_ref, u_ref, w_ref, o_ref, st_ref,
                 qbuf, kbuf, vbuf, bias_s):
    first = (pl.program_id(0) == 0) & (pl.program_id(1) == 0) & (pl.program_id(2) == 0)
    c = pl.program_id(2)
    log2e = math.log2(math.e)
    vw = 2 * HEAD_DIM

    @pl.when(first)
    def _():
        for h in range(HEADS_PER_GROUP):
            for half in range(2):
                bk = bkt_ref[half]
                bias = jnp.full((BAND, BAND), -jnp.inf, F32)
                for n in range(REL_BUCKETS):
                    bias = jnp.where(bk == n, tab_ref[n, group * HEADS_PER_GROUP + h] * log2e, bias)
                bias_s[h, :, half * BAND:(half + 1) * BAND] = bias
            for ph in range(pp):
                vbuf[ph, :, h * vw + HEAD_DIM:(h + 1) * vw] = jnp.ones((BAND + rows, HEAD_DIM), BF16)

    @pl.when(c == 0)
    def _():
        for ph in range(pp):
            kbuf[ph, 0:BAND, :] = jnp.zeros((BAND, GROUP_W), BF16)
            for h in range(HEADS_PER_GROUP):
                vbuf[ph, 0:BAND, h * vw:h * vw + HEAD_DIM] = jnp.zeros((BAND, HEAD_DIM), BF16)

    lane2 = lax.broadcasted_iota(jnp.int32, (1, 2 * BAND), 1)
    pen = jnp.where((lane2 < BAND) & (c == 0), -jnp.inf, 0.0).astype(F32)
    qk_scale = log2e / math.sqrt(HEAD_DIM)
    nt = (((1,), (1,)), ((), ()))
    lane = lax.broadcasted_iota(jnp.int32, (BAND, LANES), 1)

    def project(ph, r0):
        qkv = jnp.dot(u_ref[ph, r0:r0 + PROJ_ROWS, :], w_ref[...], preferred_element_type=F32).astype(BF16)
        qbuf[ph, r0:r0 + PROJ_ROWS, :] = qkv[:, :GROUP_W]
        kbuf[ph, BAND + r0:BAND + r0 + PROJ_ROWS, :] = qkv[:, GROUP_W:2 * GROUP_W]
        for h in range(HEADS_PER_GROUP):
            vbuf[ph, BAND + r0:BAND + r0 + PROJ_ROWS, h * vw:h * vw + HEAD_DIM] = (
                qkv[:, 2 * GROUP_W + h * HEAD_DIM:2 * GROUP_W + (h + 1) * HEAD_DIM])

    def units_of(r0):
        return [(n, h) for n in range(r0 // BAND, (r0 + PROJ_ROWS) // BAND) for h in range(HEADS_PER_GROUP)]

    def attend_scores(ph, r0):
        scores = []
        for n, h in units_of(r0):
            cols = slice(h * HEAD_DIM, (h + 1) * HEAD_DIM)
            q = qbuf[ph, n * BAND:(n + 1) * BAND, cols]
            kk = kbuf[ph, n * BAND:(n + 2) * BAND, cols]
            s = lax.dot_general(q, kk, nt, preferred_element_type=F32) * qk_scale + bias_s[h]
            if n == 0:
                s = s + pen
            scores.append(s)
        return scores

    def attend_finish(ph, r0, scores):
        units = units_of(r0)
        maxes = [jnp.max(s, axis=-1, keepdims=True) for s in scores]
        probs = [jnp.exp2(s - m).astype(BF16) for s, m in zip(scores, maxes)]
        outs = [jnp.dot(p, vbuf[ph, n * BAND:(n + 2) * BAND, h * vw:(h + 1) * vw], preferred_element_type=F32)
                for (n, h), p in zip(units, probs)]
        stats = None
        for (n, h), o, m in zip(units, outs, maxes):
            o_ref[ph, n * BAND:(n + 1) * BAND, h * HEAD_DIM:(h + 1) * HEAD_DIM] = o[:, :HEAD_DIM].astype(BF16)
            den = o[:, HEAD_DIM:]
            stats = jnp.broadcast_to(m, (BAND, LANES)) if h == 0 else jnp.where(lane == h, m, stats)
            stats = jnp.where(lane == HEADS_PER_GROUP + h, den, stats)
            if h == HEADS_PER_GROUP - 1:
                st_ref[ph, n * BAND:(n + 1) * BAND, :] = stats

    sub_blocks = [(ph, r0) for ph in range(pp) for r0 in range(0, rows, PROJ_ROWS)]
    for sb in sub_blocks[:2]:
        project(*sb)
    for i, sb in enumerate(sub_blocks):
        scores = attend_scores(*sb)
        if i + 2 < len(sub_blocks):
            project(*sub_blocks[i + 2])
        attend_finish(*sb, scores)

    for ph in range(pp):
        kbuf[ph, 0:BAND, :] = kbuf[ph, rows:rows + BAND, :]
        vbuf[ph, 0:BAND, :] = vbuf[ph, rows:rows + BAND, :]


def _attn_group(group, u_g, w_qkv, table, bkt):
    bsz, d, sub, _ = u_g.shape
    rows = min(ATT_R, sub)
    pp = min(d, ATT_R // rows)
    assert sub % rows == 0 and d % pp == 0 and (pp == 1 or rows == sub) and rows % PROJ_ROWS == 0

    def spec(width):
        return pl.BlockSpec((None, pp, rows, width), lambda b, p, c: (b, p, c, 0))

    return pl.pallas_call(
        functools.partial(_attn_kernel, group, pp, rows),
        name=f"attn{group}",
        grid=(bsz, d // pp, sub // rows),
        in_specs=[pl.BlockSpec(memory_space=pltpu.SMEM),
                  pl.BlockSpec((None, 2, BAND, BAND), lambda b, p, c: (group, 0, 0, 0)),
                  spec(D_MODEL),
                  pl.BlockSpec((D_MODEL, 3 * GROUP_W), lambda b, p, c: (0, group), pipeline_mode=pl.Buffered(1))],
        out_specs=[spec(GROUP_W), spec(LANES)],
        out_shape=[jax.ShapeDtypeStruct((bsz, d, sub, GROUP_W), BF16),
                   jax.ShapeDtypeStruct((bsz, d, sub, LANES), F32)],
        scratch_shapes=[pltpu.VMEM((pp, rows, GROUP_W), BF16),
                        pltpu.VMEM((pp, BAND + rows, GROUP_W), BF16),
                        pltpu.VMEM((pp, BAND + rows, 2 * GROUP_W), BF16),
                        pltpu.VMEM((HEADS_PER_GROUP, BAND, 2 * BAND), F32)],
        compiler_params=pltpu.CompilerParams(
            dimension_semantics=("arbitrary", "arbitrary", "arbitrary"),
            vmem_limit_bytes=VMEM_LIMIT),
    )(table, bkt, u_g, w_qkv)


def _merge_kernel(h_ref, o0_ref, o1_ref, o2_ref, l0_ref, l1_ref, l2_ref, rec_ref, gl_ref,
                  wat_ref, wrec_ref, wout_ref, gpost_ref, out_ref, osc, lsc):
    rec_d = jnp.dot(rec_ref[...], wrec_ref[...], preferred_element_type=F32)
    for gi, o_ref, l_ref in ((1, o1_ref, l1_ref), (2, o2_ref, l2_ref)):
        d = DILATIONS[gi]
        for p in range(d):
            idx = pl.ds(p, TM // d, stride=d)
            lsc[gi - 1, idx, :] = l_ref[p]
            for h in range(HEADS_PER_GROUP):
                osc[(gi - 1) * HEADS_PER_GROUP + h, idx, :] = (
                    o_ref[p, :, h * HEAD_DIM:(h + 1) * HEAD_DIM].astype(F32))
    st = (l0_ref[0], lsc[0], lsc[1])
    mx = jnp.maximum(jnp.maximum(st[0], st[1]), st[2])
    e = [jnp.exp2(x - mx) for x in st]
    den = [pltpu.roll(x, LANES - HEADS_PER_GROUP, 1) for x in st]
    inv = 1.0 / (e[0] * den[0] + e[1] * den[1] + e[2] * den[2])
    w0, w1, w2 = e[0] * inv, e[1] * inv, e[2] * inv
    parts = []
    for h in range(HEADS_PER_GROUP):
        cols = slice(h * HEAD_DIM, (h + 1) * HEAD_DIM)
        parts.append(w0[:, h:h + 1] * o0_ref[0, :, cols].astype(F32)
                     + w1[:, h:h + 1] * osc[h]
                     + w2[:, h:h + 1] * osc[HEADS_PER_GROUP + h])
    attn = jnp.concatenate(parts, axis=-1).astype(BF16)
    attn_d = jnp.dot(attn, wat_ref[...], preferred_element_type=F32)
    tg = jnp.tanh(0.5 * gl_ref[...].astype(F32))
    merged = 0.5 * ((1.0 + tg[:, :D_MODEL]) * attn_d + (1.0 + tg[:, D_MODEL:]) * rec_d)
    mo = jnp.dot(merged.astype(BF16), wout_ref[...], preferred_element_type=F32)
    out_ref[...] = h_ref[...] + _rms(mo, gpost_ref[...])


def _merge(h3, o_list, lse_list, rec3, gl3, w_attn, w_rec, w_out, g_post):
    bsz, seq, _ = h3.shape

    def row(w):
        return pl.BlockSpec((None, TM, w), lambda b, i: (b, i, 0))

    def phased(d, w):
        return pl.BlockSpec((None, d, TM // d, w), lambda b, i: (b, 0, i, 0))

    return pl.pallas_call(
        _merge_kernel,
        name="merge",
        grid=(bsz, seq // TM),
        in_specs=[row(D_MODEL)] + [phased(d, GROUP_W) for d in DILATIONS]
                 + [phased(d, LANES) for d in DILATIONS] + [row(LRU_W), row(2 * D_MODEL),
                 _const_spec((GROUP_W, D_MODEL)), _const_spec((LRU_W, D_MODEL)),
                 _const_spec((D_MODEL, D_MODEL)), _const_spec((1, D_MODEL))],
        out_specs=row(D_MODEL),
        out_shape=jax.ShapeDtypeStruct((bsz, seq, D_MODEL), F32),
        scratch_shapes=[pltpu.VMEM((2 * HEADS_PER_GROUP, TM, LANES), F32),
                        pltpu.VMEM((2, TM, LANES), F32)],
        compiler_params=pltpu.CompilerParams(
            dimension_semantics=("arbitrary", "arbitrary"), vmem_limit_bytes=VMEM_LIMIT),
    )(h3, *o_list, *lse_list, rec3, gl3, w_attn, w_rec, w_out, g_post)


def kernel(x, ffn1_norm_pre, ffn1_norm_post, ffn1_w_gate, ffn1_w_up, ffn1_w_down, mix_norm_pre, mix_norm_post, w_in, rel_bias_table, conv_w, conv_b, lru_w_x, lru_b_x, lru_w_a, lru_b_a, lru_a_param, w_attn_branch, w_rec_branch, w_out, ffn2_norm_pre, ffn2_norm_post, ffn2_w_gate, ffn2_w_up, ffn2_w_down):
    bsz, seq, _ = x.shape
    n = bsz * seq
    depth = w_in.shape[0]
    bkt = jnp.asarray(_bucket_table())
    h = x.reshape(n, D_MODEL)
    for l in range(depth):
        gw = 3 * GROUP_W
        regroup = tuple((g * gw + j * GROUP_W, j * QKV_W + g * GROUP_W, GROUP_W)
                        for g in range(N_GROUPS) for j in range(3))
        jobs = [_CastJob(w_in[l], ((3 * QKV_W, regroup), (4 * LRU_W, ((0, 3 * QKV_W, 4 * LRU_W),)))),
                _plain_cast(lru_w_x[l].reshape(LRU_W, LRU_HD)), _plain_cast(lru_w_a[l].reshape(LRU_W, LRU_HD)),
                _plain_cast(w_attn_branch[l]), _plain_cast(w_rec_branch[l]), _plain_cast(w_out[l]),
                _plain_cast(ffn2_w_gate[l]), _plain_cast(ffn2_w_up[l]), _plain_cast(ffn2_w_down[l])]
        (h, u, u_d4, u_d16, w_qkv, w_xyg, wx, wa, w_attn, w_rec, w_o, w2_gate, w2_up, w2_down) = _ffn(
            h, ffn1_norm_pre[l][None], ffn1_norm_post[l][None],
            ffn1_w_gate[l].astype(BF16), ffn1_w_up[l].astype(BF16), ffn1_w_down[l].astype(BF16),
            g_next=mix_norm_pre[l][None], seq=seq, cast_jobs=jobs)
        h3 = h.reshape(bsz, seq, D_MODEL)
        u3 = u.reshape(bsz, seq, D_MODEL)

        rec, gl = _lru(u3, w_xyg, conv_w[l], conv_b[l][None],
                       wx.reshape(LRU_HEADS, LRU_HD, LRU_HD), lru_b_x[l].reshape(1, LRU_W),
                       wa.reshape(LRU_HEADS, LRU_HD, LRU_HD), lru_b_a[l].reshape(1, LRU_W),
                       lru_a_param[l][None])
        o_list, lse_list = [], []
        for g, u_g in enumerate((u3[:, None], u_d4, u_d16)):
            o_g, lse_g = _attn_group(g, u_g, w_qkv, rel_bias_table, bkt)
            o_list.append(o_g)
            lse_list.append(lse_g)
        h3 = _merge(h3, o_list, lse_list, rec, gl, w_attn, w_rec, w_o, mix_norm_post[l][None])

        h, = _ffn(h3.reshape(n, D_MODEL), ffn2_norm_pre[l][None], ffn2_norm_post[l][None],
                  w2_gate, w2_up, w2_down)
    return h.reshape(bsz, seq, D_MODEL)
```

```python
import functools
import math
from typing import NamedTuple

import numpy as np
import jax
import jax.numpy as jnp
from jax import lax
from jax.experimental import pallas as pl
from jax.experimental.pallas import tpu as pltpu

F32 = jnp.float32
BF16 = jnp.bfloat16

D_MODEL = 1024
HEAD_DIM = 128
HEADS_PER_GROUP = 4
DILATIONS = (1, 4, 16)
BAND = 128
N_GROUPS = 3
GROUP_W = HEADS_PER_GROUP * HEAD_DIM
QKV_W = N_GROUPS * GROUP_W
LRU_W = D_MODEL
LRU_HEADS = 4
LRU_HD = LRU_W // LRU_HEADS
CONV_W = 4
LRU_C = 8.0
D_FF = 2816
REL_BUCKETS = 32
REL_MAX_DISTANCE = 2048
EPS = 1e-6
TINY = 1e-30

LANES = 128
SUBLANES = 8
BF16_TILE_ROWS = 16
MAX_CAST_BLOCKS = 64
VMEM_LIMIT = 56 * 1024 * 1024

TM_FFN = 512
TM = 512
ATT_R = 2048
SEG = TM // SUBLANES
PITCH = SEG + SUBLANES
PIECE_W = 256
PROJ_ROWS = 256
PHASE_STEP = 4
assert DILATIONS == (1, PHASE_STEP, PHASE_STEP * PHASE_STEP)


def _rms(x, g):
    ms = jnp.mean(x * x, axis=-1, keepdims=True)
    return x * lax.rsqrt(ms + EPS) * g


def _gelu_tanh(x):
    c = math.sqrt(2.0 / math.pi)
    inner = x * (c + (c * 0.044715) * (x * x))
    return (0.5 * x) * (1.0 + jnp.tanh(inner))


def _const_spec(shape):
    nd = len(shape)
    return pl.BlockSpec(shape, lambda *_: (0,) * nd, pipeline_mode=pl.Buffered(1))


class _CastJob(NamedTuple):
    src: jax.Array
    outs: tuple

    @property
    def block_rows(self):
        rows = self.src.shape[0]
        br = BF16_TILE_ROWS
        while rows % br or rows // br > MAX_CAST_BLOCKS:
            br += BF16_TILE_ROWS
        return br


def _plain_cast(w):
    return _CastJob(w, ((w.shape[1], ((0, 0, w.shape[1]),)),))


def _ffn_kernel(emit_normed, jobs_pieces, x_ref, gpre_ref, gpost_ref, wg_ref, wu_ref, wd_ref, *rest):
    n_jobs = len(jobs_pieces)
    n_normed = len(DILATIONS) if emit_normed else 0
    if emit_normed:
        gnext_ref, rest = rest[0], rest[1:]
    cast_in, o_ref = rest[:n_jobs], rest[n_jobs]
    normed_refs = rest[n_jobs + 1:n_jobs + 1 + n_normed]
    n_cast_out = sum(len(p) for p in jobs_pieces)
    cast_out = rest[n_jobs + 1 + n_normed:n_jobs + 1 + n_normed + n_cast_out]
    scratch = rest[n_jobs + 1 + n_normed + n_cast_out:]
    x = x_ref[...]
    xn = _rms(x, gpre_ref[...]).astype(BF16)
    g = jnp.dot(xn, wg_ref[...], preferred_element_type=F32)
    u = jnp.dot(xn, wu_ref[...], preferred_element_type=F32)
    hg = 0.5 * g
    a = ((hg * u) * (1.0 + jnp.tanh(hg))).astype(BF16)
    half = TM_FFN // 2
    halves = []
    for r0 in (0, half):
        f = jnp.dot(a[r0:r0 + half], wd_ref[...], preferred_element_type=F32)
        halves.append(x[r0:r0 + half] + 0.5 * _rms(f, gpost_ref[...]))
        o_ref[r0:r0 + half, :] = halves[-1]
    h = jnp.concatenate(halves, axis=0)
    if emit_normed:
        slab, slab4 = scratch
        u4_ref, u16_ref = normed_refs[1:]
        hn = _rms(h, gnext_ref[...])
        normed_refs[0][...] = hn.astype(BF16)
        r4 = TM_FFN // PHASE_STEP
        for c in range(D_MODEL // LANES):
            lanes = slice(c * LANES, (c + 1) * LANES)
            slab[c] = hn[:, lanes]
            for p in range(PHASE_STEP):
                part = slab[c, pl.ds(p, r4, stride=PHASE_STEP), :]
                u4_ref[p, :, lanes] = part.astype(BF16)
                slab4[c, p * r4:(p + 1) * r4, :] = part
            for p in range(PHASE_STEP):
                for q in range(PHASE_STEP):
                    u16_ref[q * PHASE_STEP + p, :, lanes] = (
                        slab4[c, pl.ds(p * r4 + q, r4 // PHASE_STEP, stride=PHASE_STEP), :].astype(BF16))
    outs = iter(cast_out)
    for src_ref, pieces_per_out in zip(cast_in, jobs_pieces):
        for pieces in pieces_per_out:
            dst_ref = next(outs)
            for dst_col, src_col, width in pieces:
                dst_ref[:, dst_col:dst_col + width] = src_ref[:, src_col:src_col + width].astype(BF16)


def _ffn(x2, g_pre, g_post, wg, wu, wd, g_next=None, seq=None, cast_jobs=()):
    n = x2.shape[0]
    steps = n // TM_FFN
    assert steps <= MAX_CAST_BLOCKS
    emit_normed = g_next is not None
    row = pl.BlockSpec((TM_FFN, D_MODEL), lambda i: (i, 0))
    in_specs = [row, _const_spec((1, D_MODEL)), _const_spec((1, D_MODEL)),
                _const_spec((D_MODEL, D_FF)), _const_spec((D_MODEL, D_FF)), _const_spec((D_FF, D_MODEL))]
    operands = [x2, g_pre, g_post, wg, wu, wd]
    out_specs = [row]
    out_shape = [jax.ShapeDtypeStruct((n, D_MODEL), F32)]
    scratch_shapes = []
    if emit_normed:
        tiles = seq // TM_FFN
        in_specs.append(_const_spec((1, D_MODEL)))
        operands.append(g_next)
        out_specs.append(row)
        out_shape.append(jax.ShapeDtypeStruct((n, D_MODEL), BF16))
        for d in DILATIONS[1:]:
            out_specs.append(pl.BlockSpec((None, d, TM_FFN // d, D_MODEL),
                                          functools.partial(lambda i, t: (i // t, 0, i % t, 0), t=tiles)))
            out_shape.append(jax.ShapeDtypeStruct((n // seq, d, seq // d, D_MODEL), BF16))
        scratch_shapes += [pltpu.VMEM((D_MODEL // LANES, TM_FFN, LANES), F32)] * 2
    for job in cast_jobs:
        rows, br = job.src.shape[0], job.block_rows
        index_map = functools.partial(lambda i, last: (jnp.minimum(i, last), 0), last=rows // br - 1)
        in_specs.append(pl.BlockSpec((br, job.src.shape[1]), index_map))
        operands.append(job.src)
        for cols, _ in job.outs:
            out_specs.append(pl.BlockSpec((br, cols), index_map))
            out_shape.append(jax.ShapeDtypeStruct((rows, cols), BF16))
    jobs_pieces = tuple(tuple(pieces for _, pieces in job.outs) for job in cast_jobs)
    return pl.pallas_call(
        functools.partial(_ffn_kernel, emit_normed, jobs_pieces),
        name="ffn",
        grid=(steps,),
        in_specs=in_specs,
        out_specs=out_specs,
        out_shape=out_shape,
        scratch_shapes=scratch_shapes,
        compiler_params=pltpu.CompilerParams(
            dimension_semantics=("arbitrary",), vmem_limit_bytes=VMEM_LIMIT),
    )(*operands)


def _lru_kernel(u_ref, wlru_ref, cw_ref, cb_ref, wx_ref, bx_ref, wa_ref, ba_ref, ap_ref,
                rec_ref, gl_ref, xbuf, a_s, b_s, gy_s, hcar, cin_s):
    @pl.when(pl.program_id(1) == 0)
    def _():
        xbuf[0:SUBLANES, :] = jnp.zeros((SUBLANES, LRU_W), F32)
        hcar[...] = jnp.zeros(hcar.shape, F32)

    slabs_per_head = LRU_HD // LANES
    ap = -ap_ref[...]
    quarter_c_softplus = (0.25 * LRU_C) * (jnp.maximum(ap, 0.0) + jnp.log1p(jnp.exp(-jnp.abs(ap))))
    state = {}

    def proj_x(k):
        cols = slice(k * PIECE_W, (k + 1) * PIECE_W)
        xbuf[SUBLANES:SUBLANES + TM, cols] = jnp.dot(u_ref[0], wlru_ref[:, cols], preferred_element_type=F32)

    def proj_y(k):
        cols = slice(k * PIECE_W, (k + 1) * PIECE_W)
        yr = jnp.dot(u_ref[0], wlru_ref[:, LRU_W + k * PIECE_W:LRU_W + (k + 1) * PIECE_W],
                     preferred_element_type=F32)
        gy_s[:, cols] = _gelu_tanh(yr)

    def proj_gl(k):
        cols = slice(k * PIECE_W, (k + 1) * PIECE_W)
        gl_ref[0, :, cols] = jnp.dot(u_ref[0], wlru_ref[:, 2 * LRU_W + k * PIECE_W:2 * LRU_W + (k + 1) * PIECE_W],
                                     preferred_element_type=F32).astype(BF16)

    def gate_dots(i):
        xh = state["xcb", i]
        state["pre_x", i] = jnp.dot(xh, wx_ref[i], preferred_element_type=F32)
        state["pre_a", i] = jnp.dot(xh, wa_ref[i], preferred_element_type=F32)

    def conv(i):
        cols = slice(i * LRU_HD, (i + 1) * LRU_HD)
        xc = cb_ref[:, cols]
        for j in range(CONV_W):
            off = SUBLANES - (CONV_W - 1) + j
            xc = xc + cw_ref[j:j + 1, cols] * xbuf[off:off + TM, cols]
        xbuf[0:SUBLANES, cols] = xbuf[TM:TM + SUBLANES, cols]
        state["xc", i] = xc
        state["xcb", i] = xc.astype(BF16)

    def gates(i):
        for k in range(slabs_per_head):
            c = i * slabs_per_head + k
            lanes = slice(c * LANES, (c + 1) * LANES)
            hl = slice(k * LANES, (k + 1) * LANES)
            qc = quarter_c_softplus[:, lanes]
            for s in range(SUBLANES):
                rows = slice(s * SEG, (s + 1) * SEG)
                tx = jnp.tanh(0.5 * (state["pre_x", i][rows, hl] + bx_ref[:, lanes]))
                ta = jnp.tanh(0.5 * (state["pre_a", i][rows, hl] + ba_ref[:, lanes]))
                t = jnp.tanh(qc + qc * ta)
                r = 1.0 / (1.0 + t)
                a_s[c, s * PITCH:s * PITCH + SEG, :] = (1.0 - t) * r
                b_s[c, s * PITCH:s * PITCH + SEG, :] = (
                    ((t * lax.rsqrt(jnp.maximum(t, TINY))) * r) * ((1.0 + tx) * state["xc", i][rows, hl]))

    def scan(i):
        slab_ids = range(i * slabs_per_head, (i + 1) * slabs_per_head)
        hs = {c: jnp.zeros((SUBLANES, LANES), F32) for c in slab_ids}
        cum = {c: jnp.ones((SUBLANES, LANES), F32) for c in slab_ids}
        for j in range(SEG):
            idx = pl.ds(j, SUBLANES, stride=PITCH)
            for c in slab_ids:
                aj = a_s[c, idx, :]
                hs[c] = aj * hs[c] + b_s[c, idx, :]
                cum[c] = aj * cum[c]
                a_s[c, idx, :] = cum[c]
                b_s[c, idx, :] = hs[c]
        row = lax.broadcasted_iota(jnp.int32, (SUBLANES, LANES), 0)
        for c in slab_ids:
            cols = slice(c * LANES, (c + 1) * LANES)
            h_in = jnp.broadcast_to(hcar[SUBLANES - 1:SUBLANES, cols], (SUBLANES, LANES))
            cin = h_in
            for _ in range(SUBLANES - 1):
                out = hs[c] + cum[c] * cin
                cin = jnp.where(row == 0, h_in, pltpu.roll(out, 1, 0))
            cin_s[c] = cin
            hcar[:, cols] = hs[c] + cum[c] * cin
        for c in slab_ids:
            cols = slice(c * LANES, (c + 1) * LANES)
            for s in range(SUBLANES):
                rows = slice(s * SEG, (s + 1) * SEG)
                prow = slice(s * PITCH, s * PITCH + SEG)
                h = b_s[c, prow, :] + a_s[c, prow, :] * cin_s[c, s:s + 1, :]
                rec_ref[0, rows, cols] = (h * gy_s[rows, cols]).astype(BF16)

    heads_per_piece = PIECE_W // LRU_HD
    filler = iter([functools.partial(proj_gl, k) for k in range(2 * D_MODEL // PIECE_W)])

    def fill():
        piece = next(filler, None)
        if piece is not None:
            piece()

    for k in range(LRU_W // PIECE_W):
        proj_x(k)
    for k in range(LRU_W // PIECE_W):
        proj_y(k)
        for i in range(k * heads_per_piece, (k + 1) * heads_per_piece):
            conv(i)
            fill()
            gate_dots(i)
    for i in range(LRU_HEADS):
        gates(i)
        fill()
        scan(i)
    for piece in filler:
        piece()


def _lru(u3, w_xyg, conv_w, conv_b, wx, bx, wa, ba, ap):
    bsz, seq, _ = u3.shape
    n_slab = LRU_W // LANES

    def row(w):
        return pl.BlockSpec((1, TM, w), lambda b, t: (b, t, 0))

    return pl.pallas_call(
        _lru_kernel,
        name="lru",
        grid=(bsz, seq // TM),
        in_specs=[row(D_MODEL), _const_spec((D_MODEL, 4 * LRU_W)),
                  _const_spec((CONV_W, LRU_W)), _const_spec((1, LRU_W)),
                  _const_spec((LRU_HEADS, LRU_HD, LRU_HD)), _const_spec((1, LRU_W)),
                  _const_spec((LRU_HEADS, LRU_HD, LRU_HD)), _const_spec((1, LRU_W)),
                  _const_spec((1, LRU_W))],
        out_specs=[row(LRU_W), row(2 * D_MODEL)],
        out_shape=[jax.ShapeDtypeStruct((bsz, seq, LRU_W), BF16),
                   jax.ShapeDtypeStruct((bsz, seq, 2 * D_MODEL), BF16)],
        scratch_shapes=[pltpu.VMEM((TM + SUBLANES, LRU_W), F32),
                        pltpu.VMEM((n_slab, SUBLANES * PITCH, LANES), F32),
                        pltpu.VMEM((n_slab, SUBLANES * PITCH, LANES), F32),
                        pltpu.VMEM((TM, LRU_W), F32),
                        pltpu.VMEM((SUBLANES, LRU_W), F32),
                        pltpu.VMEM((n_slab, SUBLANES, LANES), F32)],
        compiler_params=pltpu.CompilerParams(
            dimension_semantics=("arbitrary", "arbitrary"), vmem_limit_bytes=VMEM_LIMIT),
    )(u3, w_xyg, conv_w, conv_b, wx, bx, wa, ba, ap)


def _bucket_table():
    qi = np.arange(BAND)[:, None]
    kj = np.arange(BAND)[None, :]
    max_exact = REL_BUCKETS // 2
    out = np.zeros((N_GROUPS, 2, BAND, BAND), np.int32)
    for g, d in enumerate(DILATIONS):
        for half in range(2):
            steps = qi + BAND - kj if half == 0 else qi - kj
            valid = (steps >= 0) & (steps <= BAND)
            dist = np.maximum(steps, 0) * d
            nf = np.maximum(dist, 1).astype(np.float32)
            large = max_exact + (np.log(nf / np.float32(max_exact))
                                 / np.float32(math.log(REL_MAX_DISTANCE / max_exact))
                                 * np.float32(REL_BUCKETS - max_exact)).astype(np.int32)
            large = np.minimum(large, REL_BUCKETS - 1)
            bucket = np.where(dist < max_exact, dist, large)
            out[g, half] = np.where(valid, bucket, -1)
    return out


def _attn_kernel(group, pp, rows, tab_ref, bkt_ref, u_ref, w_ref, o_ref, st_ref,
                 qbuf, kbuf, vbuf, bias_s):
    first = (pl.program_id(0) == 0) & (pl.program_id(1) == 0) & (pl.program_id(2) == 0)
    c = pl.program_id(2)
    log2e = math.log2(math.e)
    vw = 2 * HEAD_DIM

    @pl.when(first)
    def _():
        for h in range(HEADS_PER_GROUP):
            for half in range(2):
                bk = bkt_ref[half]
                bias = jnp.full((BAND, BAND), -jnp.inf, F32)
                for n in range(REL_BUCKETS):
                    bias = jnp.where(bk == n, tab_ref[n, group * HEADS_PER_GROUP + h] * log2e, bias)
                bias_s[h, :, half * BAND:(half + 1) * BAND] = bias
            for ph in range(pp):
                vbuf[ph, :, h * vw + HEAD_DIM:(h + 1) * vw] = jnp.ones((BAND + rows, HEAD_DIM), BF16)

    @pl.when(c == 0)
    def _():
        for ph in range(pp):
            kbuf[ph, 0:BAND, :] = jnp.zeros((BAND, GROUP_W), BF16)
            for h in range(HEADS_PER_GROUP):
                vbuf[ph, 0:BAND, h * vw:h * vw + HEAD_DIM] = jnp.zeros((BAND, HEAD_DIM), BF16)

    lane2 = lax.broadcasted_iota(jnp.int32, (1, 2 * BAND), 1)
    pen = jnp.where((lane2 < BAND) & (c == 0), -jnp.inf, 0.0).astype(F32)
    qk_scale = log2e / math.sqrt(HEAD_DIM)
    nt = (((1,), (1,)), ((), ()))
    lane = lax.broadcasted_iota(jnp.int32, (BAND, LANES), 1)

    def project(ph, r0):
        qkv = jnp.dot(u_ref[ph, r0:r0 + PROJ_ROWS, :], w_ref[...], preferred_element_type=F32).astype(BF16)
        qbuf[ph, r0:r0 + PROJ_ROWS, :] = qkv[:, :GROUP_W]
        kbuf[ph, BAND + r0:BAND + r0 + PROJ_ROWS, :] = qkv[:, GROUP_W:2 * GROUP_W]
        for h in range(HEADS_PER_GROUP):
            vbuf[ph, BAND + r0:BAND + r0 + PROJ_ROWS, h * vw:h * vw + HEAD_DIM] = (
                qkv[:, 2 * GROUP_W + h * HEAD_DIM:2 * GROUP_W + (h + 1) * HEAD_DIM])

    def units_of(r0):
        return [(n, h) for n in range(r0 // BAND, (r0 + PROJ_ROWS) // BAND) for h in range(HEADS_PER_GROUP)]

    def attend_scores(ph, r0):
        scores = []
        for n, h in units_of(r0):
            cols = slice(h * HEAD_DIM, (h + 1) * HEAD_DIM)
            q = qbuf[ph, n * BAND:(n + 1) * BAND, cols]
            kk = kbuf[ph, n * BAND:(n + 2) * BAND, cols]
            s = lax.dot_general(q, kk, nt, preferred_element_type=F32) * qk_scale + bias_s[h]
            if n == 0:
                s = s + pen
            scores.append(s)
        return scores

    def attend_finish(ph, r0, scores):
        units = units_of(r0)
        maxes = [jnp.max(s, axis=-1, keepdims=True) for s in scores]
        probs = [jnp.exp2(s - m).astype(BF16) for s, m in zip(scores, maxes)]
        outs = [jnp.dot(p, vbuf[ph, n * BAND:(n + 2) * BAND, h * vw:(h + 1) * vw], preferred_element_type=F32)
                for (n, h), p in zip(units, probs)]
        stats = None
        for (n, h), o, m in zip(units, outs, maxes):
            o_ref[ph, n * BAND:(n + 1) * BAND, h * HEAD_DIM:(h + 1) * HEAD_DIM] = o[:, :HEAD_DIM].astype(BF16)
            den = o[:, HEAD_DIM:]
            stats = jnp.broadcast_to(m, (BAND, LANES)) if h == 0 else jnp.where(lane == h, m, stats)
            stats = jnp.where(lane == HEADS_PER_GROUP + h, den, stats)
            if h == HEADS_PER_GROUP - 1:
                st_ref[ph, n * BAND:(n + 1) * BAND, :] = stats

    sub_blocks = [(ph, r0) for ph in range(pp) for r0 in range(0, rows, PROJ_ROWS)]
    for sb in sub_blocks[:2]:
        project(*sb)
    for i, sb in enumerate(sub_blocks):
        scores = attend_scores(*sb)
        if i + 2 < len(sub_blocks):
            project(*sub_blocks[i + 2])
        attend_finish(*sb, scores)

    for ph in range(pp):
        kbuf[ph, 0:BAND, :] = kbuf[ph, rows:rows + BAND, :]
        vbuf[ph, 0:BAND, :] = vbuf[ph, rows:rows + BAND, :]


def _attn_group(group, u_g, w_qkv, table, bkt):
    bsz, d, sub, _ = u_g.shape
    rows = min(ATT_R, sub)
    pp = min(d, ATT_R // rows)
    assert sub % rows == 0 and d % pp == 0 and (pp == 1 or rows == sub) and rows % PROJ_ROWS == 0

    def spec(width):
        return pl.BlockSpec((None, pp, rows, width), lambda b, p, c: (b, p, c, 0))

    return pl.pallas_call(
        functools.partial(_attn_kernel, group, pp, rows),
        name=f"attn{group}",
        grid=(bsz, d // pp, sub // rows),
        in_specs=[pl.BlockSpec(memory_space=pltpu.SMEM),
                  pl.BlockSpec((None, 2, BAND, BAND), lambda b, p, c: (group, 0, 0, 0)),
                  spec(D_MODEL),
                  pl.BlockSpec((D_MODEL, 3 * GROUP_W), lambda b, p, c: (0, group), pipeline_mode=pl.Buffered(1))],
        out_specs=[spec(GROUP_W), spec(LANES)],
        out_shape=[jax.ShapeDtypeStruct((bsz, d, sub, GROUP_W), BF16),
                   jax.ShapeDtypeStruct((bsz, d, sub, LANES), F32)],
        scratch_shapes=[pltpu.VMEM((pp, rows, GROUP_W), BF16),
                        pltpu.VMEM((pp, BAND + rows, GROUP_W), BF16),
                        pltpu.VMEM((pp, BAND + rows, 2 * GROUP_W), BF16),
                        pltpu.VMEM((HEADS_PER_GROUP, BAND, 2 * BAND), F32)],
        compiler_params=pltpu.CompilerParams(
            dimension_semantics=("arbitrary", "arbitrary", "arbitrary"),
            vmem_limit_bytes=VMEM_LIMIT),
    )(table, bkt, u_g, w_qkv)


def _merge_kernel(h_ref, o0_ref, o1_ref, o2_ref, l0_ref, l1_ref, l2_ref, rec_ref, gl_ref,
                  wat_ref, wrec_ref, wout_ref, gpost_ref, out_ref, osc, lsc):
    rec_d = jnp.dot(rec_ref[...], wrec_ref[...], preferred_element_type=F32)
    for gi, o_ref, l_ref in ((1, o1_ref, l1_ref), (2, o2_ref, l2_ref)):
        d = DILATIONS[gi]
        for p in range(d):
            idx = pl.ds(p, TM // d, stride=d)
            lsc[gi - 1, idx, :] = l_ref[p]
            for h in range(HEADS_PER_GROUP):
                osc[(gi - 1) * HEADS_PER_GROUP + h, idx, :] = (
                    o_ref[p, :, h * HEAD_DIM:(h + 1) * HEAD_DIM].astype(F32))
    st = (l0_ref[0], lsc[0], lsc[1])
    mx = jnp.maximum(jnp.maximum(st[0], st[1]), st[2])
    e = [jnp.exp2(x - mx) for x in st]
    den = [pltpu.roll(x, LANES - HEADS_PER_GROUP, 1) for x in st]
    inv = 1.0 / (e[0] * den[0] + e[1] * den[1] + e[2] * den[2])
    w0, w1, w2 = e[0] * inv, e[1] * inv, e[2] * inv
    parts = []
    for h in range(HEADS_PER_GROUP):
        cols = slice(h * HEAD_DIM, (h + 1) * HEAD_DIM)
        parts.append(w0[:, h:h + 1] * o0_ref[0, :, cols].astype(F32)
                     + w1[:, h:h + 1] * osc[h]
                     + w2[:, h:h + 1] * osc[HEADS_PER_GROUP + h])
    attn = jnp.concatenate(parts, axis=-1).astype(BF16)
    attn_d = jnp.dot(attn, wat_ref[...], preferred_element_type=F32)
    tg = jnp.tanh(0.5 * gl_ref[...].astype(F32))
    merged = 0.5 * ((1.0 + tg[:, :D_MODEL]) * attn_d + (1.0 + tg[:, D_MODEL:]) * rec_d)
    mo = jnp.dot(merged.astype(BF16), wout_ref[...], preferred_element_type=F32)
    out_ref[...] = h_ref[...] + _rms(mo, gpost_ref[...])


def _merge(h3, o_list, lse_list, rec3, gl3, w_attn, w_rec, w_out, g_post):
    bsz, seq, _ = h3.shape

    def row(w):
        return pl.BlockSpec((None, TM, w), lambda b, i: (b, i, 0))

    def phased(d, w):
        return pl.BlockSpec((None, d, TM // d, w), lambda b, i: (b, 0, i, 0))

    return pl.pallas_call(
        _merge_kernel,
        name="merge",
        grid=(bsz, seq // TM),
        in_specs=[row(D_MODEL)] + [phased(d, GROUP_W) for d in DILATIONS]
                 + [phased(d, LANES) for d in DILATIONS] + [row(LRU_W), row(2 * D_MODEL),
                 _const_spec((GROUP_W, D_MODEL)), _const_spec((LRU_W, D_MODEL)),
                 _const_spec((D_MODEL, D_MODEL)), _const_spec((1, D_MODEL))],
        out_specs=row(D_MODEL),
        out_shape=jax.ShapeDtypeStruct((bsz, seq, D_MODEL), F32),
        scratch_shapes=[pltpu.VMEM((2 * HEADS_PER_GROUP, TM, LANES), F32),
                        pltpu.VMEM((2, TM, LANES), F32)],
        compiler_params=pltpu.CompilerParams(
            dimension_semantics=("arbitrary", "arbitrary"), vmem_limit_bytes=VMEM_LIMIT),
    )(h3, *o_list, *lse_list, rec3, gl3, w_attn, w_rec, w_out, g_post)


def kernel(x, ffn1_norm_pre, ffn1_norm_post, ffn1_w_gate, ffn1_w_up, ffn1_w_down, mix_norm_pre, mix_norm_post, w_in, rel_bias_table, conv_w, conv_b, lru_w_x, lru_b_x, lru_w_a, lru_b_a, lru_a_param, w_attn_branch, w_rec_branch, w_out, ffn2_norm_pre, ffn2_norm_post, ffn2_w_gate, ffn2_w_up, ffn2_w_down):
    bsz, seq, _ = x.shape
    n = bsz * seq
    depth = w_in.shape[0]
    bkt = jnp.asarray(_bucket_table())
    h = x.reshape(n, D_MODEL)
    for l in range(depth):
        gw = 3 * GROUP_W
        regroup = tuple((g * gw + j * GROUP_W, j * QKV_W + g * GROUP_W, GROUP_W)
                        for g in range(N_GROUPS) for j in range(3))
        jobs = [_CastJob(w_in[l], ((3 * QKV_W, regroup), (4 * LRU_W, ((0, 3 * QKV_W, 4 * LRU_W),)))),
                _plain_cast(lru_w_x[l].reshape(LRU_W, LRU_HD)), _plain_cast(lru_w_a[l].reshape(LRU_W, LRU_HD)),
                _plain_cast(w_attn_branch[l]), _plain_cast(w_rec_branch[l]), _plain_cast(w_out[l]),
                _plain_cast(ffn2_w_gate[l]), _plain_cast(ffn2_w_up[l]), _plain_cast(ffn2_w_down[l])]
        (h, u, u_d4, u_d16, w_qkv, w_xyg, wx, wa, w_attn, w_rec, w_o, w2_gate, w2_up, w2_down) = _ffn(
            h, ffn1_norm_pre[l][None], ffn1_norm_post[l][None],
            ffn1_w_gate[l].astype(BF16), ffn1_w_up[l].astype(BF16), ffn1_w_down[l].astype(BF16),
            g_next=mix_norm_pre[l][None], seq=seq, cast_jobs=jobs)
        h3 = h.reshape(bsz, seq, D_MODEL)
        u3 = u.reshape(bsz, seq, D_MODEL)

        rec, gl = _lru(u3, w_xyg, conv_w[l], conv_b[l][None],
                       wx.reshape(LRU_HEADS, LRU_HD, LRU_HD), lru_b_x[l].reshape(1, LRU_W),
                       wa.reshape(LRU_HEADS, LRU_HD, LRU_HD), lru_b_a[l].reshape(1, LRU_W),
                       lru_a_param[l][None])
        o_list, lse_list = [], []
        for g, u_g in enumerate((u3[:, None], u_d4, u_d16)):
            o_g, lse_g = _attn_group(g, u_g, w_qkv, rel_bias_table, bkt)
            o_list.append(o_g)
            lse_list.append(lse_g)
        h3 = _merge(h3, o_list, lse_list, rec, gl, w_attn, w_rec, w_o, mix_norm_post[l][None])

        h, = _ffn(h3.reshape(n, D_MODEL), ffn2_norm_pre[l][None], ffn2_norm_post[l][None],
                  w2_gate, w2_up, w2_down)
    return h.reshape(bsz, seq, D_MODEL)
```
